```python
import math
import functools
import jax
import jax.numpy as jnp
from jax import lax
import numpy as np

D_MODEL = 1024
BATCH = 4
SEQ = 4096
DEPTH = 2
DEC_BATCH = 32
DEC_SEQ = 4
PAST_LEN = 8192
PAGE_SIZE = 128

DN_HEADS = 4
DN_DK = 128
DN_DV = 128
DN_CONV = 4
DN_QKV = 2 * DN_HEADS * DN_DK + DN_HEADS * DN_DV
RET_HEADS = 4
RET_DK = 128
RET_DV = 128
ROPE_BASE = 10000.0
ATT_HEADS = 4
ATT_DH = 128
IDX_HEADS = 4
IDX_DIM = 64
TOPK_MAX = 256
Q_BLOCK = 128
N_BUCKETS = 32
MAX_DISTANCE = 128
CHUNK = 64
N_BRANCH = 3
BRANCH_W = DN_HEADS * DN_DV
D_FF = 2816
FFN_CONV = 3
EPS = 1e-6
F32 = jnp.float32
IN_SIZES = (DN_QKV, DN_HEADS * DN_DV, DN_HEADS, DN_HEADS,
            RET_HEADS * RET_DK, RET_HEADS * RET_DK, RET_HEADS * RET_DV, RET_HEADS * RET_DV,
            ATT_HEADS * ATT_DH, ATT_HEADS * ATT_DH, ATT_HEADS * ATT_DH,
            IDX_HEADS * IDX_DIM, IDX_DIM, IDX_HEADS, N_BRANCH * D_MODEL)
IN_COLS = sum(IN_SIZES)

kernel_name = 'hybrid_delta_retention_dsa_step'


def _rms(x, g):
    xf = x.astype(F32)
    y = xf * lax.rsqrt(jnp.mean(xf * xf, axis=-1, keepdims=True) + EPS)
    return (y * g.astype(F32)).astype(x.dtype)


def _group_norm(x, g):
    xf = x.astype(F32)
    mu = jnp.mean(xf, axis=-1, keepdims=True)
    xc = xf - mu
    var = jnp.mean(xc * xc, axis=-1, keepdims=True)
    return (xc * lax.rsqrt(var + EPS) * g.astype(F32)).astype(x.dtype)


def _l2norm(x):
    xf = x.astype(F32)
    return (xf * lax.rsqrt(jnp.sum(xf * xf, axis=-1, keepdims=True) + EPS)).astype(x.dtype)


def _split_cols(z):
    offs = np.cumsum(np.array(IN_SIZES))[:-1].tolist()
    return jnp.split(z, offs, axis=-1)


def _causal_conv(x, prev, w):
    width = w.shape[0]
    L = x.shape[1]
    xc = jnp.concatenate([prev.astype(x.dtype), x], axis=1)
    y = sum(xc[:, i:i + L] * w[i] for i in range(width))
    return y, xc[:, L:]


def _rotary(x, pos):
    half = x.shape[-1] // 2
    inv = 1.0 / (ROPE_BASE ** jnp.linspace(0.0, 1.0, half, dtype=F32))
    ang = pos.astype(F32)[:, None] * inv
    cos = jnp.cos(ang)[:, None, :]
    sin = jnp.sin(ang)[:, None, :]
    xf = x.astype(F32)
    x1, x2 = xf[..., :half], xf[..., half:]
    return jnp.concatenate([x1 * cos - x2 * sin, x1 * sin + x2 * cos], axis=-1).astype(x.dtype)


def _chunked_recurrence(q, k, v, g, beta, s0):
    B, L, H, DK = q.shape
    DV = v.shape[-1]
    C = min(CHUNK, L)
    n = -(-L // C)
    pad = n * C - L

    def prep(t):
        t = t.astype(F32)
        t = jnp.pad(t, [(0, 0), (0, pad)] + [(0, 0)] * (t.ndim - 2))
        t = t.reshape((B, n, C) + t.shape[2:])
        return jnp.moveaxis(t, (1, 2), (0, 3))

    qc, kc, vc, gc = prep(q), prep(k), prep(v), prep(g)
    G = jnp.cumsum(gc, axis=-1)
    incl = jnp.tril(jnp.ones((C, C), bool))
    diff = G[..., :, None] - G[..., None, :]
    decay = jnp.where(incl, jnp.exp(jnp.where(incl, diff, 0.0)), 0.0)
    qk = jnp.einsum('nbhid,nbhjd->nbhij', qc, kc) * decay
    if beta is None:
        u, kcum = vc, None
    else:
        bc = prep(beta)
        strict = jnp.tril(jnp.ones((C, C), bool), -1)
        m = jnp.where(strict, jnp.einsum('nbhid,nbhjd->nbhij', kc, kc) * decay * bc[..., :, None], 0.0)
        a = m + jnp.eye(C, dtype=F32)
        rhs = jnp.concatenate([vc * bc[..., None], kc * (bc * jnp.exp(G))[..., None]], axis=-1)
        sol = lax.linalg.triangular_solve(a, rhs, left_side=True, lower=True, unit_diagonal=True)
        u, kcum = sol[..., :DV], sol[..., DV:]
    q_dec = qc * jnp.exp(G)[..., None]
    k_dec = kc * jnp.exp(G[..., -1:] - G)[..., None]
    g_tot = jnp.exp(G[..., -1])[..., None, None]

    def step(s, xs):
        u_c, kcum_c, qd, kd, qk_c, gt = xs
        w = u_c if kcum_c is None else u_c - jnp.einsum('bhck,bhkv->bhcv', kcum_c, s)
        o = jnp.einsum('bhck,bhkv->bhcv', qd, s) + jnp.einsum('bhij,bhjv->bhiv', qk_c, w)
        s = s * gt + jnp.einsum('bhck,bhcv->bhkv', kd, w)
        return s, o

    s_fin, o = lax.scan(step, s0.astype(F32), (u, kcum, q_dec, k_dec, qk, g_tot))
    o = jnp.moveaxis(o, (0, 3), (1, 2)).reshape(B, n * C, H, DV)[:, :L]
    return o.astype(v.dtype), s_fin.astype(s0.dtype)


def _t5_bucket(dist):
    exact = N_BUCKETS // 2
    d = dist.astype(F32)
    large = exact + (jnp.log(jnp.maximum(d, 1.0) / exact) / math.log(MAX_DISTANCE / exact)
                     * (N_BUCKETS - exact)).astype(jnp.int32)
    large = jnp.minimum(large, N_BUCKETS - 1)
    return jnp.where(dist < exact, dist, large)


def _index_select(qi, wi, ki, qpos, kpos, topk):
    s = jnp.einsum('bqhd,bld->bqhl', qi.astype(F32), ki.astype(F32)) * IDX_DIM ** -0.5
    score = jnp.einsum('bqhl,bqh->bql', jax.nn.relu(s), wi.astype(F32)) * IDX_HEADS ** -0.5
    adm = kpos[None, None, :] <= qpos[None, :, None]
    score = jnp.where(adm, score, -jnp.inf)
    _, idx = lax.top_k(score, topk)
    valid = idx <= qpos[None, :, None]
    return idx, valid


def _attend_selected(q, ksel, vsel, qpos, kpos, valid, rel_bias):
    logits = jnp.einsum('bqhd,bqkhd->bqhk', q, ksel, preferred_element_type=F32) * ATT_DH ** -0.5
    bucket = _t5_bucket(jnp.maximum(qpos[None, :, None] - kpos, 0))
    bias = jnp.moveaxis(rel_bias.astype(F32)[bucket], -1, 2)
    logits = jnp.where(valid[:, :, None, :], logits + bias, -jnp.inf)
    p = jax.nn.softmax(logits, axis=-1)
    return jnp.einsum('bqhk,bqkhd->bqhd', p.astype(vsel.dtype), vsel)


def _gather_rows(a, i):
    return jax.vmap(lambda ab, ib: ab[ib])(a, i)


def _dsa_prompt(q, k, v, qi, ki, wi, rel_bias):
    B, S, H, Dh = q.shape
    topk = min(TOPK_MAX, S // 4)
    kpos = jnp.arange(S)

    def block(i):
        st = i * Q_BLOCK
        sl = lambda t: lax.dynamic_slice_in_dim(t, st, Q_BLOCK, axis=1)
        qpos = st + jnp.arange(Q_BLOCK)
        idx, valid = _index_select(sl(qi), sl(wi), ki, qpos, kpos, topk)
        return _attend_selected(sl(q), _gather_rows(k, idx), _gather_rows(v, idx), qpos, idx, valid, rel_bias)

    o = lax.map(block, jnp.arange(S // Q_BLOCK))
    return jnp.moveaxis(o, 0, 1).reshape(B, S, H, Dh)


def _dsa_sample(q, k, v, qi, ki, wi, layer, cache_k, cache_v, cache_kidx, page_table, rel_bias):
    DB, DS, H, Dh = q.shape
    past = page_table.shape[1] * PAGE_SIZE
    L = past + DS
    topk = min(TOPK_MAX, L // 4)
    ki_past = cache_kidx[layer, page_table].reshape(DB, past, IDX_DIM)
    ki_all = jnp.concatenate([ki_past.astype(ki.dtype), ki], axis=1)
    qpos = past + jnp.arange(DS)
    idx, valid = _index_select(qi, wi, ki_all, qpos, jnp.arange(L), topk)
    pidx = jnp.minimum(idx, past - 1)
    phys = _gather_rows(page_table, pidx // PAGE_SIZE) * PAGE_SIZE + pidx % PAGE_SIZE
    nidx = jnp.clip(idx - past, 0, DS - 1)
    in_past = (idx < past)[..., None, None]

    def pick(pool, new):
        flat = pool.reshape((pool.shape[0], -1) + pool.shape[3:])
        return jnp.where(in_past, flat[layer, phys].astype(new.dtype), _gather_rows(new, nidx))

    return _attend_selected(q, pick(cache_k, k), pick(cache_v, v), qpos, idx, valid, rel_bias)


def _layer(x, pos, dn_conv0, dn_s0, ret_s0, ffn_conv0, attend,
           norm_mix, w_in, dn_conv_w, dn_a_log, dn_dt_bias, dn_norm, ret_norm,
           w_branch, w_o, norm_ffn, w_up, ffn_conv_w, w_down):
    B, L, _ = x.shape
    u = _rms(x, norm_mix)
    (dn_qkv, dn_z, dn_b, dn_a, r_q, r_k, r_v, r_g,
     a_q, a_k, a_v, i_q, i_k, i_w, gate) = _split_cols(jnp.einsum('bld,dc->blc', u, w_in))
    heads = lambda t, h: t.reshape(B, L, h, -1)

    c, dn_conv_new = _causal_conv(dn_qkv, dn_conv0, dn_conv_w)
    q, k, v = jnp.split(jax.nn.silu(c), [DN_HEADS * DN_DK, 2 * DN_HEADS * DN_DK], axis=-1)
    q = _l2norm(heads(q, DN_HEADS)) * DN_DK ** -0.5
    k = _l2norm(heads(k, DN_HEADS))
    beta = jax.nn.sigmoid(dn_b.astype(F32))
    g = -jnp.exp(dn_a_log.astype(F32)) * jax.nn.softplus(dn_a.astype(F32) + dn_dt_bias.astype(F32))
    o_a, dn_s_new = _chunked_recurrence(q, k, heads(v, DN_HEADS), g, beta, dn_s0)
    o_a = _rms(o_a, dn_norm) * jax.nn.silu(heads(dn_z, DN_HEADS))

    rq = _rotary(heads(r_q, RET_HEADS), pos) * RET_DK ** -0.5
    rk = _rotary(heads(r_k, RET_HEADS), pos)
    log_gamma = jnp.log1p(-jnp.exp2(-5.0 - jnp.arange(RET_HEADS, dtype=F32)))
    g_r = jnp.broadcast_to(log_gamma, (B, L, RET_HEADS))
    o_b, ret_s_new = _chunked_recurrence(rq, rk, heads(r_v, RET_HEADS), g_r, None, ret_s0)
    o_b = _group_norm(o_b, ret_norm) * jax.nn.silu(heads(r_g, RET_HEADS))

    k_c = heads(a_k, ATT_HEADS)
    v_c = heads(a_v, ATT_HEADS)
    o_c = attend(heads(a_q, ATT_HEADS), k_c, v_c, heads(i_q, IDX_HEADS), i_k, i_w)

    br = jnp.stack([o_a.reshape(B, L, BRANCH_W), o_b.reshape(B, L, BRANCH_W),
                    o_c.reshape(B, L, BRANCH_W)], axis=2)
    br = jnp.einsum('blnc,ncd->blnd', br, w_branch)
    gates = jax.nn.sigmoid(gate.reshape(B, L, N_BRANCH, D_MODEL))
    h = x + jnp.einsum('bld,de->ble', jnp.sum(gates * br, axis=2), w_o)

    a, ffn_conv_new = _causal_conv(jnp.einsum('bld,df->blf', _rms(h, norm_ffn), w_up), ffn_conv0, ffn_conv_w)
    a_gate, a_val = jnp.split(a, 2, axis=-1)
    y = h + jnp.einsum('blf,fd->bld', jax.nn.silu(a_gate) * a_val, w_down)
    return y, (dn_conv_new, dn_s_new, ret_s_new, k_c, v_c, i_k, ffn_conv_new)


def setup_inputs(seed: int = 0) -> dict:
    key = jax.random.key(seed)
    ks = jax.random.split(key, 32)
    nrm = lambda i, shape, scale: jax.random.normal(ks[i], shape, F32) * scale
    n_pages = PAST_LEN // PAGE_SIZE
    n_used = DEC_BATCH * n_pages
    n_phys = n_used + max(1, n_used // 4)
    page_table = jax.random.permutation(ks[0], n_phys)[:n_used].reshape(DEC_BATCH, n_pages).astype(jnp.int32)
    dt = jnp.exp(jax.random.uniform(ks[1], (DEPTH, DN_HEADS), F32, math.log(1e-3), math.log(1e-1)))
    dn_dt_bias = dt + jnp.log(-jnp.expm1(-dt))
    dn_a_log = jnp.log(jax.random.uniform(ks[2], (DEPTH, DN_HEADS), F32, 1.0, 16.0))
    return {
        'x_prompt': nrm(3, (BATCH, SEQ, D_MODEL), 1.0),
        'x_sample': nrm(4, (DEC_BATCH, DEC_SEQ, D_MODEL), 1.0),
        'cache_k': nrm(5, (DEPTH, n_phys, PAGE_SIZE, ATT_HEADS, ATT_DH), 1.0),
        'cache_v': nrm(6, (DEPTH, n_phys, PAGE_SIZE, ATT_HEADS, ATT_DH), 1.0),
        'cache_kidx': nrm(7, (DEPTH, n_phys, PAGE_SIZE, IDX_DIM), 1.0),
        'state_dn_conv': nrm(8, (DEPTH, DEC_BATCH, DN_CONV - 1, DN_QKV), 1.0),
        'state_dn': nrm(9, (DEPTH, DEC_BATCH, DN_HEADS, DN_DK, DN_DV), 0.1),
        'state_ret': nrm(10, (DEPTH, DEC_BATCH, RET_HEADS, RET_DK, RET_DV), 0.3),
        'state_ffn_conv': nrm(11, (DEPTH, DEC_BATCH, FFN_CONV - 1, 2 * D_FF), 1.0),
        'page_table': page_table,
        'norm_mix': 1.0 + nrm(12, (DEPTH, D_MODEL), 0.02),
        'w_in': nrm(13, (DEPTH, D_MODEL, IN_COLS), D_MODEL ** -0.5),
        'dn_conv_w': nrm(14, (DEPTH, DN_CONV, DN_QKV), DN_CONV ** -0.5),
        'dn_a_log': dn_a_log,
        'dn_dt_bias': dn_dt_bias,
        'dn_norm': 1.0 + nrm(15, (DEPTH, DN_DV), 0.02),
        'ret_norm': 1.0 + nrm(16, (DEPTH, RET_HEADS, RET_DV), 0.02),
        'rel_bias': nrm(17, (N_BUCKETS, ATT_HEADS), 0.5),
        'w_branch': nrm(18, (DEPTH, N_BRANCH, BRANCH_W, D_MODEL), BRANCH_W ** -0.5),
        'w_o': nrm(19, (DEPTH, D_MODEL, D_MODEL), D_MODEL ** -0.5),
        'norm_ffn': 1.0 + nrm(20, (DEPTH, D_MODEL), 0.02),
        'w_up': nrm(21, (DEPTH, D_MODEL, 2 * D_FF), D_MODEL ** -0.5),
        'ffn_conv_w': nrm(22, (DEPTH, FFN_CONV, 2 * D_FF), FFN_CONV ** -0.5),
        'w_down': nrm(23, (DEPTH, D_FF, D_MODEL), D_FF ** -0.5),
        'norm_final': 1.0 + nrm(24, (D_MODEL,), 0.02),
    }


def reference(x_prompt, x_sample, cache_k, cache_v, cache_kidx, state_dn_conv, state_dn, state_ret,
              state_ffn_conv, page_table, norm_mix, w_in, dn_conv_w, dn_a_log, dn_dt_bias, dn_norm,
              ret_norm, rel_bias, w_branch, w_o, norm_ffn, w_up, ffn_conv_w, w_down, norm_final):
    B, S, _ = x_prompt.shape
    DB, DS, _ = x_sample.shape
    pos_p = jnp.arange(S)
    pos_s = PAST_LEN + jnp.arange(DS)
    xp, xs = x_prompt, x_sample
    dt = x_prompt.dtype
    p_states, s_states = [], []
    attend_p = functools.partial(_dsa_prompt, rel_bias=rel_bias)
    for l in range(DEPTH):
        lw = (norm_mix[l], w_in[l], dn_conv_w[l], dn_a_log[l], dn_dt_bias[l], dn_norm[l], ret_norm[l],
              w_branch[l], w_o[l], norm_ffn[l], w_up[l], ffn_conv_w[l], w_down[l])
        xp, st = _layer(xp, pos_p,
                        jnp.zeros((B, DN_CONV - 1, DN_QKV), dt),
                        jnp.zeros((B, DN_HEADS, DN_DK, DN_DV), dt),
                        jnp.zeros((B, RET_HEADS, RET_DK, RET_DV), dt),
                        jnp.zeros((B, FFN_CONV - 1, 2 * D_FF), dt),
                        attend_p, *lw)
        p_states.append(st)
        attend_s = functools.partial(_dsa_sample, layer=l, cache_k=cache_k, cache_v=cache_v,
                                     cache_kidx=cache_kidx, page_table=page_table, rel_bias=rel_bias)
        xs, st = _layer(xs, pos_s, state_dn_conv[l], state_dn[l], state_ret[l], state_ffn_conv[l],
                        attend_s, *lw)
        s_states.append(st)
    y_prompt = _rms(xp, norm_final)
    y_sample = _rms(xs, norm_final)
    stk = lambda states, i: jnp.stack([st[i] for st in states])
    p_dn_conv, p_dn, p_ret = stk(p_states, 0), stk(p_states, 1), stk(p_states, 2)
    p_k, p_v, p_kidx, p_ffn_conv = stk(p_states, 3), stk(p_states, 4), stk(p_states, 5), stk(p_states, 6)
    s_dn_conv, s_dn, s_ret = stk(s_states, 0), stk(s_states, 1), stk(s_states, 2)
    s_k, s_v, s_kidx, s_ffn_conv = stk(s_states, 3), stk(s_states, 4), stk(s_states, 5), stk(s_states, 6)
    return (y_prompt, y_sample, p_dn_conv, p_dn, p_ret, p_k, p_v, p_kidx, p_ffn_conv,
            s_dn_conv, s_dn, s_ret, s_k, s_v, s_kidx, s_ffn_conv)
```

```python
import functools
import math

import numpy as np
import jax
import jax.numpy as jnp
from jax import lax
from jax.experimental import pallas as pl
from jax.experimental.pallas import tpu as pltpu

D_MODEL = 1024
DEPTH = 2
PAST_LEN = 8192
PAGE_SIZE = 128
DN_HEADS = 4
DN_DK = 128
DN_DV = 128
DN_CONV = 4
DN_QKV = 2 * DN_HEADS * DN_DK + DN_HEADS * DN_DV
RET_HEADS = 4
RET_DK = 128
RET_DV = 128
ROPE_BASE = 10000.0
ATT_HEADS = 4
ATT_DH = 128
IDX_HEADS = 4
IDX_DIM = 64
TOPK_MAX = 256
N_BUCKETS = 32
MAX_DISTANCE = 128
N_BRANCH = 3
BRANCH_W = DN_HEADS * DN_DV
D_FF = 2816
FFN_CONV = 3
EPS = 1e-6
F32 = jnp.float32
BF16 = jnp.bfloat16
IN_SIZES = (DN_QKV, DN_HEADS * DN_DV, DN_HEADS, DN_HEADS,
            RET_HEADS * RET_DK, RET_HEADS * RET_DK, RET_HEADS * RET_DV, RET_HEADS * RET_DV,
            ATT_HEADS * ATT_DH, ATT_HEADS * ATT_DH, ATT_HEADS * ATT_DH,
            IDX_HEADS * IDX_DIM, IDX_DIM, IDX_HEADS, N_BRANCH * D_MODEL)

Z_DNQKV = 0
Z_DNZ = 1536
Z_GATE = 2048
Z_RQ, Z_RK, Z_RV, Z_RG = 5120, 5632, 6144, 6656
Z_AQ, Z_AK, Z_AV = 7168, 7680, 8192
Z_IQ = 8704
Z_SM = 8960
SM_IK, SM_IW, SM_DNB, SM_DNA = 0, 64, 68, 72
Z_COLS = 9216

LANES = 128
SUBLANES = 8
CHUNK = 128
SROWS = 8
S_LO, S_HI = 3, 7
NEG = -1e30
IMIN = -2 ** 31
VMEM_LIMIT = 56 * 1024 * 1024


def _cparams(sem):
    return pltpu.CompilerParams(dimension_semantics=sem, vmem_limit_bytes=VMEM_LIMIT)


def _sigmoid(x):
    return 1.0 / (1.0 + jnp.exp(-x))


def _silu(x):
    return x * _sigmoid(x)


def _softplus(x):
    return jnp.maximum(x, 0.0) + jnp.log(1.0 + jnp.exp(-jnp.abs(x)))


_DIMS = {'nn': (((1,), (0,)), ((), ())), 'nt': (((1,), (1,)), ((), ())), 'tn': (((0,), (0,)), ((), ()))}


def _split_bf16(a, n):
    parts = []
    r = a
    for i in range(n):
        p = r.astype(BF16)
        parts.append(p)
        if i + 1 < n:
            r = r - p.astype(F32)
    return parts


def _mm(a, b, dims='nn', mode='bf16'):
    dn = _DIMS[dims]
    dg = lambda x, y: lax.dot_general(x, y, dn, preferred_element_type=F32)
    if mode == 'bf16':
        return dg(a.astype(BF16), b.astype(BF16))
    if mode == 'x3':
        ah, al = _split_bf16(a, 2)
        bh, bl = _split_bf16(b, 2)
        return dg(ah, bh) + dg(ah, bl) + dg(al, bh)
    if mode == 'l01':
        ab = a.astype(BF16)
        b1, b2, b3 = _split_bf16(b, 3)
        return dg(ab, b1) + dg(ab, b2) + dg(ab, b3)
    raise ValueError(mode)


def _rms_mm_kernel(x_ref, g_ref, w_ref, o_ref, u_ref):
    @pl.when(pl.program_id(1) == 0)
    def _():
        x = x_ref[...]
        r = lax.rsqrt(jnp.mean(x * x, axis=-1, keepdims=True) + EPS)
        u_ref[...] = (x * r * g_ref[...]).astype(u_ref.dtype)

    o_ref[...] = jnp.dot(u_ref[...], w_ref[...], preferred_element_type=F32)


def _rms_matmul(x, g, w, tm, tn):
    M, K = x.shape
    N = w.shape[1]
    return pl.pallas_call(
        _rms_mm_kernel,
        grid=(M // tm, N // tn),
        in_specs=[pl.BlockSpec((tm, K), lambda i, j: (i, 0)),
                  pl.BlockSpec((1, K), lambda i, j: (0, 0)),
                  pl.BlockSpec((K, tn), lambda i, j: (0, j))],
        out_specs=pl.BlockSpec((tm, tn), lambda i, j: (i, j)),
        out_shape=jax.ShapeDtypeStruct((M, N), F32),
        scratch_shapes=[pltpu.VMEM((tm, K), BF16)],
        compiler_params=_cparams(("parallel", "arbitrary")),
        name="rms_matmul",
    )(x, g.reshape(1, K), w)


def _conv_tile(x, prev8, w, width):
    y = x * w[width - 1:width, :]
    for s in range(1, width):
        y = y + pltpu.roll(x, s, axis=0) * w[width - 1 - s:width - s, :]
    x0 = x[0:SUBLANES, :]
    rid = lax.broadcasted_iota(jnp.int32, x0.shape, 0)
    y0 = x0 * w[width - 1:width, :]
    for s in range(1, width):
        xs = jnp.where(rid < s, pltpu.roll(prev8, s, axis=0), pltpu.roll(x0, s, axis=0))
        y0 = y0 + xs * w[width - 1 - s:width - s, :]
    return y, y0


def _idiv(x, n):
    assert n & (n - 1) == 0
    return lax.shift_right_arithmetic(x, jnp.int32(n.bit_length() - 1))


def _imod(x, n):
    assert n & (n - 1) == 0
    return x & jnp.int32(n - 1)


def _chunk_masks(C, seg):
    ri = lax.broadcasted_iota(jnp.int32, (C, C), 0)
    ci = lax.broadcasted_iota(jnp.int32, (C, C), 1)
    if seg == C:
        return ri >= ci, ri > ci, None
    same = _idiv(ri, seg) == _idiv(ci, seg)
    return (ri >= ci) & same, (ri > ci) & same, same


def _valid_col(C, seg, lo, hi):
    r = _imod(lax.broadcasted_iota(jnp.int32, (C, 1), 0), seg)
    return jnp.where((r >= lo) & (r < hi), 1.0, 0.0)


def _tri_inv(m, span):
    C = m.shape[0]
    eye = jnp.where(lax.broadcasted_iota(jnp.int32, (C, C), 0) == lax.broadcasted_iota(jnp.int32, (C, C), 1),
                    1.0, 0.0)
    inv = eye - m
    p = m
    n = 2
    while n < span:
        p = _mm(p, p, 'nn', 'x3')
        inv = inv + _mm(inv, p, 'nn', 'x3')
        n *= 2
    return inv


def _state_update(S_scr, h, u, kcum, qd, kd, qk, gtot, C, seg):
    nseg = C // seg
    ws, o1s = [], []
    for sg in range(nseg):
        rs = slice(sg * seg, (sg + 1) * seg)
        S = S_scr[sg, h]
        if kcum is None:
            ws.append(u[rs])
        else:
            ws.append(u[rs] - _mm(kcum[rs], S))
        o1s.append(_mm(qd[rs], S))
    w = ws[0] if nseg == 1 else jnp.concatenate(ws, axis=0)
    o1 = o1s[0] if nseg == 1 else jnp.concatenate(o1s, axis=0)
    o = o1 + _mm(qk, w)
    rowid = lax.broadcasted_iota(jnp.int32, (C, 1), 0)
    for sg in range(nseg):
        kdm = kd if nseg == 1 else jnp.where(_idiv(rowid, seg) == sg, kd, 0.0)
        gt = jnp.exp(gtot[sg * seg:sg * seg + 1, :])
        S_scr[sg, h] = S_scr[sg, h] * gt + _mm(kdm, w, 'tn')
    return o


def _dn_kernel(qkv_ref, dz_ref, sm_ref, cw_ref, hp_ref, nrm_ref, s0_ref, o_ref, sfin_ref,
               S_scr, prev_scr, c_scr, *, C, seg, lo, hi):
    t = pl.program_id(1)
    TL = qkv_ref.shape[0]
    H, DK = DN_HEADS, DN_DK
    masked = seg != C

    @pl.when(t == 0)
    def _():
        S_scr[...] = s0_ref[...]
        prev_scr[...] = jnp.zeros_like(prev_scr)

    x = qkv_ref[...]
    y, y0 = _conv_tile(x, prev_scr[...], cw_ref[...], DN_CONV)
    c_scr[...] = _silu(y)
    c_scr[0:SUBLANES, :] = _silu(y0)
    prev_scr[...] = x[TL - SUBLANES:TL, :]

    lowm, strictm, same = _chunk_masks(C, seg)
    ltri = jnp.where(lowm, 1.0, 0.0)
    valid = _valid_col(C, seg, lo, hi) if masked else None
    span = (hi - lo) if masked else C
    a_coef = -jnp.exp(hp_ref[0:1, :])
    dtb = hp_ref[1:2, :]

    for cidx in range(TL // C):
        r0 = cidx * C
        cc = c_scr[r0:r0 + C, :]
        sm = sm_ref[r0:r0 + C, :]
        g128 = a_coef * _softplus(sm + dtb)
        b128 = _sigmoid(sm)
        if masked:
            g128 = g128 * valid
            b128 = b128 * valid
        Gc128 = _mm(ltri, g128, 'nn', 'l01')
        if masked:
            Gt128 = _mm(jnp.where(same, 1.0, 0.0), g128, 'nn', 'l01')
        else:
            Gt128 = jnp.broadcast_to(Gc128[C - 1:C, :], Gc128.shape)
        GT = Gc128.T
        for h in range(H):
            q = cc[:, h * DK:(h + 1) * DK]
            k = cc[:, (H + h) * DK:(H + h + 1) * DK]
            v = cc[:, (2 * H + h) * DK:(2 * H + h + 1) * DK]
            q = q * lax.rsqrt(jnp.sum(q * q, axis=-1, keepdims=True) + EPS) * DK ** -0.5
            k = k * lax.rsqrt(jnp.sum(k * k, axis=-1, keepdims=True) + EPS)
            if masked:
                k = k * valid
            Gc = Gc128[:, SM_DNA + h:SM_DNA + h + 1]
            Gr = GT[SM_DNA + h:SM_DNA + h + 1, :]
            Gt = Gt128[:, SM_DNA + h:SM_DNA + h + 1]
            bc = b128[:, SM_DNB + h:SM_DNB + h + 1]
            decay = jnp.where(lowm, jnp.exp(jnp.where(lowm, Gc - Gr, 0.0)), 0.0)
            kk = _mm(k, k, 'nt', 'x3')
            qk = _mm(q, k, 'nt') * decay
            m = jnp.where(strictm, kk * decay * bc, 0.0)
            ainv = _tri_inv(m, span)
            eG = jnp.exp(Gc)
            u = _mm(ainv, v * bc, 'nn', 'x3')
            kcum = _mm(ainv, k * (bc * eG), 'nn', 'x3')
            o = _state_update(S_scr, h, u, kcum, q * eG, k * jnp.exp(Gt - Gc), qk, Gt, C, seg)
            on = o * lax.rsqrt(jnp.mean(o * o, axis=-1, keepdims=True) + EPS) * nrm_ref[...]
            zg = dz_ref[r0:r0 + C, h * DN_DV:(h + 1) * DN_DV]
            o_ref[r0:r0 + C, h * DN_DV:(h + 1) * DN_DV] = on * _silu(zg)

    @pl.when(t == pl.num_programs(1) - 1)
    def _():
        sfin_ref[...] = S_scr[...]


def _dn_call(z, conv_w, hp, nrm, s0, n_outer, TL, seg, lo, hi):
    M = z.shape[0]
    nt = M // (n_outer * TL)
    nseg = CHUNK // seg
    rowmap = lambda cb: (lambda b, t: (b * nt + t, cb))
    kern = functools.partial(_dn_kernel, C=CHUNK, seg=seg, lo=lo, hi=hi)
    return pl.pallas_call(
        kern,
        grid=(n_outer, nt),
        in_specs=[pl.BlockSpec((TL, DN_QKV), rowmap(Z_DNQKV // DN_QKV)),
                  pl.BlockSpec((TL, BRANCH_W), rowmap(Z_DNZ // BRANCH_W)),
                  pl.BlockSpec((TL, LANES), rowmap(Z_SM // LANES)),
                  pl.BlockSpec((DN_CONV, DN_QKV), lambda b, t: (0, 0)),
                  pl.BlockSpec((SUBLANES, LANES), lambda b, t: (0, 0)),
                  pl.BlockSpec((1, DN_DV), lambda b, t: (0, 0)),
                  pl.BlockSpec((nseg, DN_HEADS, DN_DK, DN_DV), lambda b, t: (b, 0, 0, 0))],
        out_specs=[pl.BlockSpec((TL, BRANCH_W), lambda b, t: (b * nt + t, 0)),
                   pl.BlockSpec((nseg, DN_HEADS, DN_DK, DN_DV), lambda b, t: (b, 0, 0, 0))],
        out_shape=[jax.ShapeDtypeStruct((M, BRANCH_W), F32),
                   jax.ShapeDtypeStruct(s0.shape, F32)],
        scratch_shapes=[pltpu.VMEM((nseg, DN_HEADS, DN_DK, DN_DV), F32),
                        pltpu.VMEM((SUBLANES, DN_QKV), F32),
                        pltpu.VMEM((TL, DN_QKV), F32)],
        compiler_params=_cparams(("arbitrary", "arbitrary")),
        name="dn",
    )(z, z, z, conv_w, hp, nrm, s0)


_LOG_GAMMA = [float(np.log1p(-np.exp2(-5.0 - h))) for h in range(RET_HEADS)]


def _ret_kernel(q_ref, k_ref, v_ref, g_ref, cos_ref, sin_ref, nrm_ref, s0_ref, o_ref, sfin_ref,
                S_scr, *, C, seg, lo, hi):
    t = pl.program_id(1)
    TL = q_ref.shape[0]
    H, DK = RET_HEADS, RET_DK
    masked = seg != C

    @pl.when(t == 0)
    def _():
        S_scr[...] = s0_ref[...]

    lowm, _, _ = _chunk_masks(C, seg)
    ri = lax.broadcasted_iota(jnp.int32, (C, 1), 0)
    ci = lax.broadcasted_iota(jnp.int32, (1, C), 1)
    if masked:
        valid = _valid_col(C, seg, lo, hi)
        cnt_c = jnp.clip(_imod(ri, seg) - lo + 1, 0, hi - lo).astype(F32)
        cnt_r = jnp.clip(_imod(ci, seg) - lo + 1, 0, hi - lo).astype(F32)
        cnt_t = float(hi - lo)
    else:
        valid = None
        cnt_c = (ri + 1).astype(F32)
        cnt_r = (ci + 1).astype(F32)
        cnt_t = float(C)

    for cidx in range(TL // C):
        r0 = cidx * C
        cosf = cos_ref[r0:r0 + C, :]
        sins = sin_ref[r0:r0 + C, :]
        for h in range(H):
            cs = slice(h * DK, (h + 1) * DK)
            q = q_ref[r0:r0 + C, cs]
            k = k_ref[r0:r0 + C, cs]
            v = v_ref[r0:r0 + C, cs]
            q = (q * cosf + pltpu.roll(q, DK // 2, axis=1) * sins) * DK ** -0.5
            k = k * cosf + pltpu.roll(k, DK // 2, axis=1) * sins
            if masked:
                v = v * valid
            lg = _LOG_GAMMA[h]
            Gc = cnt_c * lg
            decay = jnp.where(lowm, jnp.exp(jnp.where(lowm, (cnt_c - cnt_r) * lg, 0.0)), 0.0)
            qk = _mm(q, k, 'nt') * decay
            Gt = jnp.full((C, 1), cnt_t * lg, F32)
            o = _state_update(S_scr, h, v, None, q * jnp.exp(Gc), k * jnp.exp(Gt - Gc), qk, Gt, C, seg)
            mu = jnp.mean(o, axis=-1, keepdims=True)
            oc = o - mu
            var = jnp.mean(oc * oc, axis=-1, keepdims=True)
            on = oc * lax.rsqrt(var + EPS) * nrm_ref[h:h + 1, :]
            o_ref[r0:r0 + C, cs] = on * _silu(g_ref[r0:r0 + C, cs])

    @pl.when(t == pl.num_programs(1) - 1)
    def _():
        sfin_ref[...] = S_scr[...]


def _ret_call(z, cosf, sins, nrm, s0, n_outer, TL, seg, lo, hi):
    M = z.shape[0]
    nt = M // (n_outer * TL)
    nseg = CHUNK // seg
    W = RET_HEADS * RET_DK
    rowmap = lambda cb: (lambda b, t: (b * nt + t, cb))
    kern = functools.partial(_ret_kernel, C=CHUNK, seg=seg, lo=lo, hi=hi)
    return pl.pallas_call(
        kern,
        grid=(n_outer, nt),
        in_specs=[pl.BlockSpec((TL, W), rowmap(Z_RQ // W)),
                  pl.BlockSpec((TL, W), rowmap(Z_RK // W)),
                  pl.BlockSpec((TL, W), rowmap(Z_RV // W)),
                  pl.BlockSpec((TL, W), rowmap(Z_RG // W)),
                  pl.BlockSpec((TL, RET_DK), lambda b, t: (t, 0)),
                  pl.BlockSpec((TL, RET_DK), lambda b, t: (t, 0)),
                  pl.BlockSpec((RET_HEADS, RET_DV), lambda b, t: (0, 0)),
                  pl.BlockSpec((nseg, RET_HEADS, RET_DK, RET_DV), lambda b, t: (b, 0, 0, 0))],
        out_specs=[pl.BlockSpec((TL, W), lambda b, t: (b * nt + t, 0)),
                   pl.BlockSpec((nseg, RET_HEADS, RET_DK, RET_DV), lambda b, t: (b, 0, 0, 0))],
        out_shape=[jax.ShapeDtypeStruct((M, W), F32),
                   jax.ShapeDtypeStruct(s0.shape, F32)],
        scratch_shapes=[pltpu.VMEM((nseg, RET_HEADS, RET_DK, RET_DV), F32)],
        compiler_params=_cparams(("arbitrary", "arbitrary")),
        name="ret",
    )(z, z, z, z, cosf, sins, nrm, s0)


def _f2key(x):
    b = lax.bitcast_convert_type(x + 0.0, jnp.int32)
    return jnp.where(b >= 0, b, b ^ jnp.int32(0x7FFFFFFF))


def _t5_bucket(d):
    exact = N_BUCKETS // 2
    df = d.astype(F32)
    large = exact + (jnp.log(jnp.maximum(df, 1.0) / exact) / math.log(MAX_DISTANCE / exact)
                     * (N_BUCKETS - exact)).astype(jnp.int32)
    large = jnp.minimum(large, N_BUCKETS - 1)
    return jnp.where(d < exact, d, large)


def _bias_from_dist(d, rb_ref, h):
    bk = _t5_bucket(d)
    r = jnp.zeros(d.shape, F32)
    for jb in range(N_BUCKETS):
        r = jnp.where(bk == jb, rb_ref[jb, h], r)
    return r


def _threshold_search(count_ge, shape, total, kf):
    zero = jnp.zeros(shape, jnp.int32)
    c0 = count_ge(zero)
    ok0 = c0 >= kf
    T = jnp.where(ok0, 0, IMIN).astype(jnp.int32)
    cT = jnp.where(ok0, c0, total)

    def body(it, carry):
        T, cT = carry
        cand = T + lax.shift_left(jnp.int32(1), jnp.int32(30) - it)
        c = count_ge(cand)
        ok = c >= kf
        return jnp.where(ok, cand, T), jnp.where(ok, c, cT)

    return lax.fori_loop(0, 31, body, (T, cT))


def _fold_lanes(x):
    f = x[:, 0:LANES]
    for u in range(1, x.shape[1] // LANES):
        f = f + x[:, u * LANES:(u + 1) * LANES]
    return f


def _dsa_prompt_kernel(rb_ref, q_ref, qi_ref, smq_ref, k_ref, v_ref, smk_ref, o_ref, keys_scr, *, TQ, topk):
    i = pl.program_id(1)
    KC = TQ
    nk = i + 1
    kf = float(topk)
    qi = qi_ref[...]
    w4 = smq_ref[:, SM_IW:SM_IW + IDX_HEADS]
    rowi = lax.broadcasted_iota(jnp.int32, (TQ, KC), 0)
    coli = lax.broadcasted_iota(jnp.int32, (TQ, KC), 1)
    lane = lax.broadcasted_iota(jnp.int32, (KC, LANES), 1)

    def p1(j, c):
        r0 = pl.multiple_of(j * KC, KC)
        k_lo = jnp.where(lane < IDX_DIM, smk_ref[pl.ds(r0, KC), :], 0.0)
        k_hi = pltpu.roll(k_lo, IDX_DIM, axis=1)
        acc = jnp.zeros((TQ, KC), F32)
        for h in range(IDX_HEADS):
            slab = qi[:, (h // 2) * LANES:(h // 2 + 1) * LANES]
            s = _mm(slab, k_lo if h % 2 == 0 else k_hi, 'nt', 'x3')
            acc = acc + jnp.maximum(s, 0.0) * w4[:, h:h + 1]
        key = _f2key(acc * (IDX_DIM ** -0.5 * IDX_HEADS ** -0.5))
        keys_scr[j] = jnp.where(coli + r0 <= rowi + i * TQ, key, IMIN)
        return c

    lax.fori_loop(0, nk, p1, 0)

    def count_ge(cand):
        def body(j, part):
            return part + _fold_lanes(jnp.where(keys_scr[j] >= cand, 1.0, 0.0))
        part = lax.fori_loop(0, nk, body, jnp.zeros((TQ, LANES), F32))
        return jnp.sum(part, axis=1, keepdims=True)

    total = (nk * KC).astype(F32)
    T, cT = _threshold_search(count_ge, (TQ, 1), total, kf)

    ties = jnp.max(jnp.where((cT > kf) & (T > IMIN), 1.0, 0.0)) > 0.0

    @pl.when(ties)
    def _():
        need = kf - count_ge(T + 1)
        triu = jnp.where(lax.broadcasted_iota(jnp.int32, (KC, KC), 0)
                         <= lax.broadcasted_iota(jnp.int32, (KC, KC), 1), 1.0, 0.0).astype(BF16)

        def body(j, seen):
            kj = keys_scr[j]
            eq = kj == T
            pre = jnp.dot(jnp.where(eq, 1.0, 0.0).astype(BF16), triu, preferred_element_type=F32)
            keys_scr[j] = jnp.where(eq & (seen + pre > need), IMIN, kj)
            return seen + pre[:, KC - 1:KC]

        lax.fori_loop(0, nk, body, jnp.zeros((TQ, 1), F32))

    Tp = jnp.maximum(T, IMIN + 1)

    cidx = lax.broadcasted_iota(jnp.int32, (1, 2 * KC), 1)
    e = jnp.where(cidx < KC, cidx, cidx - 2 * KC)
    scale = ATT_DH ** -0.5
    jprev = jnp.maximum(i - 1, 0)

    def toeplitz(r):
        y = pltpu.roll(jnp.broadcast_to(r, (TQ, 2 * KC)), 0, 1, stride=1, stride_axis=0)
        return y[:, 0:KC]

    for h in range(ATT_HEADS):
        cs = slice(h * ATT_DH, (h + 1) * ATT_DH)
        qh = q_ref[:, cs].astype(BF16)

        def step(j, bias, extra_ok, carry):
            m, l, acc = carry
            r0 = pl.multiple_of(j * KC, KC)
            kh = k_ref[pl.ds(r0, KC), cs]
            vh = v_ref[pl.ds(r0, KC), cs]
            lg = lax.dot_general(qh, kh, _DIMS['nt'], preferred_element_type=F32) * scale + bias
            sel = keys_scr[j] >= Tp
            if extra_ok is not None:
                sel = sel & extra_ok
            lg = jnp.where(sel, lg, NEG)
            m_new = jnp.maximum(m, jnp.max(lg, axis=1, keepdims=True))
            p = jnp.exp(lg - m_new)
            corr = jnp.exp(m - m_new)
            l = l * corr + jnp.sum(p, axis=1, keepdims=True)
            acc = acc * corr + jnp.dot(p.astype(BF16), vh, preferred_element_type=F32)
            return m_new, l, acc

        carry = (jnp.full((TQ, 1), NEG, F32), jnp.zeros((TQ, 1), F32), jnp.zeros((TQ, ATT_DH), F32))
        far_bias = rb_ref[N_BUCKETS - 1, h]
        carry = lax.fori_loop(0, jprev, lambda j, c: step(j, far_bias, None, c), carry)
        b_prev = toeplitz(_bias_from_dist(jnp.maximum(KC - e, 0), rb_ref, h))
        carry = step(jprev, b_prev, (rowi * 0 + i) >= 1, carry)
        b_diag = toeplitz(_bias_from_dist(jnp.maximum(-e, 0), rb_ref, h))
        m, l, acc = step(i, b_diag, None, carry)
        o_ref[:, cs] = acc / l


def _dsa_prompt_call(rel_bias, z, kb, vb, B, L, TQ, topk):
    nq = L // TQ
    W = ATT_HEADS * ATT_DH
    kern = functools.partial(_dsa_prompt_kernel, TQ=TQ, topk=topk)
    return pl.pallas_call(
        kern,
        grid=(B, nq),
        in_specs=[pl.BlockSpec(memory_space=pltpu.SMEM),
                  pl.BlockSpec((TQ, W), lambda b, i: (b * nq + i, Z_AQ // W)),
                  pl.BlockSpec((TQ, IDX_HEADS * IDX_DIM), lambda b, i: (b * nq + i, Z_IQ // (IDX_HEADS * IDX_DIM))),
                  pl.BlockSpec((TQ, LANES), lambda b, i: (b * nq + i, Z_SM // LANES)),
                  pl.BlockSpec((L, W), lambda b, i: (b, 0)),
                  pl.BlockSpec((L, W), lambda b, i: (b, 0)),
                  pl.BlockSpec((L, LANES), lambda b, i: (b, Z_SM // LANES))],
        out_specs=pl.BlockSpec((TQ, W), lambda b, i: (b * nq + i, 0)),
        out_shape=jax.ShapeDtypeStruct((B * L, W), F32),
        scratch_shapes=[pltpu.VMEM((nq, TQ, TQ), jnp.int32)],
        compiler_params=_cparams(("arbitrary", "arbitrary")),
        name="dsa_prompt",
    )(rel_bias, z, z, z, kb, vb, z)


def _dsa_s_index_kernel(pt_ref, qi_ref, sm_ref, kp_ref, keys_ref, tp_ref, *, NP, topk, past):
    p = pl.program_id(1)
    kf = float(topk)
    R = SROWS
    qi = qi_ref[...]
    w4 = sm_ref[:, SM_IW:SM_IW + IDX_HEADS]
    rowi = lax.broadcasted_iota(jnp.int32, (R, LANES), 0)
    coli = lax.broadcasted_iota(jnp.int32, (R, LANES), 1)
    zpad = jnp.zeros((PAGE_SIZE, LANES - IDX_DIM), F32)

    def score_keys(kblk64):
        k_lo = jnp.concatenate([kblk64, zpad], axis=1)
        k_hi = jnp.concatenate([zpad, kblk64], axis=1)
        acc = jnp.zeros((R, LANES), F32)
        for h in range(IDX_HEADS):
            slab = qi[:, (h // 2) * LANES:(h // 2 + 1) * LANES]
            s = _mm(slab, k_lo if h % 2 == 0 else k_hi, 'nt', 'x3')
            acc = acc + jnp.maximum(s, 0.0) * w4[:, h:h + 1]
        return _f2key(acc * (IDX_DIM ** -0.5 * IDX_HEADS ** -0.5))

    keys_ref[p] = score_keys(kp_ref[...])

    @pl.when(p == NP - 1)
    def _():
        knew = jnp.concatenate([sm_ref[:, SM_IK:SM_IK + IDX_DIM],
                                jnp.zeros((PAGE_SIZE - R, IDX_DIM), F32)], axis=0)
        key = score_keys(knew)
        ok = (coli >= S_LO) & (coli < S_HI) & (coli <= rowi)
        keys_ref[NP] = jnp.where(ok, key, IMIN)

        def count_ge(cand):
            ind = jnp.where(keys_ref[...] >= cand[None], 1.0, 0.0)
            return jnp.sum(jnp.sum(ind, axis=0), axis=1, keepdims=True)

        total = jnp.full((R, 1), float((NP + 1) * LANES), F32)
        T, cT = _threshold_search(count_ge, (R, 1), total, kf)
        rid = lax.broadcasted_iota(jnp.int32, (R, 1), 0)
        token_row = (rid >= S_LO) & (rid < S_HI)
        ties = jnp.max(jnp.where((cT > kf) & (T > IMIN) & token_row, 1.0, 0.0)) > 0.0

        @pl.when(ties)
        def _():
            need = kf - count_ge(T + 1)
            triu = jnp.where(lax.broadcasted_iota(jnp.int32, (LANES, LANES), 0)
                             <= lax.broadcasted_iota(jnp.int32, (LANES, LANES), 1), 1.0, 0.0).astype(BF16)

            def body(j, seen):
                kj = keys_ref[j]
                eq = kj == T
                pre = jnp.dot(jnp.where(eq, 1.0, 0.0).astype(BF16), triu, preferred_element_type=F32)
                keys_ref[j] = jnp.where(eq & (seen + pre > need), IMIN, kj)
                return seen + pre[:, LANES - 1:LANES]

            lax.fori_loop(0, NP + 1, body, jnp.zeros((R, 1), F32))

        tp_ref[...] = jnp.broadcast_to(jnp.maximum(T, IMIN + 1), (R, LANES))


def _dsa_s_index_call(page_table, z, cache_kidx, layer, DB, NP, topk, past):
    kern = functools.partial(_dsa_s_index_kernel, NP=NP, topk=topk, past=past)
    QW = IDX_HEADS * IDX_DIM
    grid_spec = pltpu.PrefetchScalarGridSpec(
        num_scalar_prefetch=1,
        grid=(DB, NP),
        in_specs=[pl.BlockSpec((SROWS, QW), lambda b, p, pt: (b, Z_IQ // QW)),
                  pl.BlockSpec((SROWS, LANES), lambda b, p, pt: (b, Z_SM // LANES)),
                  pl.BlockSpec((None, None, PAGE_SIZE, IDX_DIM),
                               lambda b, p, pt: (layer, pt[b * NP + p], 0, 0))],
        out_specs=[pl.BlockSpec((None, NP + 1, SROWS, LANES), lambda b, p, pt: (b, 0, 0, 0)),
                   pl.BlockSpec((None, SROWS, LANES), lambda b, p, pt: (b, 0, 0))],
    )
    return pl.pallas_call(
        kern,
        grid_spec=grid_spec,
        out_shape=[jax.ShapeDtypeStruct((DB, NP + 1, SROWS, LANES), jnp.int32),
                   jax.ShapeDtypeStruct((DB, SROWS, LANES), jnp.int32)],
        compiler_params=_cparams(("arbitrary", "arbitrary")),
        name="dsa_s_index",
    )(page_table.reshape(-1), z, z, cache_kidx)


def _dsa_s_attend_kernel(pt_ref, rb_ref, q_ref, keys_ref, tp_ref, kp_ref, vp_ref, kn_ref, vn_ref, o_ref,
                         m_scr, l_scr, acc_scr, *, NP, past):
    p = pl.program_id(1)
    R = SROWS
    rowi = lax.broadcasted_iota(jnp.int32, (R, LANES), 0)
    coli = lax.broadcasted_iota(jnp.int32, (R, LANES), 1)
    qpos = past + rowi - S_LO
    scale = ATT_DH ** -0.5
    Tp = tp_ref[...]

    @pl.when(p == 0)
    def _():
        m_scr[...] = jnp.full(m_scr.shape, NEG, F32)
        l_scr[...] = jnp.zeros_like(l_scr)
        acc_scr[...] = jnp.zeros_like(acc_scr)

    def tile(keys, kpos, kblk, vblk):
        sel = keys >= Tp
        dist = jnp.maximum(qpos - kpos, 0)
        for h in range(ATT_HEADS):
            cs = slice(h * ATT_DH, (h + 1) * ATT_DH)
            qh = q_ref[:, cs].astype(BF16)
            lg = lax.dot_general(qh, kblk[:, cs].astype(BF16), _DIMS['nt'], preferred_element_type=F32)
            lg = jnp.where(sel, lg * scale + _bias_from_dist(dist, rb_ref, h), NEG)
            m = m_scr[h]
            m_new = jnp.maximum(m, jnp.max(lg, axis=1, keepdims=True))
            pr = jnp.exp(lg - m_new)
            corr = jnp.exp(m - m_new)
            l_scr[h] = l_scr[h] * corr + jnp.sum(pr, axis=1, keepdims=True)
            acc_scr[h] = acc_scr[h] * corr + jnp.dot(pr.astype(BF16), vblk[:, cs].astype(BF16),
                                                     preferred_element_type=F32)
            m_scr[h] = m_new

    tile(keys_ref[0], p * PAGE_SIZE + coli, kp_ref[...], vp_ref[...])

    @pl.when(p == NP - 1)
    def _():
        zrows = jnp.zeros((PAGE_SIZE - R, ATT_HEADS * ATT_DH), F32)
        tile(keys_ref[1], past + coli - S_LO,
             jnp.concatenate([kn_ref[...], zrows], axis=0), jnp.concatenate([vn_ref[...], zrows], axis=0))
        for h in range(ATT_HEADS):
            o_ref[:, h * ATT_DH:(h + 1) * ATT_DH] = acc_scr[h] / l_scr[h]


def _dsa_s_attend_call(page_table, rel_bias, z, keys, tp, cache_k, cache_v, layer, DB, NP, past):
    W = ATT_HEADS * ATT_DH
    kern = functools.partial(_dsa_s_attend_kernel, NP=NP, past=past)
    grid_spec = pltpu.PrefetchScalarGridSpec(
        num_scalar_prefetch=1,
        grid=(DB, NP),
        in_specs=[pl.BlockSpec(memory_space=pltpu.SMEM),
                  pl.BlockSpec((SROWS, W), lambda b, p, pt: (b, Z_AQ // W)),
                  pl.BlockSpec((None, 2, SROWS, LANES), lambda b, p, pt: (b * NP + p, 0, 0, 0)),
                  pl.BlockSpec((None, SROWS, LANES), lambda b, p, pt: (b, 0, 0)),
                  pl.BlockSpec((None, None, PAGE_SIZE, W), lambda b, p, pt: (layer, pt[b * NP + p], 0, 0)),
                  pl.BlockSpec((None, None, PAGE_SIZE, W), lambda b, p, pt: (layer, pt[b * NP + p], 0, 0)),
                  pl.BlockSpec((SROWS, W), lambda b, p, pt: (b, Z_AK // W)),
                  pl.BlockSpec((SROWS, W), lambda b, p, pt: (b, Z_AV // W))],
        out_specs=pl.BlockSpec((SROWS, W), lambda b, p, pt: (b, 0)),
        scratch_shapes=[pltpu.VMEM((ATT_HEADS, SROWS, 1), F32),
                        pltpu.VMEM((ATT_HEADS, SROWS, 1), F32),
                        pltpu.VMEM((ATT_HEADS, SROWS, ATT_DH), F32)],
    )
    new_tile = jnp.broadcast_to(keys[:, NP:NP + 1], (DB, NP, SROWS, LANES))
    paired = jnp.stack([keys[:, :NP], new_tile], axis=2).reshape(DB * NP, 2, SROWS, LANES)
    return pl.pallas_call(
        kern,
        grid_spec=grid_spec,
        out_shape=jax.ShapeDtypeStruct((DB * SROWS, W), F32),
        compiler_params=_cparams(("arbitrary", "arbitrary")),
        name="dsa_s_attend",
    )(page_table.reshape(-1), rel_bias, z, paired, tp, cache_k, cache_v, z, z)


def _merge_kernel(oa_ref, ob_ref, oc_ref, g0_ref, g1_ref, g2_ref, x_ref, wb_ref, wo_ref, h_ref):
    acc = None
    for i, (o_ref, g_ref) in enumerate(((oa_ref, g0_ref), (ob_ref, g1_ref), (oc_ref, g2_ref))):
        br = jnp.dot(o_ref[...].astype(BF16), wb_ref[i], preferred_element_type=F32)
        term = _sigmoid(g_ref[...]) * br
        acc = term if acc is None else acc + term
    h_ref[...] = x_ref[...] + jnp.dot(acc.astype(BF16), wo_ref[...], preferred_element_type=F32)


def _merge_call(oa, ob, oc, z, x, wb, wo, tm):
    M = x.shape[0]
    W = BRANCH_W
    g0 = Z_GATE // D_MODEL
    row = lambda c: (lambda i: (i, c))
    return pl.pallas_call(
        _merge_kernel,
        grid=(M // tm,),
        in_specs=[pl.BlockSpec((tm, W), row(0)), pl.BlockSpec((tm, W), row(0)), pl.BlockSpec((tm, W), row(0)),
                  pl.BlockSpec((tm, D_MODEL), row(g0)), pl.BlockSpec((tm, D_MODEL), row(g0 + 1)),
                  pl.BlockSpec((tm, D_MODEL), row(g0 + 2)),
                  pl.BlockSpec((tm, D_MODEL), row(0)),
                  pl.BlockSpec((N_BRANCH, W, D_MODEL), lambda i: (0, 0, 0)),
                  pl.BlockSpec((D_MODEL, D_MODEL), lambda i: (0, 0))],
        out_specs=pl.BlockSpec((tm, D_MODEL), row(0)),
        out_shape=jax.ShapeDtypeStruct((M, D_MODEL), F32),
        compiler_params=_cparams(("parallel",)),
        name="merge",
    )(oa, ob, oc, z, z, z, x, wb, wo)


def _ffn_down_kernel(a_ref, h_ref, cw_ref, wd_ref, gf_ref, y_ref, prev_scr, act_scr, *, final_norm):
    t = pl.program_id(1)
    tm = a_ref.shape[0]
    FH = D_FF // 2

    @pl.when(t == 0)
    def _():
        prev_scr[...] = jnp.zeros_like(prev_scr)

    acc = h_ref[...]
    for c in range(2):
        gs = slice(c * FH, (c + 1) * FH)
        vs = slice(D_FF + c * FH, D_FF + (c + 1) * FH)
        yg, yg0 = _conv_tile(a_ref[:, gs], prev_scr[:, gs], cw_ref[:, gs], FFN_CONV)
        yv, yv0 = _conv_tile(a_ref[:, vs], prev_scr[:, vs], cw_ref[:, vs], FFN_CONV)
        act_scr[...] = (_silu(yg) * yv).astype(BF16)
        act_scr[0:2 * SUBLANES, :] = jnp.concatenate(
            [_silu(yg0) * yv0, _silu(yg[SUBLANES:2 * SUBLANES]) * yv[SUBLANES:2 * SUBLANES]], axis=0).astype(BF16)
        acc = acc + jnp.dot(act_scr[...], wd_ref[gs, :], preferred_element_type=F32)
    prev_scr[...] = a_ref[tm - SUBLANES:tm, :]
    if final_norm:
        acc = acc * lax.rsqrt(jnp.mean(acc * acc, axis=-1, keepdims=True) + EPS) * gf_ref[...]
    y_ref[...] = acc


def _ffn_down_call(a, h, conv_w, wd, gf, n_outer, tm, final_norm):
    M = h.shape[0]
    nt = M // (n_outer * tm)
    kern = functools.partial(_ffn_down_kernel, final_norm=final_norm)
    return pl.pallas_call(
        kern,
        grid=(n_outer, nt),
        in_specs=[pl.BlockSpec((tm, 2 * D_FF), lambda b, t: (b * nt + t, 0)),
                  pl.BlockSpec((tm, D_MODEL), lambda b, t: (b * nt + t, 0)),
                  pl.BlockSpec((FFN_CONV, 2 * D_FF), lambda b, t: (0, 0)),
                  pl.BlockSpec((D_FF, D_MODEL), lambda b, t: (0, 0)),
                  pl.BlockSpec((1, D_MODEL), lambda b, t: (0, 0))],
        out_specs=pl.BlockSpec((tm, D_MODEL), lambda b, t: (b * nt + t, 0)),
        out_shape=jax.ShapeDtypeStruct((M, D_MODEL), F32),
        scratch_shapes=[pltpu.VMEM((SUBLANES, 2 * D_FF), F32),
                        pltpu.VMEM((tm, D_FF // 2), BF16)],
        compiler_params=_cparams(("arbitrary", "arbitrary")),
        name="ffn_down",
    )(a, h, conv_w, wd, gf.reshape(1, D_MODEL))


def _reorder_w_in(w):
    offs = np.cumsum(np.array(IN_SIZES))[:-1].tolist()
    (dn_qkv, dn_z, dn_b, dn_a, r_q, r_k, r_v, r_g, a_q, a_k, a_v, i_q, i_k, i_w, gate) = jnp.split(w, offs, axis=-1)
    pad = jnp.zeros((w.shape[0], Z_COLS - Z_SM - (IDX_DIM + IDX_HEADS + 2 * DN_HEADS)), w.dtype)
    return jnp.concatenate([dn_qkv, dn_z, gate, r_q, r_k, r_v, r_g, a_q, a_k, a_v, i_q,
                            i_k, i_w, dn_b, dn_a, pad], axis=-1).astype(BF16)


def _rope_tables(pos):
    half = RET_DK // 2
    inv = 1.0 / (ROPE_BASE ** jnp.linspace(0.0, 1.0, half, dtype=F32))
    ang = pos.astype(F32)[:, None] * inv
    cos, sin = jnp.cos(ang), jnp.sin(ang)
    return jnp.concatenate([cos, cos], axis=-1), jnp.concatenate([-sin, sin], axis=-1)


def _dn_params(a_log, dt_bias):
    hp = jnp.zeros((SUBLANES, LANES), F32)
    hp = hp.at[0, SM_DNA:SM_DNA + DN_HEADS].set(a_log.astype(F32))
    hp = hp.at[1, SM_DNA:SM_DNA + DN_HEADS].set(dt_bias.astype(F32))
    return hp


def _layer_weights(l, norm_mix, w_in, dn_conv_w, dn_a_log, dn_dt_bias, dn_norm, ret_norm,
                   w_branch, w_o, norm_ffn, w_up, ffn_conv_w, w_down):
    return dict(norm_mix=norm_mix[l], w_in=_reorder_w_in(w_in[l]), dn_conv_w=dn_conv_w[l],
                hp=_dn_params(dn_a_log[l], dn_dt_bias[l]), dn_norm=dn_norm[l].reshape(1, DN_DV),
                ret_norm=ret_norm[l], wb=w_branch[l].astype(BF16), wo=w_o[l].astype(BF16),
                norm_ffn=norm_ffn[l], w_up=w_up[l].astype(BF16), ffn_conv_w=ffn_conv_w[l],
                wd=w_down[l].astype(BF16))


def _mix_and_ffn(x, z, oa, ob, oc, lw, norm_final, final, n_outer, tm_merge, tm_up, tn_up, tm_down, ffn_state=None):
    h = _merge_call(oa, ob, oc, z, x, lw['wb'], lw['wo'], tm_merge)
    a = _rms_matmul(h, lw['norm_ffn'], lw['w_up'], tm_up, tn_up)
    a_raw = a
    if ffn_state is not None:
        DB = ffn_state.shape[0]
        a = a.reshape(DB, SROWS, 2 * D_FF).at[:, S_LO - (FFN_CONV - 1):S_LO].set(ffn_state)
        a = a.reshape(DB * SROWS, 2 * D_FF)
    y = _ffn_down_call(a, h, lw['ffn_conv_w'], lw['wd'], norm_final, n_outer, tm_down, final)
    return y, a_raw


def kernel(x_prompt, x_sample, cache_k, cache_v, cache_kidx, state_dn_conv, state_dn, state_ret,
           state_ffn_conv, page_table, norm_mix, w_in, dn_conv_w, dn_a_log, dn_dt_bias, dn_norm,
           ret_norm, rel_bias, w_branch, w_o, norm_ffn, w_up, ffn_conv_w, w_down, norm_final):
    B, S, D = x_prompt.shape
    DB, DS, _ = x_sample.shape
    depth = w_in.shape[0]
    NP = page_table.shape[1]
    past = NP * PAGE_SIZE
    n_phys = cache_k.shape[1]
    W = ATT_HEADS * ATT_DH
    assert DS == S_HI - S_LO and S % CHUNK == 0 and (DB * SROWS) % CHUNK == 0

    TL = 256 if S % 256 == 0 else CHUNK
    TQ = 256 if S % 256 == 0 else CHUNK
    tm_p = 512 if (B * S) % 512 == 0 else CHUNK
    tm_d = 256 if S % 256 == 0 else CHUNK
    MS = DB * SROWS
    NG = MS // CHUNK
    seg_per = CHUNK // SROWS

    xp = x_prompt.reshape(B * S, D)
    xs = jnp.zeros((DB, SROWS, D), F32).at[:, S_LO:S_HI].set(x_sample).reshape(MS, D)
    cos_p, sin_p = _rope_tables(jnp.arange(S))
    pos_s = past + (jnp.arange(CHUNK) % SROWS) - S_LO
    cos_s, sin_s = _rope_tables(pos_s)
    ck = cache_k.reshape(depth, n_phys, PAGE_SIZE, W)
    cv = cache_v.reshape(depth, n_phys, PAGE_SIZE, W)
    rb = rel_bias.astype(F32)
    zeros_p = jnp.zeros((B, DN_HEADS, DN_DK, DN_DV), F32)
    topk_p = min(TOPK_MAX, S // 4)
    topk_s = min(TOPK_MAX, (past + DS) // 4)

    p_states, s_states = [], []
    for l in range(depth):
        lw = _layer_weights(l, norm_mix, w_in, dn_conv_w, dn_a_log, dn_dt_bias, dn_norm, ret_norm,
                            w_branch, w_o, norm_ffn, w_up, ffn_conv_w, w_down)
        final = l == depth - 1

        z = _rms_matmul(xp, lw['norm_mix'], lw['w_in'], tm_p, 1024)
        oa, dn_s = _dn_call(z, lw['dn_conv_w'], lw['hp'], lw['dn_norm'], zeros_p, B, TL, CHUNK, 0, CHUNK)
        ob, ret_s = _ret_call(z, cos_p, sin_p, lw['ret_norm'], zeros_p, B, TL, CHUNK, 0, CHUNK)
        k_c = z[:, Z_AK:Z_AK + W]
        v_c = z[:, Z_AV:Z_AV + W]
        oc = _dsa_prompt_call(rb, z, k_c.astype(BF16), v_c.astype(BF16), B, S, TQ, topk_p)
        xp, a_up = _mix_and_ffn(xp, z, oa, ob, oc, lw, norm_final, final, B, tm_p, tm_p, 1408, tm_d)
        z3 = z.reshape(B, S, Z_COLS)
        p_states.append((z3[:, S - (DN_CONV - 1):, Z_DNQKV:Z_DNQKV + DN_QKV], dn_s, ret_s,
                         k_c.reshape(B, S, ATT_HEADS, ATT_DH), v_c.reshape(B, S, ATT_HEADS, ATT_DH),
                         z3[:, :, Z_SM + SM_IK:Z_SM + SM_IK + IDX_DIM],
                         a_up.reshape(B, S, 2 * D_FF)[:, S - (FFN_CONV - 1):]))

        zs = _rms_matmul(xs, lw['norm_mix'], lw['w_in'], MS, 1024)
        zs3 = zs.reshape(DB, SROWS, Z_COLS)
        zs_conv = zs3.at[:, S_LO - (DN_CONV - 1):S_LO, Z_DNQKV:Z_DNQKV + DN_QKV].set(state_dn_conv[l])
        zs_conv = zs_conv.reshape(MS, Z_COLS)
        oa, dn_s = _dn_call(zs_conv, lw['dn_conv_w'], lw['hp'], lw['dn_norm'], state_dn[l], NG, CHUNK,
                            SROWS, S_LO, S_HI)
        ob, ret_s = _ret_call(zs, cos_s, sin_s, lw['ret_norm'], state_ret[l], NG, CHUNK, SROWS, S_LO, S_HI)
        keys, tp = _dsa_s_index_call(page_table, zs, cache_kidx, l, DB, NP, topk_s, past)
        oc = _dsa_s_attend_call(page_table, rb, zs, keys, tp, ck, cv, l, DB, NP, past)
        xs, a_up = _mix_and_ffn(xs, zs, oa, ob, oc, lw, norm_final, final, 1, MS, MS, 1408, MS,
                                ffn_state=state_ffn_conv[l])
        tok = zs3[:, S_LO:S_HI]
        s_states.append((tok[:, DS - (DN_CONV - 1):, Z_DNQKV:Z_DNQKV + DN_QKV], dn_s, ret_s,
                         tok[:, :, Z_AK:Z_AK + W].reshape(DB, DS, ATT_HEADS, ATT_DH),
                         tok[:, :, Z_AV:Z_AV + W].reshape(DB, DS, ATT_HEADS, ATT_DH),
                         tok[:, :, Z_SM + SM_IK:Z_SM + SM_IK + IDX_DIM],
                         a_up.reshape(DB, SROWS, 2 * D_FF)[:, S_HI - (FFN_CONV - 1):S_HI]))

    y_prompt = xp.reshape(B, S, D)
    y_sample = xs.reshape(DB, SROWS, D)[:, S_LO:S_HI]
    stk = lambda states, i: jnp.stack([st[i] for st in states])
    return (y_prompt, y_sample) + tuple(stk(p_states, i) for i in range(7)) + tuple(stk(s_states, i) for i in range(7))
```

```python
import functools
import math

import numpy as np
import jax
import jax.numpy as jnp
from jax import lax
from jax.experimental import pallas as pl
from jax.experimental.pallas import tpu as pltpu

D_MODEL = 1024
DEPTH = 2
PAST_LEN = 8192
PAGE_SIZE = 128
DN_HEADS = 4
DN_DK = 128
DN_DV = 128
DN_CONV = 4
DN_QKV = 2 * DN_HEADS * DN_DK + DN_HEADS * DN_DV
RET_HEADS = 4
RET_DK = 128
RET_DV = 128
ROPE_BASE = 10000.0
ATT_HEADS = 4
ATT_DH = 128
IDX_HEADS = 4
IDX_DIM = 64
TOPK_MAX = 256
N_BUCKETS = 32
MAX_DISTANCE = 128
N_BRANCH = 3
BRANCH_W = DN_HEADS * DN_DV
D_FF = 2816
FFN_CONV = 3
EPS = 1e-6
F32 = jnp.float32
BF16 = jnp.bfloat16
IN_SIZES = (DN_QKV, DN_HEADS * DN_DV, DN_HEADS, DN_HEADS,
            RET_HEADS * RET_DK, RET_HEADS * RET_DK, RET_HEADS * RET_DV, RET_HEADS * RET_DV,
            ATT_HEADS * ATT_DH, ATT_HEADS * ATT_DH, ATT_HEADS * ATT_DH,
            IDX_HEADS * IDX_DIM, IDX_DIM, IDX_HEADS, N_BRANCH * D_MODEL)

Z_DNQKV = 0
Z_DNZ = 1536
Z_GATE = 2048
Z_RQ, Z_RK, Z_RV, Z_RG = 5120, 5632, 6144, 6656
Z_AQ, Z_AK, Z_AV = 7168, 7680, 8192
Z_IQ = 8704
Z_SM = 8960
SM_IK, SM_IW, SM_DNB, SM_DNA = 0, 64, 68, 72
Z_COLS = 9216

LANES = 128
SUBLANES = 8
CHUNK = 128
SROWS = 8
S_LO, S_HI = 3, 7
NEG = -1e30
IMIN = -2 ** 31
VMEM_LIMIT = 56 * 1024 * 1024


def _cparams(sem):
    return pltpu.CompilerParams(dimension_semantics=sem, vmem_limit_bytes=VMEM_LIMIT)


def _sigmoid(x):
    return 1.0 / (1.0 + jnp.exp(-x))


def _silu(x):
    return x * _sigmoid(x)


def _softplus(x):
    return jnp.maximum(x, 0.0) + jnp.log(1.0 + jnp.exp(-jnp.abs(x)))


_DIMS = {'nn': (((1,), (0,)), ((), ())), 'nt': (((1,), (1,)), ((), ())), 'tn': (((0,), (0,)), ((), ()))}


def _split_bf16(a, n):
    parts = []
    r = a
    for i in range(n):
        p = r.astype(BF16)
        parts.append(p)
        if i + 1 < n:
            r = r - p.astype(F32)
    return parts


def _mm(a, b, dims='nn', mode='bf16'):
    dn = _DIMS[dims]
    dg = lambda x, y: lax.dot_general(x, y, dn, preferred_element_type=F32)
    if mode == 'bf16':
        return dg(a.astype(BF16), b.astype(BF16))
    if mode == 'x3':
        ah, al = _split_bf16(a, 2)
        bh, bl = _split_bf16(b, 2)
        return dg(ah, bh) + dg(ah, bl) + dg(al, bh)
    if mode == 'l01':
        ab = a.astype(BF16)
        b1, b2, b3 = _split_bf16(b, 3)
        return dg(ab, b1) + dg(ab, b2) + dg(ab, b3)
    raise ValueError(mode)


def _rms_mm_kernel(x_ref, g_ref, w_ref, o_ref, u_ref):
    @pl.when(pl.program_id(1) == 0)
    def _():
        x = x_ref[...]
        r = lax.rsqrt(jnp.mean(x * x, axis=-1, keepdims=True) + EPS)
        u_ref[...] = (x * r * g_ref[...]).astype(u_ref.dtype)

    o_ref[...] = jnp.dot(u_ref[...], w_ref[...], preferred_element_type=F32)


def _rms_matmul(x, g, w, tm, tn):
    M, K = x.shape
    N = w.shape[1]
    return pl.pallas_call(
        _rms_mm_kernel,
        grid=(M // tm, N // tn),
        in_specs=[pl.BlockSpec((tm, K), lambda i, j: (i, 0)),
                  pl.BlockSpec((1, K), lambda i, j: (0, 0)),
                  pl.BlockSpec((K, tn), lambda i, j: (0, j))],
        out_specs=pl.BlockSpec((tm, tn), lambda i, j: (i, j)),
        out_shape=jax.ShapeDtypeStruct((M, N), F32),
        scratch_shapes=[pltpu.VMEM((tm, K), BF16)],
        compiler_params=_cparams(("parallel", "arbitrary")),
        name="rms_matmul",
    )(x, g.reshape(1, K), w)


def _conv_tile(x, prev8, w, width):
    y = x * w[width - 1:width, :]
    for s in range(1, width):
        y = y + pltpu.roll(x, s, axis=0) * w[width - 1 - s:width - s, :]
    x0 = x[0:SUBLANES, :]
    rid = lax.broadcasted_iota(jnp.int32, x0.shape, 0)
    y0 = x0 * w[width - 1:width, :]
    for s in range(1, width):
        xs = jnp.where(rid < s, pltpu.roll(prev8, s, axis=0), pltpu.roll(x0, s, axis=0))
        y0 = y0 + xs * w[width - 1 - s:width - s, :]
    return y, y0


def _idiv(x, n):
    assert n & (n - 1) == 0
    return lax.shift_right_arithmetic(x, jnp.int32(n.bit_length() - 1))


def _imod(x, n):
    assert n & (n - 1) == 0
    return x & jnp.int32(n - 1)


def _chunk_masks(C, seg):
    ri = lax.broadcasted_iota(jnp.int32, (C, C), 0)
    ci = lax.broadcasted_iota(jnp.int32, (C, C), 1)
    if seg == C:
        return ri >= ci, ri > ci, None
    same = _idiv(ri, seg) == _idiv(ci, seg)
    return (ri >= ci) & same, (ri > ci) & same, same


def _valid_col(C, seg, lo, hi):
    r = _imod(lax.broadcasted_iota(jnp.int32, (C, 1), 0), seg)
    return jnp.where((r >= lo) & (r < hi), 1.0, 0.0)


def _tri_inv(m, span):
    C = m.shape[0]
    eye = jnp.where(lax.broadcasted_iota(jnp.int32, (C, C), 0) == lax.broadcasted_iota(jnp.int32, (C, C), 1),
                    1.0, 0.0)
    inv = eye - m
    p = m
    n = 2
    while n < span:
        p = _mm(p, p, 'nn', 'x3')
        inv = inv + _mm(inv, p, 'nn', 'x3')
        n *= 2
    return inv


def _state_update(S_scr, h, u, kcum, qd, kd, qk, gtot, C, seg):
    nseg = C // seg
    ws, o1s = [], []
    for sg in range(nseg):
        rs = slice(sg * seg, (sg + 1) * seg)
        S = S_scr[sg, h]
        if kcum is None:
            ws.append(u[rs])
        else:
            ws.append(u[rs] - _mm(kcum[rs], S))
        o1s.append(_mm(qd[rs], S))
    w = ws[0] if nseg == 1 else jnp.concatenate(ws, axis=0)
    o1 = o1s[0] if nseg == 1 else jnp.concatenate(o1s, axis=0)
    o = o1 + _mm(qk, w)
    rowid = lax.broadcasted_iota(jnp.int32, (C, 1), 0)
    for sg in range(nseg):
        kdm = kd if nseg == 1 else jnp.where(_idiv(rowid, seg) == sg, kd, 0.0)
        gt = jnp.exp(gtot[sg * seg:sg * seg + 1, :])
        S_scr[sg, h] = S_scr[sg, h] * gt + _mm(kdm, w, 'tn')
    return o


def _dn_kernel(qkv_ref, dz_ref, sm_ref, cw_ref, hp_ref, nrm_ref, s0_ref, o_ref, sfin_ref,
               S_scr, prev_scr, c_scr, *, C, seg, lo, hi):
    t = pl.program_id(1)
    TL = qkv_ref.shape[0]
    H, DK = DN_HEADS, DN_DK
    masked = seg != C

    @pl.when(t == 0)
    def _():
        S_scr[...] = s0_ref[...]
        prev_scr[...] = jnp.zeros_like(prev_scr)

    x = qkv_ref[...]
    y, y0 = _conv_tile(x, prev_scr[...], cw_ref[...], DN_CONV)
    c_scr[...] = _silu(y)
    c_scr[0:SUBLANES, :] = _silu(y0)
    prev_scr[...] = x[TL - SUBLANES:TL, :]

    lowm, strictm, same = _chunk_masks(C, seg)
    ltri = jnp.where(lowm, 1.0, 0.0)
    valid = _valid_col(C, seg, lo, hi) if masked else None
    span = (hi - lo) if masked else C
    a_coef = -jnp.exp(hp_ref[0:1, :])
    dtb = hp_ref[1:2, :]

    for cidx in range(TL // C):
        r0 = cidx * C
        cc = c_scr[r0:r0 + C, :]
        sm = sm_ref[r0:r0 + C, :]
        g128 = a_coef * _softplus(sm + dtb)
        b128 = _sigmoid(sm)
        if masked:
            g128 = g128 * valid
            b128 = b128 * valid
        Gc128 = _mm(ltri, g128, 'nn', 'l01')
        if masked:
            Gt128 = _mm(jnp.where(same, 1.0, 0.0), g128, 'nn', 'l01')
        else:
            Gt128 = jnp.broadcast_to(Gc128[C - 1:C, :], Gc128.shape)
        GT = Gc128.T
        for h in range(H):
            q = cc[:, h * DK:(h + 1) * DK]
            k = cc[:, (H + h) * DK:(H + h + 1) * DK]
            v = cc[:, (2 * H + h) * DK:(2 * H + h + 1) * DK]
            q = q * lax.rsqrt(jnp.sum(q * q, axis=-1, keepdims=True) + EPS) * DK ** -0.5
            k = k * lax.rsqrt(jnp.sum(k * k, axis=-1, keepdims=True) + EPS)
            if masked:
                k = k * valid
            Gc = Gc128[:, SM_DNA + h:SM_DNA + h + 1]
            Gr = GT[SM_DNA + h:SM_DNA + h + 1, :]
            Gt = Gt128[:, SM_DNA + h:SM_DNA + h + 1]
            bc = b128[:, SM_DNB + h:SM_DNB + h + 1]
            decay = jnp.where(lowm, jnp.exp(jnp.where(lowm, Gc - Gr, 0.0)), 0.0)
            kk = _mm(k, k, 'nt', 'x3')
            qk = _mm(q, k, 'nt') * decay
            m = jnp.where(strictm, kk * decay * bc, 0.0)
            ainv = _tri_inv(m, span)
            eG = jnp.exp(Gc)
            u = _mm(ainv, v * bc, 'nn', 'x3')
            kcum = _mm(ainv, k * (bc * eG), 'nn', 'x3')
            o = _state_update(S_scr, h, u, kcum, q * eG, k * jnp.exp(Gt - Gc), qk, Gt, C, seg)
            on = o * lax.rsqrt(jnp.mean(o * o, axis=-1, keepdims=True) + EPS) * nrm_ref[...]
            zg = dz_ref[r0:r0 + C, h * DN_DV:(h + 1) * DN_DV]
            o_ref[r0:r0 + C, h * DN_DV:(h + 1) * DN_DV] = on * _silu(zg)

    @pl.when(t == pl.num_programs(1) - 1)
    def _():
        sfin_ref[...] = S_scr[...]


def _dn_call(z, conv_w, hp, nrm, s0, n_outer, TL, seg, lo, hi):
    M = z.shape[0]
    nt = M // (n_outer * TL)
    nseg = CHUNK // seg
    rowmap = lambda cb: (lambda b, t: (b * nt + t, cb))
    kern = functools.partial(_dn_kernel, C=CHUNK, seg=seg, lo=lo, hi=hi)
    return pl.pallas_call(
        kern,
        grid=(n_outer, nt),
        in_specs=[pl.BlockSpec((TL, DN_QKV), rowmap(Z_DNQKV // DN_QKV)),
                  pl.BlockSpec((TL, BRANCH_W), rowmap(Z_DNZ // BRANCH_W)),
                  pl.BlockSpec((TL, LANES), rowmap(Z_SM // LANES)),
                  pl.BlockSpec((DN_CONV, DN_QKV), lambda b, t: (0, 0)),
                  pl.BlockSpec((SUBLANES, LANES), lambda b, t: (0, 0)),
                  pl.BlockSpec((1, DN_DV), lambda b, t: (0, 0)),
                  pl.BlockSpec((nseg, DN_HEADS, DN_DK, DN_DV), lambda b, t: (b, 0, 0, 0))],
        out_specs=[pl.BlockSpec((TL, BRANCH_W), lambda b, t: (b * nt + t, 0)),
                   pl.BlockSpec((nseg, DN_HEADS, DN_DK, DN_DV), lambda b, t: (b, 0, 0, 0))],
        out_shape=[jax.ShapeDtypeStruct((M, BRANCH_W), F32),
                   jax.ShapeDtypeStruct(s0.shape, F32)],
        scratch_shapes=[pltpu.VMEM((nseg, DN_HEADS, DN_DK, DN_DV), F32),
                        pltpu.VMEM((SUBLANES, DN_QKV), F32),
                        pltpu.VMEM((TL, DN_QKV), F32)],
        compiler_params=_cparams(("arbitrary", "arbitrary")),
        name="dn",
    )(z, z, z, conv_w, hp, nrm, s0)


_LOG_GAMMA = [float(np.log1p(-np.exp2(-5.0 - h))) for h in range(RET_HEADS)]


def _ret_kernel(q_ref, k_ref, v_ref, g_ref, cos_ref, sin_ref, nrm_ref, s0_ref, o_ref, sfin_ref,
                S_scr, *, C, seg, lo, hi):
    t = pl.program_id(1)
    TL = q_ref.shape[0]
    H, DK = RET_HEADS, RET_DK
    masked = seg != C

    @pl.when(t == 0)
    def _():
        S_scr[...] = s0_ref[...]

    lowm, _, _ = _chunk_masks(C, seg)
    ri = lax.broadcasted_iota(jnp.int32, (C, 1), 0)
    ci = lax.broadcasted_iota(jnp.int32, (1, C), 1)
    if masked:
        valid = _valid_col(C, seg, lo, hi)
        cnt_c = jnp.clip(_imod(ri, seg) - lo + 1, 0, hi - lo).astype(F32)
        cnt_r = jnp.clip(_imod(ci, seg) - lo + 1, 0, hi - lo).astype(F32)
        cnt_t = float(hi - lo)
    else:
        valid = None
        cnt_c = (ri + 1).astype(F32)
        cnt_r = (ci + 1).astype(F32)
        cnt_t = float(C)

    for cidx in range(TL // C):
        r0 = cidx * C
        cosf = cos_ref[r0:r0 + C, :]
        sins = sin_ref[r0:r0 + C, :]
        for h in range(H):
            cs = slice(h * DK, (h + 1) * DK)
            q = q_ref[r0:r0 + C, cs]
            k = k_ref[r0:r0 + C, cs]
            v = v_ref[r0:r0 + C, cs]
            q = (q * cosf + pltpu.roll(q, DK // 2, axis=1) * sins) * DK ** -0.5
            k = k * cosf + pltpu.roll(k, DK // 2, axis=1) * sins
            if masked:
                v = v * valid
            lg = _LOG_GAMMA[h]
            Gc = cnt_c * lg
            decay = jnp.where(lowm, jnp.exp(jnp.where(lowm, (cnt_c - cnt_r) * lg, 0.0)), 0.0)
            qk = _mm(q, k, 'nt') * decay
            Gt = jnp.full((C, 1), cnt_t * lg, F32)
            o = _state_update(S_scr, h, v, None, q * jnp.exp(Gc), k * jnp.exp(Gt - Gc), qk, Gt, C, seg)
            mu = jnp.mean(o, axis=-1, keepdims=True)
            oc = o - mu
            var = jnp.mean(oc * oc, axis=-1, keepdims=True)
            on = oc * lax.rsqrt(var + EPS) * nrm_ref[h:h + 1, :]
            o_ref[r0:r0 + C, cs] = on * _silu(g_ref[r0:r0 + C, cs])

    @pl.when(t == pl.num_programs(1) - 1)
    def _():
        sfin_ref[...] = S_scr[...]


def _ret_call(z, cosf, sins, nrm, s0, n_outer, TL, seg, lo, hi):
    M = z.shape[0]
    nt = M // (n_outer * TL)
    nseg = CHUNK // seg
    W = RET_HEADS * RET_DK
    rowmap = lambda cb: (lambda b, t: (b * nt + t, cb))
    kern = functools.partial(_ret_kernel, C=CHUNK, seg=seg, lo=lo, hi=hi)
    return pl.pallas_call(
        kern,
        grid=(n_outer, nt),
        in_specs=[pl.BlockSpec((TL, W), rowmap(Z_RQ // W)),
                  pl.BlockSpec((TL, W), rowmap(Z_RK // W)),
                  pl.BlockSpec((TL, W), rowmap(Z_RV // W)),
                  pl.BlockSpec((TL, W), rowmap(Z_RG // W)),
                  pl.BlockSpec((TL, RET_DK), lambda b, t: (t, 0)),
                  pl.BlockSpec((TL, RET_DK), lambda b, t: (t, 0)),
                  pl.BlockSpec((RET_HEADS, RET_DV), lambda b, t: (0, 0)),
                  pl.BlockSpec((nseg, RET_HEADS, RET_DK, RET_DV), lambda b, t: (b, 0, 0, 0))],
        out_specs=[pl.BlockSpec((TL, W), lambda b, t: (b * nt + t, 0)),
                   pl.BlockSpec((nseg, RET_HEADS, RET_DK, RET_DV), lambda b, t: (b, 0, 0, 0))],
        out_shape=[jax.ShapeDtypeStruct((M, W), F32),
                   jax.ShapeDtypeStruct(s0.shape, F32)],
        scratch_shapes=[pltpu.VMEM((nseg, RET_HEADS, RET_DK, RET_DV), F32)],
        compiler_params=_cparams(("arbitrary", "arbitrary")),
        name="ret",
    )(z, z, z, z, cosf, sins, nrm, s0)


def _f2key(x):
    b = lax.bitcast_convert_type(x + 0.0, jnp.int32)
    return jnp.where(b >= 0, b, b ^ jnp.int32(0x7FFFFFFF))


def _t5_bucket(d):
    exact = N_BUCKETS // 2
    df = d.astype(F32)
    large = exact + (jnp.log(jnp.maximum(df, 1.0) / exact) / math.log(MAX_DISTANCE / exact)
                     * (N_BUCKETS - exact)).astype(jnp.int32)
    large = jnp.minimum(large, N_BUCKETS - 1)
    return jnp.where(d < exact, d, large)


def _bias_from_dist(d, rb_ref, h):
    bk = _t5_bucket(d)
    r = jnp.zeros(d.shape, F32)
    for jb in range(N_BUCKETS):
        r = jnp.where(bk == jb, rb_ref[jb, h], r)
    return r


def _threshold_search(count_ge, shape, total, kf):
    zero = jnp.zeros(shape, jnp.int32)
    c0 = count_ge(zero)
    ok0 = c0 >= kf
    T = jnp.where(ok0, 0, IMIN).astype(jnp.int32)
    cT = jnp.where(ok0, c0, total)

    def body(it, carry):
        T, cT = carry
        cand = T + lax.shift_left(jnp.int32(1), jnp.int32(30) - it)
        c = count_ge(cand)
        ok = c >= kf
        return jnp.where(ok, cand, T), jnp.where(ok, c, cT)

    return lax.fori_loop(0, 31, body, (T, cT))


def _fold_lanes(x):
    f = x[:, 0:LANES]
    for u in range(1, x.shape[1] // LANES):
        f = f + x[:, u * LANES:(u + 1) * LANES]
    return f


def _dsa_prompt_kernel(rb_ref, q_ref, qi_ref, smq_ref, k_ref, v_ref, smk_ref, o_ref, keys_scr, *, TQ, topk):
    i = pl.program_id(1)
    KC = TQ
    nk = i + 1
    kf = float(topk)
    qi = qi_ref[...]
    w4 = smq_ref[:, SM_IW:SM_IW + IDX_HEADS]
    rowi = lax.broadcasted_iota(jnp.int32, (TQ, KC), 0)
    coli = lax.broadcasted_iota(jnp.int32, (TQ, KC), 1)
    lane = lax.broadcasted_iota(jnp.int32, (KC, LANES), 1)

    def p1(j, c):
        r0 = pl.multiple_of(j * KC, KC)
        k_lo = jnp.where(lane < IDX_DIM, smk_ref[pl.ds(r0, KC), :], 0.0)
        k_hi = pltpu.roll(k_lo, IDX_DIM, axis=1)
        acc = jnp.zeros((TQ, KC), F32)
        for h in range(IDX_HEADS):
            slab = qi[:, (h // 2) * LANES:(h // 2 + 1) * LANES]
            s = _mm(slab, k_lo if h % 2 == 0 else k_hi, 'nt', 'x3')
            acc = acc + jnp.maximum(s, 0.0) * w4[:, h:h + 1]
        key = _f2key(acc * (IDX_DIM ** -0.5 * IDX_HEADS ** -0.5))
        keys_scr[j] = jnp.where(coli + r0 <= rowi + i * TQ, key, IMIN)
        return c

    lax.fori_loop(0, nk, p1, 0)

    def count_ge(cand):
        def body(j, part):
            return part + _fold_lanes(jnp.where(keys_scr[j] >= cand, 1.0, 0.0))
        part = lax.fori_loop(0, nk, body, jnp.zeros((TQ, LANES), F32))
        return jnp.sum(part, axis=1, keepdims=True)

    total = (nk * KC).astype(F32)
    T, cT = _threshold_search(count_ge, (TQ, 1), total, kf)

    ties = jnp.max(jnp.where((cT > kf) & (T > IMIN), 1.0, 0.0)) > 0.0

    @pl.when(ties)
    def _():
        need = kf - count_ge(T + 1)
        triu = jnp.where(lax.broadcasted_iota(jnp.int32, (KC, KC), 0)
                         <= lax.broadcasted_iota(jnp.int32, (KC, KC), 1), 1.0, 0.0).astype(BF16)

        def body(j, seen):
            kj = keys_scr[j]
            eq = kj == T
            pre = jnp.dot(jnp.where(eq, 1.0, 0.0).astype(BF16), triu, preferred_element_type=F32)
            keys_scr[j] = jnp.where(eq & (seen + pre > need), IMIN, kj)
            return seen + pre[:, KC - 1:KC]

        lax.fori_loop(0, nk, body, jnp.zeros((TQ, 1), F32))

    Tp = jnp.maximum(T, IMIN + 1)

    cidx = lax.broadcasted_iota(jnp.int32, (1, 2 * KC), 1)
    e = jnp.where(cidx < KC, cidx, cidx - 2 * KC)
    scale = ATT_DH ** -0.5
    jprev = jnp.maximum(i - 1, 0)

    def toeplitz(r):
        y = pltpu.roll(jnp.broadcast_to(r, (TQ, 2 * KC)), 0, 1, stride=1, stride_axis=0)
        return y[:, 0:KC]

    heads = range(ATT_HEADS)
    hcols = [slice(h * ATT_DH, (h + 1) * ATT_DH) for h in heads]
    qhs = [q_ref[:, cs].astype(BF16) for cs in hcols]

    def step(j, biases, extra_ok, carry):
        r0 = pl.multiple_of(j * KC, KC)
        sel = keys_scr[j] >= Tp
        if extra_ok is not None:
            sel = sel & extra_ok
        madd = jnp.where(sel, 0.0, NEG)
        out = []
        for h in heads:
            m, l, acc = carry[h]
            kh = k_ref[pl.ds(r0, KC), hcols[h]]
            vh = v_ref[pl.ds(r0, KC), hcols[h]]
            lg = lax.dot_general(qhs[h], kh, _DIMS['nt'], preferred_element_type=F32) * scale + biases[h] + madd
            m_new = jnp.maximum(m, jnp.max(lg, axis=1, keepdims=True))
            p = jnp.exp(lg - m_new)
            corr = jnp.exp(m - m_new)
            l = l * corr + jnp.sum(p, axis=1, keepdims=True)
            acc = acc * corr + jnp.dot(p.astype(BF16), vh, preferred_element_type=F32)
            out.append((m_new, l, acc))
        return tuple(out)

    carry = tuple((jnp.full((TQ, 1), NEG, F32), jnp.zeros((TQ, 1), F32), jnp.zeros((TQ, ATT_DH), F32))
                  for _ in heads)
    far_bias = [rb_ref[N_BUCKETS - 1, h] for h in heads]
    carry = lax.fori_loop(0, jprev, lambda j, c: step(j, far_bias, None, c), carry)
    b_prev = [toeplitz(_bias_from_dist(jnp.maximum(KC - e, 0), rb_ref, h)) for h in heads]
    carry = step(jprev, b_prev, (rowi * 0 + i) >= 1, carry)
    b_diag = [toeplitz(_bias_from_dist(jnp.maximum(-e, 0), rb_ref, h)) for h in heads]
    carry = step(i, b_diag, None, carry)
    for h in heads:
        m, l, acc = carry[h]
        o_ref[:, hcols[h]] = acc / l


def _dsa_prompt_call(rel_bias, z, kb, vb, B, L, TQ, topk):
    nq = L // TQ
    W = ATT_HEADS * ATT_DH
    kern = functools.partial(_dsa_prompt_kernel, TQ=TQ, topk=topk)
    return pl.pallas_call(
        kern,
        grid=(B, nq),
        in_specs=[pl.BlockSpec(memory_space=pltpu.SMEM),
                  pl.BlockSpec((TQ, W), lambda b, i: (b * nq + i, Z_AQ // W)),
                  pl.BlockSpec((TQ, IDX_HEADS * IDX_DIM), lambda b, i: (b * nq + i, Z_IQ // (IDX_HEADS * IDX_DIM))),
                  pl.BlockSpec((TQ, LANES), lambda b, i: (b * nq + i, Z_SM // LANES)),
                  pl.BlockSpec((L, W), lambda b, i: (b, 0)),
                  pl.BlockSpec((L, W), lambda b, i: (b, 0)),
                  pl.BlockSpec((L, LANES), lambda b, i: (b, Z_SM // LANES))],
        out_specs=pl.BlockSpec((TQ, W), lambda b, i: (b * nq + i, 0)),
        out_shape=jax.ShapeDtypeStruct((B * L, W), F32),
        scratch_shapes=[pltpu.VMEM((nq, TQ, TQ), jnp.int32)],
        compiler_params=_cparams(("arbitrary", "arbitrary")),
        name="dsa_prompt",
    )(rel_bias, z, z, z, kb, vb, z)


def _stack_heads(x, nh, w):
    return jnp.concatenate([x[:, h * w:(h + 1) * w] for h in range(nh)], axis=0)


def _page_map(layer, NS, G, g):
    return lambda b, p, pt: (layer, pt[(b * NS + p) * G + g], 0, 0)


def _dsa_s_index_kernel(pt_ref, qi_ref, sm_ref, *rest, NS, G, topk):
    kp_refs = rest[:G]
    keysp_ref, keysn_ref, tp_ref = rest[G:]
    p = pl.program_id(1)
    kf = float(topk)
    R = SROWS
    qs = _stack_heads(qi_ref[...], IDX_HEADS, IDX_DIM)
    wcol = _stack_heads(sm_ref[:, SM_IW:SM_IW + IDX_HEADS], IDX_HEADS, 1)

    def score_keys(kcat):
        s = _mm(qs, kcat, 'nt', 'x3')
        t = jnp.maximum(s, 0.0) * wcol
        acc = t[0:R]
        for h in range(1, IDX_HEADS):
            acc = acc + t[h * R:(h + 1) * R]
        return _f2key(acc * (IDX_DIM ** -0.5 * IDX_HEADS ** -0.5))

    keysp_ref[p] = score_keys(jnp.concatenate([r[...] for r in kp_refs], axis=0))

    @pl.when(p == NS - 1)
    def _():
        rowi = lax.broadcasted_iota(jnp.int32, (R, LANES), 0)
        coli = lax.broadcasted_iota(jnp.int32, (R, LANES), 1)
        knew = jnp.concatenate([sm_ref[:, SM_IK:SM_IK + IDX_DIM],
                                jnp.zeros((PAGE_SIZE - R, IDX_DIM), F32)], axis=0)
        ok = (coli >= S_LO) & (coli < S_HI) & (coli <= rowi)
        keysn_ref[...] = jnp.where(ok, score_keys(knew), IMIN)

        def count_ge(cand):
            a = jnp.sum(jnp.where(keysp_ref[...] >= cand[None], 1.0, 0.0), axis=0)
            b = jnp.where(keysn_ref[...] >= cand, 1.0, 0.0)
            return jnp.sum(_fold_lanes(a) + b, axis=1, keepdims=True)

        total = jnp.full((R, 1), float((NS * G + 1) * PAGE_SIZE), F32)
        T, cT = _threshold_search(count_ge, (R, 1), total, kf)
        rid = lax.broadcasted_iota(jnp.int32, (R, 1), 0)
        token_row = (rid >= S_LO) & (rid < S_HI)
        ties = jnp.max(jnp.where((cT > kf) & (T > IMIN) & token_row, 1.0, 0.0)) > 0.0

        @pl.when(ties)
        def _():
            need = kf - count_ge(T + 1)
            triu = jnp.where(lax.broadcasted_iota(jnp.int32, (LANES, LANES), 0)
                             <= lax.broadcasted_iota(jnp.int32, (LANES, LANES), 1), 1.0, 0.0).astype(BF16)

            def demote(blk, seen):
                eq = blk == T
                pre = jnp.dot(jnp.where(eq, 1.0, 0.0).astype(BF16), triu, preferred_element_type=F32)
                return jnp.where(eq & (seen + pre > need), IMIN, blk), seen + pre[:, LANES - 1:LANES]

            def body(j, seen):
                kj = keysp_ref[j]
                cols = []
                for g in range(G):
                    blk, seen = demote(kj[:, g * LANES:(g + 1) * LANES], seen)
                    cols.append(blk)
                keysp_ref[j] = jnp.concatenate(cols, axis=1)
                return seen

            seen = lax.fori_loop(0, NS, body, jnp.zeros((R, 1), F32))
            blk, _ = demote(keysn_ref[...], seen)
            keysn_ref[...] = blk

        tp_ref[...] = jnp.broadcast_to(jnp.maximum(T, IMIN + 1), (R, LANES))


def _dsa_s_index_call(page_table, z, cache_kidx, layer, DB, NP, G, topk):
    NS = NP // G
    GW = G * PAGE_SIZE
    kern = functools.partial(_dsa_s_index_kernel, NS=NS, G=G, topk=topk)
    QW = IDX_HEADS * IDX_DIM
    grid_spec = pltpu.PrefetchScalarGridSpec(
        num_scalar_prefetch=1,
        grid=(DB, NS),
        in_specs=[pl.BlockSpec((SROWS, QW), lambda b, p, pt: (b, Z_IQ // QW)),
                  pl.BlockSpec((SROWS, LANES), lambda b, p, pt: (b, Z_SM // LANES))]
                 + [pl.BlockSpec((None, None, PAGE_SIZE, IDX_DIM), _page_map(layer, NS, G, g)) for g in range(G)],
        out_specs=[pl.BlockSpec((None, NS, SROWS, GW), lambda b, p, pt: (b, 0, 0, 0)),
                   pl.BlockSpec((None, SROWS, LANES), lambda b, p, pt: (b, 0, 0)),
                   pl.BlockSpec((None, SROWS, LANES), lambda b, p, pt: (b, 0, 0))],
    )
    return pl.pallas_call(
        kern,
        grid_spec=grid_spec,
        out_shape=[jax.ShapeDtypeStruct((DB, NS, SROWS, GW), jnp.int32),
                   jax.ShapeDtypeStruct((DB, SROWS, LANES), jnp.int32),
                   jax.ShapeDtypeStruct((DB, SROWS, LANES), jnp.int32)],
        compiler_params=_cparams(("arbitrary", "arbitrary")),
        name="dsa_s_index",
    )(page_table.reshape(-1), z, z, *([cache_kidx] * G))


def _dsa_s_attend_kernel(pt_ref, rbr_ref, q_ref, keysp_ref, keysn_ref, tp_ref, *rest, NS, G, past):
    kp_refs, vp_refs = rest[:G], rest[G:2 * G]
    kn_ref, vn_ref, o_ref, m_scr, l_scr, acc_scr = rest[2 * G:]
    p = pl.program_id(1)
    R, H = SROWS, ATT_HEADS
    HR, PW = H * R, PAGE_SIZE * H
    rowi = lax.broadcasted_iota(jnp.int32, (HR, PW), 0)
    coli = lax.broadcasted_iota(jnp.int32, (HR, PW), 1)
    headmask = _imod(coli, H) == _idiv(rowi, R)
    qpos = past + _imod(rowi, R) - S_LO
    kin = _idiv(coli, H)
    expand = jnp.where(_idiv(lax.broadcasted_iota(jnp.int32, (PAGE_SIZE, PW), 1), H)
                       == lax.broadcasted_iota(jnp.int32, (PAGE_SIZE, PW), 0), 1.0, 0.0).astype(BF16)
    qa = _stack_heads(q_ref[...], H, ATT_DH).astype(BF16)
    Tp = tp_ref[...]
    scale = ATT_DH ** -0.5

    @pl.when(p == 0)
    def _():
        m_scr[...] = jnp.full(m_scr.shape, NEG, F32)
        l_scr[...] = jnp.zeros_like(l_scr)
        acc_scr[...] = jnp.zeros_like(acc_scr)

    def process(pages, near):
        lgs = []
        for ktile, kbase, xk_ref, _ in pages:
            s = lax.dot_general(qa, xk_ref[...].astype(BF16), _DIMS['nt'], preferred_element_type=F32)
            sel = jnp.dot(jnp.where(ktile >= Tp, 1.0, 0.0).astype(BF16), expand, preferred_element_type=F32)
            ok = (jnp.concatenate([sel] * H, axis=0) > 0.5) & headmask
            if near:
                bk = _t5_bucket(jnp.maximum(qpos - (kbase + kin), 0))
                bias = jnp.zeros((HR, PW), F32)
                for jb in range(N_BUCKETS):
                    bias = jnp.where(bk == jb, rbr_ref[:, jb:jb + 1], bias)
            else:
                bias = rbr_ref[:, N_BUCKETS - 1:N_BUCKETS]
            lgs.append(jnp.where(ok, s * scale + bias, NEG))
        mx = lgs[0]
        for lg in lgs[1:]:
            mx = jnp.maximum(mx, lg)
        m_old = m_scr[...]
        m_new = jnp.maximum(m_old, jnp.max(mx, axis=1, keepdims=True))
        corr = jnp.exp(m_old - m_new)
        tot, pv = None, None
        for lg, (_, _, _, xv_ref) in zip(lgs, pages):
            pr = jnp.exp(lg - m_new)
            d = jnp.dot(pr.astype(BF16), xv_ref[...].astype(BF16), preferred_element_type=F32)
            tot = pr if tot is None else tot + pr
            pv = d if pv is None else pv + d
        l_scr[...] = l_scr[...] * corr + jnp.sum(tot, axis=1, keepdims=True)
        acc_scr[...] = acc_scr[...] * corr + pv
        m_scr[...] = m_new

    def cache_pages():
        kt = keysp_ref[...]
        return [(kt[:, g * PAGE_SIZE:(g + 1) * PAGE_SIZE], (p * G + g) * PAGE_SIZE, kp_refs[g], vp_refs[g])
                for g in range(G)]

    @pl.when(p < NS - 1)
    def _():
        process(cache_pages(), False)

    @pl.when(p == NS - 1)
    def _():
        process(cache_pages(), True)
        process([(keysn_ref[...], past - S_LO, kn_ref, vn_ref)], True)
        inv = 1.0 / l_scr[...]
        for h in range(H):
            o_ref[:, h * ATT_DH:(h + 1) * ATT_DH] = acc_scr[h * R:(h + 1) * R, :] * inv[h * R:(h + 1) * R, :]


def _dsa_s_attend_call(page_table, rbrows, z, keysp, keysn, tp, cache_k, cache_v, knew, vnew, layer, DB, NP, G, past):
    NS = NP // G
    GW = G * PAGE_SIZE
    W = ATT_HEADS * ATT_DH
    PW = PAGE_SIZE * ATT_HEADS
    assert G * PAGE_SIZE >= MAX_DISTANCE
    kern = functools.partial(_dsa_s_attend_kernel, NS=NS, G=G, past=past)
    page_specs = [pl.BlockSpec((None, None, PW, ATT_DH), _page_map(layer, NS, G, g)) for g in range(G)]
    grid_spec = pltpu.PrefetchScalarGridSpec(
        num_scalar_prefetch=1,
        grid=(DB, NS),
        in_specs=[pl.BlockSpec((ATT_HEADS * SROWS, LANES), lambda b, p, pt: (0, 0)),
                  pl.BlockSpec((SROWS, W), lambda b, p, pt: (b, Z_AQ // W)),
                  pl.BlockSpec((None, None, SROWS, GW), lambda b, p, pt: (b, p, 0, 0)),
                  pl.BlockSpec((None, SROWS, LANES), lambda b, p, pt: (b, 0, 0)),
                  pl.BlockSpec((None, SROWS, LANES), lambda b, p, pt: (b, 0, 0))]
                 + page_specs + page_specs
                 + [pl.BlockSpec((None, PW, ATT_DH), lambda b, p, pt: (b, 0, 0)),
                    pl.BlockSpec((None, PW, ATT_DH), lambda b, p, pt: (b, 0, 0))],
        out_specs=pl.BlockSpec((SROWS, W), lambda b, p, pt: (b, 0)),
        scratch_shapes=[pltpu.VMEM((ATT_HEADS * SROWS, 1), F32),
                        pltpu.VMEM((ATT_HEADS * SROWS, 1), F32),
                        pltpu.VMEM((ATT_HEADS * SROWS, ATT_DH), F32)],
    )
    return pl.pallas_call(
        kern,
        grid_spec=grid_spec,
        out_shape=jax.ShapeDtypeStruct((DB * SROWS, W), F32),
        compiler_params=_cparams(("arbitrary", "arbitrary")),
        name="dsa_s_attend",
    )(page_table.reshape(-1), rbrows, z, keysp, keysn, tp, *([cache_k] * G), *([cache_v] * G), knew, vnew)


def _merge_kernel(oa_ref, ob_ref, oc_ref, g0_ref, g1_ref, g2_ref, x_ref, wb_ref, wo_ref, h_ref):
    acc = None
    for i, (o_ref, g_ref) in enumerate(((oa_ref, g0_ref), (ob_ref, g1_ref), (oc_ref, g2_ref))):
        br = jnp.dot(o_ref[...].astype(BF16), wb_ref[i], preferred_element_type=F32)
        term = _sigmoid(g_ref[...]) * br
        acc = term if acc is None else acc + term
    h_ref[...] = x_ref[...] + jnp.dot(acc.astype(BF16), wo_ref[...], preferred_element_type=F32)


def _merge_call(oa, ob, oc, z, x, wb, wo, tm):
    M = x.shape[0]
    W = BRANCH_W
    g0 = Z_GATE // D_MODEL
    row = lambda c: (lambda i: (i, c))
    return pl.pallas_call(
        _merge_kernel,
        grid=(M // tm,),
        in_specs=[pl.BlockSpec((tm, W), row(0)), pl.BlockSpec((tm, W), row(0)), pl.BlockSpec((tm, W), row(0)),
                  pl.BlockSpec((tm, D_MODEL), row(g0)), pl.BlockSpec((tm, D_MODEL), row(g0 + 1)),
                  pl.BlockSpec((tm, D_MODEL), row(g0 + 2)),
                  pl.BlockSpec((tm, D_MODEL), row(0)),
                  pl.BlockSpec((N_BRANCH, W, D_MODEL), lambda i: (0, 0, 0)),
                  pl.BlockSpec((D_MODEL, D_MODEL), lambda i: (0, 0))],
        out_specs=pl.BlockSpec((tm, D_MODEL), row(0)),
        out_shape=jax.ShapeDtypeStruct((M, D_MODEL), F32),
        compiler_params=_cparams(("parallel",)),
        name="merge",
    )(oa, ob, oc, z, z, z, x, wb, wo)


def _ffn_down_kernel(a_ref, h_ref, cw_ref, wd_ref, gf_ref, y_ref, prev_scr, act_scr, *, final_norm):
    t = pl.program_id(1)
    tm = a_ref.shape[0]
    FH = D_FF // 2

    @pl.when(t == 0)
    def _():
        prev_scr[...] = jnp.zeros_like(prev_scr)

    acc = h_ref[...]
    for c in range(2):
        gs = slice(c * FH, (c + 1) * FH)
        vs = slice(D_FF + c * FH, D_FF + (c + 1) * FH)
        yg, yg0 = _conv_tile(a_ref[:, gs], prev_scr[:, gs], cw_ref[:, gs], FFN_CONV)
        yv, yv0 = _conv_tile(a_ref[:, vs], prev_scr[:, vs], cw_ref[:, vs], FFN_CONV)
        act_scr[...] = (_silu(yg) * yv).astype(BF16)
        act_scr[0:2 * SUBLANES, :] = jnp.concatenate(
            [_silu(yg0) * yv0, _silu(yg[SUBLANES:2 * SUBLANES]) * yv[SUBLANES:2 * SUBLANES]], axis=0).astype(BF16)
        acc = acc + jnp.dot(act_scr[...], wd_ref[gs, :], preferred_element_type=F32)
    prev_scr[...] = a_ref[tm - SUBLANES:tm, :]
    if final_norm:
        acc = acc * lax.rsqrt(jnp.mean(acc * acc, axis=-1, keepdims=True) + EPS) * gf_ref[...]
    y_ref[...] = acc


def _ffn_down_call(a, h, conv_w, wd, gf, n_outer, tm, final_norm):
    M = h.shape[0]
    nt = M // (n_outer * tm)
    kern = functools.partial(_ffn_down_kernel, final_norm=final_norm)
    return pl.pallas_call(
        kern,
        grid=(n_outer, nt),
        in_specs=[pl.BlockSpec((tm, 2 * D_FF), lambda b, t: (b * nt + t, 0)),
                  pl.BlockSpec((tm, D_MODEL), lambda b, t: (b * nt + t, 0)),
                  pl.BlockSpec((FFN_CONV, 2 * D_FF), lambda b, t: (0, 0)),
                  pl.BlockSpec((D_FF, D_MODEL), lambda b, t: (0, 0)),
                  pl.BlockSpec((1, D_MODEL), lambda b, t: (0, 0))],
        out_specs=pl.BlockSpec((tm, D_MODEL), lambda b, t: (b * nt + t, 0)),
        out_shape=jax.ShapeDtypeStruct((M, D_MODEL), F32),
        scratch_shapes=[pltpu.VMEM((SUBLANES, 2 * D_FF), F32),
                        pltpu.VMEM((tm, D_FF // 2), BF16)],
        compiler_params=_cparams(("arbitrary", "arbitrary")),
        name="ffn_down",
    )(a, h, conv_w, wd, gf.reshape(1, D_MODEL))


def _reorder_w_in(w):
    offs = np.cumsum(np.array(IN_SIZES))[:-1].tolist()
    (dn_qkv, dn_z, dn_b, dn_a, r_q, r_k, r_v, r_g, a_q, a_k, a_v, i_q, i_k, i_w, gate) = jnp.split(w, offs, axis=-1)
    pad = jnp.zeros((w.shape[0], Z_COLS - Z_SM - (IDX_DIM + IDX_HEADS + 2 * DN_HEADS)), w.dtype)
    return jnp.concatenate([dn_qkv, dn_z, gate, r_q, r_k, r_v, r_g, a_q, a_k, a_v, i_q,
                            i_k, i_w, dn_b, dn_a, pad], axis=-1).astype(BF16)


def _rope_tables(pos):
    half = RET_DK // 2
    inv = 1.0 / (ROPE_BASE ** jnp.linspace(0.0, 1.0, half, dtype=F32))
    ang = pos.astype(F32)[:, None] * inv
    cos, sin = jnp.cos(ang), jnp.sin(ang)
    return jnp.concatenate([cos, cos], axis=-1), jnp.concatenate([-sin, sin], axis=-1)


def _dn_params(a_log, dt_bias):
    hp = jnp.zeros((SUBLANES, LANES), F32)
    hp = hp.at[0, SM_DNA:SM_DNA + DN_HEADS].set(a_log.astype(F32))
    hp = hp.at[1, SM_DNA:SM_DNA + DN_HEADS].set(dt_bias.astype(F32))
    return hp


def _layer_weights(l, norm_mix, w_in, dn_conv_w, dn_a_log, dn_dt_bias, dn_norm, ret_norm,
                   w_branch, w_o, norm_ffn, w_up, ffn_conv_w, w_down):
    return dict(norm_mix=norm_mix[l], w_in=_reorder_w_in(w_in[l]), dn_conv_w=dn_conv_w[l],
                hp=_dn_params(dn_a_log[l], dn_dt_bias[l]), dn_norm=dn_norm[l].reshape(1, DN_DV),
                ret_norm=ret_norm[l], wb=w_branch[l].astype(BF16), wo=w_o[l].astype(BF16),
                norm_ffn=norm_ffn[l], w_up=w_up[l].astype(BF16), ffn_conv_w=ffn_conv_w[l],
                wd=w_down[l].astype(BF16))


def _mix_and_ffn(x, z, oa, ob, oc, lw, norm_final, final, n_outer, tm_merge, tm_up, tn_up, tm_down, ffn_state=None):
    h = _merge_call(oa, ob, oc, z, x, lw['wb'], lw['wo'], tm_merge)
    a = _rms_matmul(h, lw['norm_ffn'], lw['w_up'], tm_up, tn_up)
    a_raw = a
    if ffn_state is not None:
        DB = ffn_state.shape[0]
        a = a.reshape(DB, SROWS, 2 * D_FF).at[:, S_LO - (FFN_CONV - 1):S_LO].set(ffn_state)
        a = a.reshape(DB * SROWS, 2 * D_FF)
    y = _ffn_down_call(a, h, lw['ffn_conv_w'], lw['wd'], norm_final, n_outer, tm_down, final)
    return y, a_raw


def kernel(x_prompt, x_sample, cache_k, cache_v, cache_kidx, state_dn_conv, state_dn, state_ret,
           state_ffn_conv, page_table, norm_mix, w_in, dn_conv_w, dn_a_log, dn_dt_bias, dn_norm,
           ret_norm, rel_bias, w_branch, w_o, norm_ffn, w_up, ffn_conv_w, w_down, norm_final):
    B, S, D = x_prompt.shape
    DB, DS, _ = x_sample.shape
    depth = w_in.shape[0]
    NP = page_table.shape[1]
    past = NP * PAGE_SIZE
    n_phys = cache_k.shape[1]
    W = ATT_HEADS * ATT_DH
    assert DS == S_HI - S_LO and S % CHUNK == 0 and (DB * SROWS) % CHUNK == 0

    TL = 256 if S % 256 == 0 else CHUNK
    TQ = 256 if S % 256 == 0 else CHUNK
    tm_p = 512 if (B * S) % 512 == 0 else CHUNK
    tm_mm = 1024 if (B * S) % 1024 == 0 else tm_p
    tm_d = 256 if S % 256 == 0 else CHUNK
    MS = DB * SROWS
    NG = MS // CHUNK
    seg_per = CHUNK // SROWS

    xp = x_prompt.reshape(B * S, D)
    xs = jnp.zeros((DB, SROWS, D), F32).at[:, S_LO:S_HI].set(x_sample).reshape(MS, D)
    cos_p, sin_p = _rope_tables(jnp.arange(S))
    pos_s = past + (jnp.arange(CHUNK) % SROWS) - S_LO
    cos_s, sin_s = _rope_tables(pos_s)
    ck = cache_k.reshape(depth, n_phys, PAGE_SIZE * ATT_HEADS, ATT_DH)
    cv = cache_v.reshape(depth, n_phys, PAGE_SIZE * ATT_HEADS, ATT_DH)
    rb = rel_bias.astype(F32)
    rbrows = jnp.pad(jnp.repeat(rb.T, SROWS, axis=0), ((0, 0), (0, LANES - N_BUCKETS)))
    G = next(g for g in (8, 4, 2, 1) if NP % g == 0)
    zeros_p = jnp.zeros((B, DN_HEADS, DN_DK, DN_DV), F32)
    topk_p = min(TOPK_MAX, S // 4)
    topk_s = min(TOPK_MAX, (past + DS) // 4)

    p_states, s_states = [], []
    for l in range(depth):
        lw = _layer_weights(l, norm_mix, w_in, dn_conv_w, dn_a_log, dn_dt_bias, dn_norm, ret_norm,
                            w_branch, w_o, norm_ffn, w_up, ffn_conv_w, w_down)
        final = l == depth - 1

        z = _rms_matmul(xp, lw['norm_mix'], lw['w_in'], tm_mm, 1024)
        oa, dn_s = _dn_call(z, lw['dn_conv_w'], lw['hp'], lw['dn_norm'], zeros_p, B, TL, CHUNK, 0, CHUNK)
        ob, ret_s = _ret_call(z, cos_p, sin_p, lw['ret_norm'], zeros_p, B, TL, CHUNK, 0, CHUNK)
        k_c = z[:, Z_AK:Z_AK + W]
        v_c = z[:, Z_AV:Z_AV + W]
        oc = _dsa_prompt_call(rb, z, k_c.astype(BF16), v_c.astype(BF16), B, S, TQ, topk_p)
        xp, a_up = _mix_and_ffn(xp, z, oa, ob, oc, lw, norm_final, final, B, tm_p, tm_mm, 1408, tm_d)
        z3 = z.reshape(B, S, Z_COLS)
        p_states.append((z3[:, S - (DN_CONV - 1):, Z_DNQKV:Z_DNQKV + DN_QKV], dn_s, ret_s,
                         k_c.reshape(B, S, ATT_HEADS, ATT_DH), v_c.reshape(B, S, ATT_HEADS, ATT_DH),
                         z3[:, :, Z_SM + SM_IK:Z_SM + SM_IK + IDX_DIM],
                         a_up.reshape(B, S, 2 * D_FF)[:, S - (FFN_CONV - 1):]))

        zs = _rms_matmul(xs, lw['norm_mix'], lw['w_in'], MS, 1024)
        zs3 = zs.reshape(DB, SROWS, Z_COLS)
        zs_conv = zs3.at[:, S_LO - (DN_CONV - 1):S_LO, Z_DNQKV:Z_DNQKV + DN_QKV].set(state_dn_conv[l])
        zs_conv = zs_conv.reshape(MS, Z_COLS)
        oa, dn_s = _dn_call(zs_conv, lw['dn_conv_w'], lw['hp'], lw['dn_norm'], state_dn[l], NG, CHUNK,
                            SROWS, S_LO, S_HI)
        ob, ret_s = _ret_call(zs, cos_s, sin_s, lw['ret_norm'], state_ret[l], NG, CHUNK, SROWS, S_LO, S_HI)
        keysp, keysn, tp = _dsa_s_index_call(page_table, zs, cache_kidx, l, DB, NP, G, topk_s)
        new_rows = lambda c0: jnp.pad(zs[:, c0:c0 + W].reshape(DB, SROWS * ATT_HEADS, ATT_DH),
                                      ((0, 0), (0, (PAGE_SIZE - SROWS) * ATT_HEADS), (0, 0)))
        oc = _dsa_s_attend_call(page_table, rbrows, zs, keysp, keysn, tp, ck, cv, new_rows(Z_AK), new_rows(Z_AV),
                                l, DB, NP, G, past)
        xs, a_up = _mix_and_ffn(xs, zs, oa, ob, oc, lw, norm_final, final, 1, MS, MS, 1408, MS,
                                ffn_state=state_ffn_conv[l])
        tok = zs3[:, S_LO:S_HI]
        s_states.append((tok[:, DS - (DN_CONV - 1):, Z_DNQKV:Z_DNQKV + DN_QKV], dn_s, ret_s,
                         tok[:, :, Z_AK:Z_AK + W].reshape(DB, DS, ATT_HEADS, ATT_DH),
                         tok[:, :, Z_AV:Z_AV + W].reshape(DB, DS, ATT_HEADS, ATT_DH),
                         tok[:, :, Z_SM + SM_IK:Z_SM + SM_IK + IDX_DIM],
                         a_up.reshape(DB, SROWS, 2 * D_FF)[:, S_HI - (FFN_CONV - 1):S_HI]))

    y_prompt = xp.reshape(B, S, D)
    y_sample = xs.reshape(DB, SROWS, D)[:, S_LO:S_HI]
    stk = lambda states, i: jnp.stack([st[i] for st in states])
    return (y_prompt, y_sample) + tuple(stk(p_states, i) for i in range(7)) + tuple(stk(s_states, i) for i in range(7))
```

```python
import functools
import math

import numpy as np
import jax
import jax.numpy as jnp
from jax import lax
from jax.experimental import pallas as pl
from jax.experimental.pallas import tpu as pltpu

D_MODEL = 1024
DEPTH = 2
PAST_LEN = 8192
PAGE_SIZE = 128
DN_HEADS = 4
DN_DK = 128
DN_DV = 128
DN_CONV = 4
DN_QKV = 2 * DN_HEADS * DN_DK + DN_HEADS * DN_DV
RET_HEADS = 4
RET_DK = 128
RET_DV = 128
ROPE_BASE = 10000.0
ATT_HEADS = 4
ATT_DH = 128
IDX_HEADS = 4
IDX_DIM = 64
TOPK_MAX = 256
N_BUCKETS = 32
MAX_DISTANCE = 128
N_BRANCH = 3
BRANCH_W = DN_HEADS * DN_DV
D_FF = 2816
FFN_CONV = 3
EPS = 1e-6
F32 = jnp.float32
BF16 = jnp.bfloat16
IN_SIZES = (DN_QKV, DN_HEADS * DN_DV, DN_HEADS, DN_HEADS,
            RET_HEADS * RET_DK, RET_HEADS * RET_DK, RET_HEADS * RET_DV, RET_HEADS * RET_DV,
            ATT_HEADS * ATT_DH, ATT_HEADS * ATT_DH, ATT_HEADS * ATT_DH,
            IDX_HEADS * IDX_DIM, IDX_DIM, IDX_HEADS, N_BRANCH * D_MODEL)

Z_DNQKV = 0
Z_DNZ = 1536
Z_GATE = 2048
Z_RQ, Z_RK, Z_RV, Z_RG = 5120, 5632, 6144, 6656
Z_AQ, Z_AK, Z_AV = 7168, 7680, 8192
Z_IQ = 8704
Z_SM = 8960
SM_IK, SM_IW, SM_DNB, SM_DNA = 0, 64, 68, 72
Z_COLS = 9216

LANES = 128
SUBLANES = 8
CHUNK = 128
SROWS = 8
S_LO, S_HI = 3, 7
NEG = -1e30
IMIN = -2 ** 31
VMEM_LIMIT = 56 * 1024 * 1024


def _cparams(sem):
    return pltpu.CompilerParams(dimension_semantics=sem, vmem_limit_bytes=VMEM_LIMIT)


def _sigmoid(x):
    return 1.0 / (1.0 + jnp.exp(-x))


def _silu(x):
    return x * _sigmoid(x)


def _softplus(x):
    return jnp.maximum(x, 0.0) + jnp.log(1.0 + jnp.exp(-jnp.abs(x)))


_DIMS = {'nn': (((1,), (0,)), ((), ())), 'nt': (((1,), (1,)), ((), ())), 'tn': (((0,), (0,)), ((), ()))}


def _split_bf16(a, n):
    parts = []
    r = a
    for i in range(n):
        p = r.astype(BF16)
        parts.append(p)
        if i + 1 < n:
            r = r - p.astype(F32)
    return parts


def _mm(a, b, dims='nn', mode='bf16'):
    dn = _DIMS[dims]
    dg = lambda x, y: lax.dot_general(x, y, dn, preferred_element_type=F32)
    if mode == 'bf16':
        return dg(a.astype(BF16), b.astype(BF16))
    if mode == 'x3':
        ah, al = _split_bf16(a, 2)
        bh, bl = _split_bf16(b, 2)
        return dg(ah, bh) + dg(ah, bl) + dg(al, bh)
    if mode == 'l01':
        ab = a.astype(BF16)
        b1, b2, b3 = _split_bf16(b, 3)
        return dg(ab, b1) + dg(ab, b2) + dg(ab, b3)
    raise ValueError(mode)


def _rms_mm_kernel(x_ref, g_ref, w_ref, o_ref, u_ref):
    @pl.when(pl.program_id(1) == 0)
    def _():
        x = x_ref[...]
        r = lax.rsqrt(jnp.mean(x * x, axis=-1, keepdims=True) + EPS)
        u_ref[...] = (x * r * g_ref[...]).astype(u_ref.dtype)

    o_ref[...] = jnp.dot(u_ref[...], w_ref[...], preferred_element_type=F32)


def _rms_matmul(x, g, w, layer, tm, tn):
    M, K = x.shape
    N = w.shape[2]
    return pl.pallas_call(
        _rms_mm_kernel,
        grid=(M // tm, N // tn),
        in_specs=[pl.BlockSpec((tm, K), lambda i, j: (i, 0)),
                  pl.BlockSpec((None, 1, K), lambda i, j: (layer, 0, 0)),
                  pl.BlockSpec((None, K, tn), lambda i, j: (layer, 0, j))],
        out_specs=pl.BlockSpec((tm, tn), lambda i, j: (i, j)),
        out_shape=jax.ShapeDtypeStruct((M, N), F32),
        scratch_shapes=[pltpu.VMEM((tm, K), BF16)],
        compiler_params=_cparams(("parallel", "arbitrary")),
        name="rms_matmul",
    )(x, g.reshape(g.shape[0], 1, K), w)


_IN_OFFS = [0] + np.cumsum(np.array(IN_SIZES)).tolist()
_SRC_DNB, _SRC_RQ, _SRC_IK, _SRC_IW, _SRC_GATE, IN_COLS = (_IN_OFFS[2], _IN_OFFS[4], _IN_OFFS[12], _IN_OFFS[13],
                                                           _IN_OFFS[14], _IN_OFFS[15])
IN_COLS_PAD = -(-IN_COLS // LANES) * LANES


def _prep_w_in_kernel(w_ref, o_ref):
    def shifted(src, width):
        a = src // LANES * LANES
        win = -(-(src - a + width) // LANES) * LANES
        return pltpu.roll(w_ref[:, a:a + win], win - (src - a), axis=1)[:, 0:width]

    n_head = Z_GATE
    o_ref[:, 0:n_head] = w_ref[:, 0:n_head].astype(BF16)
    o_ref[:, Z_GATE:Z_GATE + N_BRANCH * D_MODEL] = shifted(_SRC_GATE, N_BRANCH * D_MODEL).astype(BF16)
    o_ref[:, Z_RQ:Z_SM] = shifted(_SRC_RQ, Z_SM - Z_RQ).astype(BF16)
    lane = lax.broadcasted_iota(jnp.int32, (w_ref.shape[0], LANES), 1)
    n_idx = IDX_DIM + IDX_HEADS
    a_ik = _SRC_IK // LANES * LANES
    idx_part = pltpu.roll(w_ref[:, a_ik:a_ik + LANES], LANES - (_SRC_IK - a_ik), axis=1)
    dn_part = pltpu.roll(w_ref[:, _SRC_DNB:_SRC_DNB + LANES], SM_DNB, axis=1)
    small = jnp.where(lane < n_idx, idx_part, jnp.where(lane < n_idx + 2 * DN_HEADS, dn_part, 0.0))
    o_ref[:, Z_SM:Z_SM + LANES] = small.astype(BF16)
    o_ref[:, Z_SM + LANES:Z_COLS] = jnp.zeros((w_ref.shape[0], Z_COLS - Z_SM - LANES), BF16)


def _prep_w_in(w_in):
    depth, K, _ = w_in.shape
    assert _SRC_DNB % LANES == 0 and _IN_OFFS[1] == Z_DNZ and _SRC_DNB == Z_GATE
    assert _SRC_IW - _SRC_IK == IDX_DIM and _SRC_GATE - _SRC_IW == IDX_HEADS
    assert (_SRC_IK % LANES) + IDX_DIM + IDX_HEADS <= LANES and SM_DNB == IDX_DIM + IDX_HEADS
    assert Z_SM - Z_RQ == _SRC_IK - _SRC_RQ and SM_DNA == SM_DNB + DN_HEADS
    tr = 256
    wp = jnp.pad(w_in, ((0, 0), (0, 0), (0, IN_COLS_PAD - IN_COLS)))
    return pl.pallas_call(
        _prep_w_in_kernel,
        grid=(depth, K // tr),
        in_specs=[pl.BlockSpec((None, tr, IN_COLS_PAD), lambda l, i: (l, i, 0))],
        out_specs=pl.BlockSpec((None, tr, Z_COLS), lambda l, i: (l, i, 0)),
        out_shape=jax.ShapeDtypeStruct((depth, K, Z_COLS), BF16),
        compiler_params=_cparams(("parallel", "parallel")),
        name="prep_w_in",
    )(wp)


def _conv_tile(x, prev8, w, width):
    y = x * w[width - 1:width, :]
    for s in range(1, width):
        y = y + pltpu.roll(x, s, axis=0) * w[width - 1 - s:width - s, :]
    x0 = x[0:SUBLANES, :]
    rid = lax.broadcasted_iota(jnp.int32, x0.shape, 0)
    y0 = x0 * w[width - 1:width, :]
    for s in range(1, width):
        xs = jnp.where(rid < s, pltpu.roll(prev8, s, axis=0), pltpu.roll(x0, s, axis=0))
        y0 = y0 + xs * w[width - 1 - s:width - s, :]
    return y, y0


def _idiv(x, n):
    assert n & (n - 1) == 0
    return lax.shift_right_arithmetic(x, jnp.int32(n.bit_length() - 1))


def _imod(x, n):
    assert n & (n - 1) == 0
    return x & jnp.int32(n - 1)


def _chunk_masks(C, seg):
    ri = lax.broadcasted_iota(jnp.int32, (C, C), 0)
    ci = lax.broadcasted_iota(jnp.int32, (C, C), 1)
    if seg == C:
        return ri >= ci, ri > ci, None
    same = _idiv(ri, seg) == _idiv(ci, seg)
    return (ri >= ci) & same, (ri > ci) & same, same


def _valid_col(C, seg, lo, hi):
    r = _imod(lax.broadcasted_iota(jnp.int32, (C, 1), 0), seg)
    return jnp.where((r >= lo) & (r < hi), 1.0, 0.0)


def _tri_inv(ms, span):
    C = ms[0].shape[0]
    eye = jnp.where(lax.broadcasted_iota(jnp.int32, (C, C), 0) == lax.broadcasted_iota(jnp.int32, (C, C), 1),
                    1.0, 0.0)
    invs = [eye - m for m in ms]
    ps = list(ms)
    n = 2
    while n < span:
        ps = [_mm(p, p, 'nn', 'x3') for p in ps]
        invs = [inv + _mm(inv, p, 'nn', 'x3') for inv, p in zip(invs, ps)]
        n *= 2
    return invs


def _state_update(S_scr, h, u, kcum, qd, kd, qk, gtot, C, seg):
    nseg = C // seg
    ws, o1s = [], []
    for sg in range(nseg):
        rs = slice(sg * seg, (sg + 1) * seg)
        S = S_scr[sg, h]
        if kcum is None:
            ws.append(u[rs])
        else:
            ws.append(u[rs] - _mm(kcum[rs], S))
        o1s.append(_mm(qd[rs], S))
    w = ws[0] if nseg == 1 else jnp.concatenate(ws, axis=0)
    o1 = o1s[0] if nseg == 1 else jnp.concatenate(o1s, axis=0)
    o = o1 + _mm(qk, w)
    rowid = lax.broadcasted_iota(jnp.int32, (C, 1), 0)
    for sg in range(nseg):
        kdm = kd if nseg == 1 else jnp.where(_idiv(rowid, seg) == sg, kd, 0.0)
        gt = jnp.exp(gtot[sg * seg:sg * seg + 1, :])
        S_scr[sg, h] = S_scr[sg, h] * gt + _mm(kdm, w, 'tn')
    return o


def _dn_kernel(qkv_ref, dz_ref, sm_ref, cw_ref, hp_ref, nrm_ref, s0_ref, o_ref, sfin_ref,
               S_scr, prev_scr, c_scr, *, C, seg, lo, hi):
    t = pl.program_id(1)
    TL = qkv_ref.shape[0]
    H, DK = DN_HEADS, DN_DK
    masked = seg != C

    @pl.when(t == 0)
    def _():
        S_scr[...] = s0_ref[...]
        prev_scr[...] = jnp.zeros_like(prev_scr)

    x = qkv_ref[...]
    y, y0 = _conv_tile(x, prev_scr[...], cw_ref[...], DN_CONV)
    c_scr[...] = _silu(y)
    c_scr[0:SUBLANES, :] = _silu(y0)
    prev_scr[...] = x[TL - SUBLANES:TL, :]

    lowm, strictm, same = _chunk_masks(C, seg)
    ltri = jnp.where(lowm, 1.0, 0.0)
    valid = _valid_col(C, seg, lo, hi) if masked else None
    span = (hi - lo) if masked else C
    a_coef = -jnp.exp(hp_ref[0:1, :])
    dtb = hp_ref[1:2, :]

    units = []
    for cidx in range(TL // C):
        r0 = cidx * C
        cc = c_scr[r0:r0 + C, :]
        sm = sm_ref[r0:r0 + C, :]
        g128 = a_coef * _softplus(sm + dtb)
        b128 = _sigmoid(sm)
        if masked:
            g128 = g128 * valid
            b128 = b128 * valid
        Gc128 = _mm(ltri, g128, 'nn', 'l01')
        if masked:
            Gt128 = _mm(jnp.where(same, 1.0, 0.0), g128, 'nn', 'l01')
        else:
            Gt128 = jnp.broadcast_to(Gc128[C - 1:C, :], Gc128.shape)
        GT = Gc128.T
        for h in range(H):
            q = cc[:, h * DK:(h + 1) * DK]
            k = cc[:, (H + h) * DK:(H + h + 1) * DK]
            v = cc[:, (2 * H + h) * DK:(2 * H + h + 1) * DK]
            q = q * lax.rsqrt(jnp.sum(q * q, axis=-1, keepdims=True) + EPS) * DK ** -0.5
            k = k * lax.rsqrt(jnp.sum(k * k, axis=-1, keepdims=True) + EPS)
            if masked:
                k = k * valid
            Gc = Gc128[:, SM_DNA + h:SM_DNA + h + 1]
            Gr = GT[SM_DNA + h:SM_DNA + h + 1, :]
            Gt = Gt128[:, SM_DNA + h:SM_DNA + h + 1]
            bc = b128[:, SM_DNB + h:SM_DNB + h + 1]
            decay = jnp.where(lowm, jnp.exp(jnp.where(lowm, Gc - Gr, 0.0)), 0.0)
            eG = jnp.exp(Gc)
            units.append(dict(r0=r0, h=h, Gt=Gt, m=jnp.where(strictm, _mm(k, k, 'nt') * decay * bc, 0.0),
                              qk=_mm(q, k, 'nt') * decay, rhs_u=v * bc, rhs_k=k * (bc * eG),
                              qd=q * eG, kd=k * jnp.exp(Gt - Gc)))
    ainvs = _tri_inv([un['m'] for un in units], span)
    for un, ainv in zip(units, ainvs):
        un['u'] = _mm(ainv, un['rhs_u'], 'nn', 'x3')
        un['kcum'] = _mm(ainv, un['rhs_k'], 'nn', 'x3')

    for un in units:
        r0, h = un['r0'], un['h']
        o = _state_update(S_scr, h, un['u'], un['kcum'], un['qd'], un['kd'], un['qk'], un['Gt'], C, seg)
        on = o * lax.rsqrt(jnp.mean(o * o, axis=-1, keepdims=True) + EPS) * nrm_ref[...]
        zg = dz_ref[r0:r0 + C, h * DN_DV:(h + 1) * DN_DV]
        o_ref[r0:r0 + C, h * DN_DV:(h + 1) * DN_DV] = on * _silu(zg)

    @pl.when(t == pl.num_programs(1) - 1)
    def _():
        sfin_ref[...] = S_scr[...]


def _dn_call(z, conv_w, hp, nrm, s0, n_outer, TL, seg, lo, hi):
    M = z.shape[0]
    nt = M // (n_outer * TL)
    nseg = CHUNK // seg
    rowmap = lambda cb: (lambda b, t: (b * nt + t, cb))
    kern = functools.partial(_dn_kernel, C=CHUNK, seg=seg, lo=lo, hi=hi)
    return pl.pallas_call(
        kern,
        grid=(n_outer, nt),
        in_specs=[pl.BlockSpec((TL, DN_QKV), rowmap(Z_DNQKV // DN_QKV)),
                  pl.BlockSpec((TL, BRANCH_W), rowmap(Z_DNZ // BRANCH_W)),
                  pl.BlockSpec((TL, LANES), rowmap(Z_SM // LANES)),
                  pl.BlockSpec((DN_CONV, DN_QKV), lambda b, t: (0, 0)),
                  pl.BlockSpec((SUBLANES, LANES), lambda b, t: (0, 0)),
                  pl.BlockSpec((1, DN_DV), lambda b, t: (0, 0)),
                  pl.BlockSpec((nseg, DN_HEADS, DN_DK, DN_DV), lambda b, t: (b, 0, 0, 0))],
        out_specs=[pl.BlockSpec((TL, BRANCH_W), lambda b, t: (b * nt + t, 0)),
                   pl.BlockSpec((nseg, DN_HEADS, DN_DK, DN_DV), lambda b, t: (b, 0, 0, 0))],
        out_shape=[jax.ShapeDtypeStruct((M, BRANCH_W), F32),
                   jax.ShapeDtypeStruct(s0.shape, F32)],
        scratch_shapes=[pltpu.VMEM((nseg, DN_HEADS, DN_DK, DN_DV), F32),
                        pltpu.VMEM((SUBLANES, DN_QKV), F32),
                        pltpu.VMEM((TL, DN_QKV), F32)],
        compiler_params=_cparams(("arbitrary", "arbitrary")),
        name="dn",
    )(z, z, z, conv_w, hp, nrm, s0)


_LOG_GAMMA = [float(np.log1p(-np.exp2(-5.0 - h))) for h in range(RET_HEADS)]


def _ret_kernel(q_ref, k_ref, v_ref, g_ref, cos_ref, sin_ref, nrm_ref, s0_ref, o_ref, sfin_ref,
                S_scr, *, C, seg, lo, hi):
    t = pl.program_id(1)
    TL = q_ref.shape[0]
    H, DK = RET_HEADS, RET_DK
    masked = seg != C

    @pl.when(t == 0)
    def _():
        S_scr[...] = s0_ref[...]

    lowm, _, _ = _chunk_masks(C, seg)
    ri = lax.broadcasted_iota(jnp.int32, (C, 1), 0)
    ci = lax.broadcasted_iota(jnp.int32, (1, C), 1)
    if masked:
        valid = _valid_col(C, seg, lo, hi)
        cnt_c = jnp.clip(_imod(ri, seg) - lo + 1, 0, hi - lo).astype(F32)
        cnt_r = jnp.clip(_imod(ci, seg) - lo + 1, 0, hi - lo).astype(F32)
        cnt_t = float(hi - lo)
    else:
        valid = None
        cnt_c = (ri + 1).astype(F32)
        cnt_r = (ci + 1).astype(F32)
        cnt_t = float(C)

    for cidx in range(TL // C):
        r0 = cidx * C
        cosf = cos_ref[r0:r0 + C, :]
        sins = sin_ref[r0:r0 + C, :]
        for h in range(H):
            cs = slice(h * DK, (h + 1) * DK)
            q = q_ref[r0:r0 + C, cs]
            k = k_ref[r0:r0 + C, cs]
            v = v_ref[r0:r0 + C, cs]
            q = (q * cosf + pltpu.roll(q, DK // 2, axis=1) * sins) * DK ** -0.5
            k = k * cosf + pltpu.roll(k, DK // 2, axis=1) * sins
            if masked:
                v = v * valid
            lg = _LOG_GAMMA[h]
            Gc = cnt_c * lg
            decay = jnp.where(lowm, jnp.exp(jnp.where(lowm, (cnt_c - cnt_r) * lg, 0.0)), 0.0)
            qk = _mm(q, k, 'nt') * decay
            Gt = jnp.full((C, 1), cnt_t * lg, F32)
            o = _state_update(S_scr, h, v, None, q * jnp.exp(Gc), k * jnp.exp(Gt - Gc), qk, Gt, C, seg)
            mu = jnp.mean(o, axis=-1, keepdims=True)
            oc = o - mu
            var = jnp.mean(oc * oc, axis=-1, keepdims=True)
            on = oc * lax.rsqrt(var + EPS) * nrm_ref[h:h + 1, :]
            o_ref[r0:r0 + C, cs] = on * _silu(g_ref[r0:r0 + C, cs])

    @pl.when(t == pl.num_programs(1) - 1)
    def _():
        sfin_ref[...] = S_scr[...]


def _ret_call(z, cosf, sins, nrm, s0, n_outer, TL, seg, lo, hi):
    M = z.shape[0]
    nt = M // (n_outer * TL)
    nseg = CHUNK // seg
    W = RET_HEADS * RET_DK
    rowmap = lambda cb: (lambda b, t: (b * nt + t, cb))
    kern = functools.partial(_ret_kernel, C=CHUNK, seg=seg, lo=lo, hi=hi)
    return pl.pallas_call(
        kern,
        grid=(n_outer, nt),
        in_specs=[pl.BlockSpec((TL, W), rowmap(Z_RQ // W)),
                  pl.BlockSpec((TL, W), rowmap(Z_RK // W)),
                  pl.BlockSpec((TL, W), rowmap(Z_RV // W)),
                  pl.BlockSpec((TL, W), rowmap(Z_RG // W)),
                  pl.BlockSpec((TL, RET_DK), lambda b, t: (t, 0)),
                  pl.BlockSpec((TL, RET_DK), lambda b, t: (t, 0)),
                  pl.BlockSpec((RET_HEADS, RET_DV), lambda b, t: (0, 0)),
                  pl.BlockSpec((nseg, RET_HEADS, RET_DK, RET_DV), lambda b, t: (b, 0, 0, 0))],
        out_specs=[pl.BlockSpec((TL, W), lambda b, t: (b * nt + t, 0)),
                   pl.BlockSpec((nseg, RET_HEADS, RET_DK, RET_DV), lambda b, t: (b, 0, 0, 0))],
        out_shape=[jax.ShapeDtypeStruct((M, W), F32),
                   jax.ShapeDtypeStruct(s0.shape, F32)],
        scratch_shapes=[pltpu.VMEM((nseg, RET_HEADS, RET_DK, RET_DV), F32)],
        compiler_params=_cparams(("arbitrary", "arbitrary")),
        name="ret",
    )(z, z, z, z, cosf, sins, nrm, s0)


def _f2key(x):
    b = lax.bitcast_convert_type(x + 0.0, jnp.int32)
    return jnp.where(b >= 0, b, b ^ jnp.int32(0x7FFFFFFF))


def _t5_bucket(d):
    exact = N_BUCKETS // 2
    df = d.astype(F32)
    large = exact + (jnp.log(jnp.maximum(df, 1.0) / exact) / math.log(MAX_DISTANCE / exact)
                     * (N_BUCKETS - exact)).astype(jnp.int32)
    large = jnp.minimum(large, N_BUCKETS - 1)
    return jnp.where(d < exact, d, large)


def _bias_from_dist(d, rb_ref, h):
    bk = _t5_bucket(d)
    r = jnp.zeros(d.shape, F32)
    for jb in range(N_BUCKETS):
        r = jnp.where(bk == jb, rb_ref[jb, h], r)
    return r


def _threshold_search(count_ge, shape, total, kf):
    zero = jnp.zeros(shape, jnp.int32)
    c0 = count_ge(zero)
    ok0 = c0 >= kf
    T = jnp.where(ok0, 0, IMIN).astype(jnp.int32)
    cT = jnp.where(ok0, c0, total)

    def body(it, carry):
        T, cT = carry
        cand = T + lax.shift_left(jnp.int32(1), jnp.int32(30) - it)
        c = count_ge(cand)
        ok = c >= kf
        return jnp.where(ok, cand, T), jnp.where(ok, c, cT)

    return lax.fori_loop(0, 31, body, (T, cT))


def _fold_lanes(x):
    f = x[:, 0:LANES]
    for u in range(1, x.shape[1] // LANES):
        f = f + x[:, u * LANES:(u + 1) * LANES]
    return f


def _fold_rows(x):
    return jnp.sum(x.reshape(x.shape[0] // SUBLANES, SUBLANES, x.shape[1]), axis=0)


def _dsa_prompt_kernel(rb_ref, q_ref, qi_ref, smq_ref, k_ref, vt_ref, smk_ref, o_ref, keys_scr, *, TQ, topk):
    i = pl.program_id(1)
    KC = TQ
    nk = i + 1
    kf = float(topk)
    qi = qi_ref[...]
    wT = smq_ref[...].T
    kpos0 = lax.broadcasted_iota(jnp.int32, (KC, TQ), 0)
    qidx = lax.broadcasted_iota(jnp.int32, (KC, TQ), 1)
    lane = lax.broadcasted_iota(jnp.int32, (KC, LANES), 1)

    def p1(j, c):
        r0 = pl.multiple_of(j * KC, KC)
        k_lo = jnp.where(lane < IDX_DIM, smk_ref[pl.ds(r0, KC), :], 0.0)
        k_hi = pltpu.roll(k_lo, IDX_DIM, axis=1)
        acc = jnp.zeros((KC, TQ), F32)
        for h in range(IDX_HEADS):
            slab = qi[:, (h // 2) * LANES:(h // 2 + 1) * LANES]
            s = _mm(k_lo if h % 2 == 0 else k_hi, slab, 'nt', 'x3')
            acc = acc + jnp.maximum(s, 0.0) * wT[SM_IW + h:SM_IW + h + 1, :]
        key = _f2key(acc * (IDX_DIM ** -0.5 * IDX_HEADS ** -0.5))
        keys_scr[j] = jnp.where(kpos0 + r0 <= qidx + i * TQ, key, IMIN)
        return c

    lax.fori_loop(0, nk, p1, 0)

    def count_ge(cand):
        def body(j, part):
            return part + _fold_rows(jnp.where(keys_scr[j] >= cand, 1.0, 0.0))
        part = lax.fori_loop(0, nk, body, jnp.zeros((SUBLANES, TQ), F32))
        return jnp.sum(part, axis=0, keepdims=True)

    total = (nk * KC).astype(F32)
    T, cT = _threshold_search(count_ge, (1, TQ), total, kf)

    ties = jnp.max(jnp.where((cT > kf) & (T > IMIN), 1.0, 0.0)) > 0.0

    @pl.when(ties)
    def _():
        need = kf - count_ge(T + 1)
        tril = jnp.where(lax.broadcasted_iota(jnp.int32, (KC, KC), 0)
                         >= lax.broadcasted_iota(jnp.int32, (KC, KC), 1), 1.0, 0.0).astype(BF16)

        def body(j, seen):
            kj = keys_scr[j]
            eq = kj == T
            pre = jnp.dot(tril, jnp.where(eq, 1.0, 0.0).astype(BF16), preferred_element_type=F32)
            keys_scr[j] = jnp.where(eq & (seen + pre > need), IMIN, kj)
            return seen + pre[KC - 1:KC, :]

        lax.fori_loop(0, nk, body, jnp.zeros((1, TQ), F32))

    Tp = jnp.maximum(T, IMIN + 1)

    cidx = lax.broadcasted_iota(jnp.int32, (1, 2 * KC), 1)
    e = jnp.where(cidx < KC, cidx, cidx - 2 * KC)
    scale = ATT_DH ** -0.5
    jprev = jnp.maximum(i - 1, 0)

    def toeplitz(r):
        y = pltpu.roll(jnp.broadcast_to(r, (KC, 2 * KC)), 0, 1, stride=1, stride_axis=0)
        return y[:, 0:TQ]

    heads = range(ATT_HEADS)
    hcols = [slice(h * ATT_DH, (h + 1) * ATT_DH) for h in heads]
    qhs = [q_ref[:, cs].astype(BF16) for cs in hcols]

    def step(j, biases, extra_ok, carry):
        r0 = pl.multiple_of(j * KC, KC)
        sel = keys_scr[j] >= Tp
        if extra_ok is not None:
            sel = sel & extra_ok
        madd = jnp.where(sel, 0.0, NEG)
        out = []
        for h in heads:
            m, l, acc = carry[h]
            kh = k_ref[pl.ds(r0, KC), hcols[h]]
            lg = lax.dot_general(kh, qhs[h], _DIMS['nt'], preferred_element_type=F32) * scale + biases[h] + madd
            m_new = jnp.maximum(m, jnp.max(lg, axis=0, keepdims=True))
            p = jnp.exp(lg - m_new)
            corr = jnp.exp(m - m_new)
            l = l * corr + jnp.sum(p, axis=0, keepdims=True)
            acc = acc * corr + jnp.dot(vt_ref[j, hcols[h], :], p.astype(BF16), preferred_element_type=F32)
            out.append((m_new, l, acc))
        return tuple(out)

    carry = tuple((jnp.full((1, TQ), NEG, F32), jnp.zeros((1, TQ), F32), jnp.zeros((ATT_DH, TQ), F32))
                  for _ in heads)
    far_bias = [rb_ref[N_BUCKETS - 1, h] for h in heads]
    carry = lax.fori_loop(0, jprev, lambda j, c: step(j, far_bias, None, c), carry)
    b_prev = [toeplitz(_bias_from_dist(jnp.maximum(KC + e, 0), rb_ref, h)) for h in heads]
    carry = step(jprev, b_prev, (qidx * 0 + i) >= 1, carry)
    b_diag = [toeplitz(_bias_from_dist(jnp.maximum(e, 0), rb_ref, h)) for h in heads]
    carry = step(i, b_diag, None, carry)
    for h in heads:
        m, l, acc = carry[h]
        o_ref[:, hcols[h]] = (acc / l).T


def _dsa_prompt_call(rel_bias, z, kb, vt, B, L, TQ, topk):
    nq = L // TQ
    W = ATT_HEADS * ATT_DH
    kern = functools.partial(_dsa_prompt_kernel, TQ=TQ, topk=topk)
    return pl.pallas_call(
        kern,
        grid=(B, nq),
        in_specs=[pl.BlockSpec(memory_space=pltpu.SMEM),
                  pl.BlockSpec((TQ, W), lambda b, i: (b * nq + i, Z_AQ // W)),
                  pl.BlockSpec((TQ, IDX_HEADS * IDX_DIM), lambda b, i: (b * nq + i, Z_IQ // (IDX_HEADS * IDX_DIM))),
                  pl.BlockSpec((TQ, LANES), lambda b, i: (b * nq + i, Z_SM // LANES)),
                  pl.BlockSpec((L, W), lambda b, i: (b, 0)),
                  pl.BlockSpec((None, nq, W, TQ), lambda b, i: (b, 0, 0, 0)),
                  pl.BlockSpec((L, LANES), lambda b, i: (b, Z_SM // LANES))],
        out_specs=pl.BlockSpec((TQ, W), lambda b, i: (b * nq + i, 0)),
        out_shape=jax.ShapeDtypeStruct((B * L, W), F32),
        scratch_shapes=[pltpu.VMEM((nq, TQ, TQ), jnp.int32)],
        compiler_params=_cparams(("arbitrary", "arbitrary")),
        name="dsa_prompt",
    )(rel_bias, z, z, z, kb, vt, z)


def _stack_heads(x, nh, w):
    return jnp.concatenate([x[:, h * w:(h + 1) * w] for h in range(nh)], axis=0)


def _page_map(layer, NS, G, g):
    return lambda b, p, pt: (layer, pt[(b * NS + p) * G + g], 0, 0)


def _dsa_s_index_kernel(pt_ref, qi_ref, sm_ref, *rest, NS, G, topk):
    kp_refs = rest[:G]
    keysp_ref, keysn_ref, tp_ref = rest[G:]
    p = pl.program_id(1)
    kf = float(topk)
    R = SROWS
    qs = _stack_heads(qi_ref[...], IDX_HEADS, IDX_DIM)
    wcol = _stack_heads(sm_ref[:, SM_IW:SM_IW + IDX_HEADS], IDX_HEADS, 1)

    def score_keys(kcat):
        s = _mm(qs, kcat, 'nt', 'x3')
        t = jnp.maximum(s, 0.0) * wcol
        acc = t[0:R]
        for h in range(1, IDX_HEADS):
            acc = acc + t[h * R:(h + 1) * R]
        return _f2key(acc * (IDX_DIM ** -0.5 * IDX_HEADS ** -0.5))

    keysp_ref[p] = score_keys(jnp.concatenate([r[...] for r in kp_refs], axis=0))

    @pl.when(p == NS - 1)
    def _():
        rowi = lax.broadcasted_iota(jnp.int32, (R, LANES), 0)
        coli = lax.broadcasted_iota(jnp.int32, (R, LANES), 1)
        knew = jnp.concatenate([sm_ref[:, SM_IK:SM_IK + IDX_DIM],
                                jnp.zeros((PAGE_SIZE - R, IDX_DIM), F32)], axis=0)
        ok = (coli >= S_LO) & (coli < S_HI) & (coli <= rowi)
        keysn_ref[...] = jnp.where(ok, score_keys(knew), IMIN)

        def count_ge(cand):
            a = jnp.sum(jnp.where(keysp_ref[...] >= cand[None], 1.0, 0.0), axis=0)
            b = jnp.where(keysn_ref[...] >= cand, 1.0, 0.0)
            return jnp.sum(_fold_lanes(a) + b, axis=1, keepdims=True)

        total = jnp.full((R, 1), float((NS * G + 1) * PAGE_SIZE), F32)
        T, cT = _threshold_search(count_ge, (R, 1), total, kf)
        rid = lax.broadcasted_iota(jnp.int32, (R, 1), 0)
        token_row = (rid >= S_LO) & (rid < S_HI)
        ties = jnp.max(jnp.where((cT > kf) & (T > IMIN) & token_row, 1.0, 0.0)) > 0.0

        @pl.when(ties)
        def _():
            need = kf - count_ge(T + 1)
            triu = jnp.where(lax.broadcasted_iota(jnp.int32, (LANES, LANES), 0)
                             <= lax.broadcasted_iota(jnp.int32, (LANES, LANES), 1), 1.0, 0.0).astype(BF16)

            def demote(blk, seen):
                eq = blk == T
                pre = jnp.dot(jnp.where(eq, 1.0, 0.0).astype(BF16), triu, preferred_element_type=F32)
                return jnp.where(eq & (seen + pre > need), IMIN, blk), seen + pre[:, LANES - 1:LANES]

            def body(j, seen):
                kj = keysp_ref[j]
                cols = []
                for g in range(G):
                    blk, seen = demote(kj[:, g * LANES:(g + 1) * LANES], seen)
                    cols.append(blk)
                keysp_ref[j] = jnp.concatenate(cols, axis=1)
                return seen

            seen = lax.fori_loop(0, NS, body, jnp.zeros((R, 1), F32))
            blk, _ = demote(keysn_ref[...], seen)
            keysn_ref[...] = blk

        tp_ref[...] = jnp.broadcast_to(jnp.maximum(T, IMIN + 1), (R, LANES))


def _dsa_s_index_call(page_table, z, cache_kidx, layer, DB, NP, G, topk):
    NS = NP // G
    GW = G * PAGE_SIZE
    kern = functools.partial(_dsa_s_index_kernel, NS=NS, G=G, topk=topk)
    QW = IDX_HEADS * IDX_DIM
    grid_spec = pltpu.PrefetchScalarGridSpec(
        num_scalar_prefetch=1,
        grid=(DB, NS),
        in_specs=[pl.BlockSpec((SROWS, QW), lambda b, p, pt: (b, Z_IQ // QW)),
                  pl.BlockSpec((SROWS, LANES), lambda b, p, pt: (b, Z_SM // LANES))]
                 + [pl.BlockSpec((None, None, PAGE_SIZE, IDX_DIM), _page_map(layer, NS, G, g)) for g in range(G)],
        out_specs=[pl.BlockSpec((None, NS, SROWS, GW), lambda b, p, pt: (b, 0, 0, 0)),
                   pl.BlockSpec((None, SROWS, LANES), lambda b, p, pt: (b, 0, 0)),
                   pl.BlockSpec((None, SROWS, LANES), lambda b, p, pt: (b, 0, 0))],
    )
    return pl.pallas_call(
        kern,
        grid_spec=grid_spec,
        out_shape=[jax.ShapeDtypeStruct((DB, NS, SROWS, GW), jnp.int32),
                   jax.ShapeDtypeStruct((DB, SROWS, LANES), jnp.int32),
                   jax.ShapeDtypeStruct((DB, SROWS, LANES), jnp.int32)],
        compiler_params=_cparams(("arbitrary", "arbitrary")),
        name="dsa_s_index",
    )(page_table.reshape(-1), z, z, *([cache_kidx] * G))


def _dsa_s_attend_kernel(pt_ref, rbr_ref, q_ref, keysp_ref, keysn_ref, tp_ref, *rest, NS, G, past):
    kp_refs, vp_refs = rest[:G], rest[G:2 * G]
    kn_ref, vn_ref, o_ref, m_scr, l_scr, acc_scr = rest[2 * G:]
    p = pl.program_id(1)
    R, H = SROWS, ATT_HEADS
    HR, PW = H * R, PAGE_SIZE * H
    rowi = lax.broadcasted_iota(jnp.int32, (HR, PW), 0)
    coli = lax.broadcasted_iota(jnp.int32, (HR, PW), 1)
    headmask = _imod(coli, H) == _idiv(rowi, R)
    qpos = past + _imod(rowi, R) - S_LO
    kin = _idiv(coli, H)
    expand = jnp.where(_idiv(lax.broadcasted_iota(jnp.int32, (PAGE_SIZE, PW), 1), H)
                       == lax.broadcasted_iota(jnp.int32, (PAGE_SIZE, PW), 0), 1.0, 0.0).astype(BF16)
    qa = _stack_heads(q_ref[...], H, ATT_DH).astype(BF16)
    Tp = tp_ref[...]
    scale = ATT_DH ** -0.5

    @pl.when(p == 0)
    def _():
        m_scr[...] = jnp.full(m_scr.shape, NEG, F32)
        l_scr[...] = jnp.zeros_like(l_scr)
        acc_scr[...] = jnp.zeros_like(acc_scr)

    def process(pages, near):
        lgs = []
        for ktile, kbase, xk_ref, _ in pages:
            s = lax.dot_general(qa, xk_ref[...].astype(BF16), _DIMS['nt'], preferred_element_type=F32)
            sel = jnp.dot(jnp.where(ktile >= Tp, 1.0, 0.0).astype(BF16), expand, preferred_element_type=F32)
            ok = (jnp.concatenate([sel] * H, axis=0) > 0.5) & headmask
            if near:
                bk = _t5_bucket(jnp.maximum(qpos - (kbase + kin), 0))
                bias = jnp.zeros((HR, PW), F32)
                for jb in range(N_BUCKETS):
                    bias = jnp.where(bk == jb, rbr_ref[:, jb:jb + 1], bias)
            else:
                bias = rbr_ref[:, N_BUCKETS - 1:N_BUCKETS]
            lgs.append(jnp.where(ok, s * scale + bias, NEG))
        mx = lgs[0]
        for lg in lgs[1:]:
            mx = jnp.maximum(mx, lg)
        m_old = m_scr[...]
        m_new = jnp.maximum(m_old, jnp.max(mx, axis=1, keepdims=True))
        corr = jnp.exp(m_old - m_new)
        tot, pv = None, None
        for lg, (_, _, _, xv_ref) in zip(lgs, pages):
            pr = jnp.exp(lg - m_new)
            d = jnp.dot(pr.astype(BF16), xv_ref[...].astype(BF16), preferred_element_type=F32)
            tot = pr if tot is None else tot + pr
            pv = d if pv is None else pv + d
        l_scr[...] = l_scr[...] * corr + jnp.sum(tot, axis=1, keepdims=True)
        acc_scr[...] = acc_scr[...] * corr + pv
        m_scr[...] = m_new

    def cache_pages():
        kt = keysp_ref[...]
        return [(kt[:, g * PAGE_SIZE:(g + 1) * PAGE_SIZE], (p * G + g) * PAGE_SIZE, kp_refs[g], vp_refs[g])
                for g in range(G)]

    @pl.when(p < NS - 1)
    def _():
        process(cache_pages(), False)

    @pl.when(p == NS - 1)
    def _():
        process(cache_pages(), True)
        process([(keysn_ref[...], past - S_LO, kn_ref, vn_ref)], True)
        inv = 1.0 / l_scr[...]
        for h in range(H):
            o_ref[:, h * ATT_DH:(h + 1) * ATT_DH] = acc_scr[h * R:(h + 1) * R, :] * inv[h * R:(h + 1) * R, :]


def _dsa_s_attend_call(page_table, rbrows, z, keysp, keysn, tp, cache_k, cache_v, knew, vnew, layer, DB, NP, G, past):
    NS = NP // G
    GW = G * PAGE_SIZE
    W = ATT_HEADS * ATT_DH
    PW = PAGE_SIZE * ATT_HEADS
    assert G * PAGE_SIZE >= MAX_DISTANCE
    kern = functools.partial(_dsa_s_attend_kernel, NS=NS, G=G, past=past)
    page_specs = [pl.BlockSpec((None, None, PW, ATT_DH), _page_map(layer, NS, G, g)) for g in range(G)]
    grid_spec = pltpu.PrefetchScalarGridSpec(
        num_scalar_prefetch=1,
        grid=(DB, NS),
        in_specs=[pl.BlockSpec((ATT_HEADS * SROWS, LANES), lambda b, p, pt: (0, 0)),
                  pl.BlockSpec((SROWS, W), lambda b, p, pt: (b, Z_AQ // W)),
                  pl.BlockSpec((None, None, SROWS, GW), lambda b, p, pt: (b, p, 0, 0)),
                  pl.BlockSpec((None, SROWS, LANES), lambda b, p, pt: (b, 0, 0)),
                  pl.BlockSpec((None, SROWS, LANES), lambda b, p, pt: (b, 0, 0))]
                 + page_specs + page_specs
                 + [pl.BlockSpec((None, PW, ATT_DH), lambda b, p, pt: (b, 0, 0)),
                    pl.BlockSpec((None, PW, ATT_DH), lambda b, p, pt: (b, 0, 0))],
        out_specs=pl.BlockSpec((SROWS, W), lambda b, p, pt: (b, 0)),
        scratch_shapes=[pltpu.VMEM((ATT_HEADS * SROWS, 1), F32),
                        pltpu.VMEM((ATT_HEADS * SROWS, 1), F32),
                        pltpu.VMEM((ATT_HEADS * SROWS, ATT_DH), F32)],
    )
    return pl.pallas_call(
        kern,
        grid_spec=grid_spec,
        out_shape=jax.ShapeDtypeStruct((DB * SROWS, W), F32),
        compiler_params=_cparams(("arbitrary", "arbitrary")),
        name="dsa_s_attend",
    )(page_table.reshape(-1), rbrows, z, keysp, keysn, tp, *([cache_k] * G), *([cache_v] * G), knew, vnew)


def _merge_kernel(oa_ref, ob_ref, oc_ref, g0_ref, g1_ref, g2_ref, x_ref, wb_ref, wo_ref, h_ref):
    acc = None
    for i, (o_ref, g_ref) in enumerate(((oa_ref, g0_ref), (ob_ref, g1_ref), (oc_ref, g2_ref))):
        br = jnp.dot(o_ref[...].astype(BF16), wb_ref[i], preferred_element_type=F32)
        term = _sigmoid(g_ref[...]) * br
        acc = term if acc is None else acc + term
    h_ref[...] = x_ref[...] + jnp.dot(acc.astype(BF16), wo_ref[...], preferred_element_type=F32)


def _merge_call(oa, ob, oc, z, x, wb, wo, layer, tm):
    M = x.shape[0]
    W = BRANCH_W
    g0 = Z_GATE // D_MODEL
    row = lambda c: (lambda i: (i, c))
    return pl.pallas_call(
        _merge_kernel,
        grid=(M // tm,),
        in_specs=[pl.BlockSpec((tm, W), row(0)), pl.BlockSpec((tm, W), row(0)), pl.BlockSpec((tm, W), row(0)),
                  pl.BlockSpec((tm, D_MODEL), row(g0)), pl.BlockSpec((tm, D_MODEL), row(g0 + 1)),
                  pl.BlockSpec((tm, D_MODEL), row(g0 + 2)),
                  pl.BlockSpec((tm, D_MODEL), row(0)),
                  pl.BlockSpec((None, N_BRANCH, W, D_MODEL), lambda i: (layer, 0, 0, 0)),
                  pl.BlockSpec((None, D_MODEL, D_MODEL), lambda i: (layer, 0, 0))],
        out_specs=pl.BlockSpec((tm, D_MODEL), row(0)),
        out_shape=jax.ShapeDtypeStruct((M, D_MODEL), F32),
        compiler_params=_cparams(("parallel",)),
        name="merge",
    )(oa, ob, oc, z, z, z, x, wb, wo)


def _ffn_down_kernel(a_ref, h_ref, cw_ref, wd_ref, gf_ref, y_ref, prev_scr, act_scr, *, final_norm):
    t = pl.program_id(1)
    tm = a_ref.shape[0]
    FH = D_FF // 2

    @pl.when(t == 0)
    def _():
        prev_scr[...] = jnp.zeros_like(prev_scr)

    acc = h_ref[...]
    for c in range(2):
        gs = slice(c * FH, (c + 1) * FH)
        vs = slice(D_FF + c * FH, D_FF + (c + 1) * FH)
        yg, yg0 = _conv_tile(a_ref[:, gs], prev_scr[:, gs], cw_ref[:, gs], FFN_CONV)
        yv, yv0 = _conv_tile(a_ref[:, vs], prev_scr[:, vs], cw_ref[:, vs], FFN_CONV)
        act_scr[...] = (_silu(yg) * yv).astype(BF16)
        act_scr[0:2 * SUBLANES, :] = jnp.concatenate(
            [_silu(yg0) * yv0, _silu(yg[SUBLANES:2 * SUBLANES]) * yv[SUBLANES:2 * SUBLANES]], axis=0).astype(BF16)
        acc = acc + jnp.dot(act_scr[...], wd_ref[gs, :], preferred_element_type=F32)
    prev_scr[...] = a_ref[tm - SUBLANES:tm, :]
    if final_norm:
        acc = acc * lax.rsqrt(jnp.mean(acc * acc, axis=-1, keepdims=True) + EPS) * gf_ref[...]
    y_ref[...] = acc


def _ffn_down_call(a, h, conv_w, wd, gf, layer, n_outer, tm, final_norm):
    M = h.shape[0]
    nt = M // (n_outer * tm)
    kern = functools.partial(_ffn_down_kernel, final_norm=final_norm)
    return pl.pallas_call(
        kern,
        grid=(n_outer, nt),
        in_specs=[pl.BlockSpec((tm, 2 * D_FF), lambda b, t: (b * nt + t, 0)),
                  pl.BlockSpec((tm, D_MODEL), lambda b, t: (b * nt + t, 0)),
                  pl.BlockSpec((None, FFN_CONV, 2 * D_FF), lambda b, t: (layer, 0, 0)),
                  pl.BlockSpec((None, D_FF, D_MODEL), lambda b, t: (layer, 0, 0)),
                  pl.BlockSpec((1, D_MODEL), lambda b, t: (0, 0))],
        out_specs=pl.BlockSpec((tm, D_MODEL), lambda b, t: (b * nt + t, 0)),
        out_shape=jax.ShapeDtypeStruct((M, D_MODEL), F32),
        scratch_shapes=[pltpu.VMEM((SUBLANES, 2 * D_FF), F32),
                        pltpu.VMEM((tm, D_FF // 2), BF16)],
        compiler_params=_cparams(("arbitrary", "arbitrary")),
        name="ffn_down",
    )(a, h, conv_w, wd, gf.reshape(1, D_MODEL))


def _rope_tables(pos):
    half = RET_DK // 2
    inv = 1.0 / (ROPE_BASE ** jnp.linspace(0.0, 1.0, half, dtype=F32))
    ang = pos.astype(F32)[:, None] * inv
    cos, sin = jnp.cos(ang), jnp.sin(ang)
    return jnp.concatenate([cos, cos], axis=-1), jnp.concatenate([-sin, sin], axis=-1)


def _dn_params(a_log, dt_bias):
    hp = jnp.zeros((SUBLANES, LANES), F32)
    hp = hp.at[0, SM_DNA:SM_DNA + DN_HEADS].set(a_log.astype(F32))
    hp = hp.at[1, SM_DNA:SM_DNA + DN_HEADS].set(dt_bias.astype(F32))
    return hp


def _mix_and_ffn(x, z, oa, ob, oc, sw, l, final, n_outer, tm_merge, tm_up, tn_up, tm_down, ffn_state=None):
    h = _merge_call(oa, ob, oc, z, x, sw['wb'], sw['wo'], l, tm_merge)
    a = _rms_matmul(h, sw['norm_ffn'], sw['w_up'], l, tm_up, tn_up)
    a_raw = a
    if ffn_state is not None:
        DB = ffn_state.shape[0]
        a = a.reshape(DB, SROWS, 2 * D_FF).at[:, S_LO - (FFN_CONV - 1):S_LO].set(ffn_state)
        a = a.reshape(DB * SROWS, 2 * D_FF)
    y = _ffn_down_call(a, h, sw['ffn_conv_w'], sw['wd'], sw['norm_final'], l, n_outer, tm_down, final)
    return y, a_raw


def kernel(x_prompt, x_sample, cache_k, cache_v, cache_kidx, state_dn_conv, state_dn, state_ret,
           state_ffn_conv, page_table, norm_mix, w_in, dn_conv_w, dn_a_log, dn_dt_bias, dn_norm,
           ret_norm, rel_bias, w_branch, w_o, norm_ffn, w_up, ffn_conv_w, w_down, norm_final):
    B, S, D = x_prompt.shape
    DB, DS, _ = x_sample.shape
    depth = w_in.shape[0]
    NP = page_table.shape[1]
    past = NP * PAGE_SIZE
    n_phys = cache_k.shape[1]
    W = ATT_HEADS * ATT_DH
    assert DS == S_HI - S_LO and S % CHUNK == 0 and (DB * SROWS) % CHUNK == 0

    TL = 256 if S % 256 == 0 else CHUNK
    TQ = 256 if S % 256 == 0 else CHUNK
    tm_p = 512 if (B * S) % 512 == 0 else CHUNK
    tm_mm = 1024 if (B * S) % 1024 == 0 else tm_p
    tm_d = 256 if S % 256 == 0 else CHUNK
    MS = DB * SROWS
    NG = MS // CHUNK
    seg_per = CHUNK // SROWS

    xp = x_prompt.reshape(B * S, D)
    xs = jnp.zeros((DB, SROWS, D), F32).at[:, S_LO:S_HI].set(x_sample).reshape(MS, D)
    cos_p, sin_p = _rope_tables(jnp.arange(S))
    pos_s = past + (jnp.arange(CHUNK) % SROWS) - S_LO
    cos_s, sin_s = _rope_tables(pos_s)
    ck = cache_k.reshape(depth, n_phys, PAGE_SIZE * ATT_HEADS, ATT_DH)
    cv = cache_v.reshape(depth, n_phys, PAGE_SIZE * ATT_HEADS, ATT_DH)
    rb = rel_bias.astype(F32)
    rbrows = jnp.pad(jnp.repeat(rb.T, SROWS, axis=0), ((0, 0), (0, LANES - N_BUCKETS)))
    G = next(g for g in (8, 4, 2, 1) if NP % g == 0)
    zeros_p = jnp.zeros((B, DN_HEADS, DN_DK, DN_DV), F32)
    topk_p = min(TOPK_MAX, S // 4)
    topk_s = min(TOPK_MAX, (past + DS) // 4)

    sw = dict(w_in=_prep_w_in(w_in), wb=w_branch.astype(BF16), wo=w_o.astype(BF16), w_up=w_up.astype(BF16),
              wd=w_down.astype(BF16), norm_ffn=norm_ffn, ffn_conv_w=ffn_conv_w, norm_final=norm_final)

    p_states, s_states = [], []
    for l in range(depth):
        lw = dict(dn_conv_w=dn_conv_w[l], hp=_dn_params(dn_a_log[l], dn_dt_bias[l]),
                  dn_norm=dn_norm[l].reshape(1, DN_DV), ret_norm=ret_norm[l])
        final = l == depth - 1

        z = _rms_matmul(xp, norm_mix, sw['w_in'], l, tm_mm, 1024)
        oa, dn_s = _dn_call(z, lw['dn_conv_w'], lw['hp'], lw['dn_norm'], zeros_p, B, TL, CHUNK, 0, CHUNK)
        ob, ret_s = _ret_call(z, cos_p, sin_p, lw['ret_norm'], zeros_p, B, TL, CHUNK, 0, CHUNK)
        k_c = z[:, Z_AK:Z_AK + W]
        v_c = z[:, Z_AV:Z_AV + W]
        vt = jnp.swapaxes(v_c.astype(BF16).reshape(B, S // TQ, TQ, W), 2, 3)
        oc = _dsa_prompt_call(rb, z, k_c.astype(BF16), vt, B, S, TQ, topk_p)
        xp, a_up = _mix_and_ffn(xp, z, oa, ob, oc, sw, l, final, B, tm_p, tm_mm, 1408, tm_d)
        z3 = z.reshape(B, S, Z_COLS)
        p_states.append((z3[:, S - (DN_CONV - 1):, Z_DNQKV:Z_DNQKV + DN_QKV], dn_s, ret_s,
                         k_c.reshape(B, S, ATT_HEADS, ATT_DH), v_c.reshape(B, S, ATT_HEADS, ATT_DH),
                         z3[:, :, Z_SM + SM_IK:Z_SM + SM_IK + IDX_DIM],
                         a_up.reshape(B, S, 2 * D_FF)[:, S - (FFN_CONV - 1):]))

        zs = _rms_matmul(xs, norm_mix, sw['w_in'], l, MS, 1024)
        zs3 = zs.reshape(DB, SROWS, Z_COLS)
        zs_conv = zs3.at[:, S_LO - (DN_CONV - 1):S_LO, Z_DNQKV:Z_DNQKV + DN_QKV].set(state_dn_conv[l])
        zs_conv = zs_conv.reshape(MS, Z_COLS)
        oa, dn_s = _dn_call(zs_conv, lw['dn_conv_w'], lw['hp'], lw['dn_norm'], state_dn[l], NG, CHUNK,
                            SROWS, S_LO, S_HI)
        ob, ret_s = _ret_call(zs, cos_s, sin_s, lw['ret_norm'], state_ret[l], NG, CHUNK, SROWS, S_LO, S_HI)
        keysp, keysn, tp = _dsa_s_index_call(page_table, zs, cache_kidx, l, DB, NP, G, topk_s)
        new_rows = lambda c0: jnp.pad(zs[:, c0:c0 + W].reshape(DB, SROWS * ATT_HEADS, ATT_DH),
                                      ((0, 0), (0, (PAGE_SIZE - SROWS) * ATT_HEADS), (0, 0)))
        oc = _dsa_s_attend_call(page_table, rbrows, zs, keysp, keysn, tp, ck, cv, new_rows(Z_AK), new_rows(Z_AV),
                                l, DB, NP, G, past)
        xs, a_up = _mix_and_ffn(xs, zs, oa, ob, oc, sw, l, final, 1, MS, MS, 1408, MS,
                                ffn_state=state_ffn_conv[l])
        tok = zs3[:, S_LO:S_HI]
        s_states.append((tok[:, DS - (DN_CONV - 1):, Z_DNQKV:Z_DNQKV + DN_QKV], dn_s, ret_s,
                         tok[:, :, Z_AK:Z_AK + W].reshape(DB, DS, ATT_HEADS, ATT_DH),
                         tok[:, :, Z_AV:Z_AV + W].reshape(DB, DS, ATT_HEADS, ATT_DH),
                         tok[:, :, Z_SM + SM_IK:Z_SM + SM_IK + IDX_DIM],
                         a_up.reshape(DB, SROWS, 2 * D_FF)[:, S_HI - (FFN_CONV - 1):S_HI]))

    y_prompt = xp.reshape(B, S, D)
    y_sample = xs.reshape(DB, SROWS, D)[:, S_LO:S_HI]
    stk = lambda states, i: jnp.stack([st[i] for st in states])
    return (y_prompt, y_sample) + tuple(stk(p_states, i) for i in range(7)) + tuple(stk(s_states, i) for i in range(7))
```

```python
import functools
import math

import numpy as np
import jax
import jax.numpy as jnp
from jax import lax
from jax.experimental import pallas as pl
from jax.experimental.pallas import tpu as pltpu

D_MODEL = 1024
DEPTH = 2
PAST_LEN = 8192
PAGE_SIZE = 128
DN_HEADS = 4
DN_DK = 128
DN_DV = 128
DN_CONV = 4
DN_QKV = 2 * DN_HEADS * DN_DK + DN_HEADS * DN_DV
RET_HEADS = 4
RET_DK = 128
RET_DV = 128
ROPE_BASE = 10000.0
ATT_HEADS = 4
ATT_DH = 128
IDX_HEADS = 4
IDX_DIM = 64
TOPK_MAX = 256
N_BUCKETS = 32
MAX_DISTANCE = 128
N_BRANCH = 3
BRANCH_W = DN_HEADS * DN_DV
D_FF = 2816
FFN_CONV = 3
EPS = 1e-6
F32 = jnp.float32
BF16 = jnp.bfloat16
IN_SIZES = (DN_QKV, DN_HEADS * DN_DV, DN_HEADS, DN_HEADS,
            RET_HEADS * RET_DK, RET_HEADS * RET_DK, RET_HEADS * RET_DV, RET_HEADS * RET_DV,
            ATT_HEADS * ATT_DH, ATT_HEADS * ATT_DH, ATT_HEADS * ATT_DH,
            IDX_HEADS * IDX_DIM, IDX_DIM, IDX_HEADS, N_BRANCH * D_MODEL)

Z_DNQKV = 0
Z_DNZ = 1536
Z_GATE = 2048
Z_RQ, Z_RK, Z_RV, Z_RG = 5120, 5632, 6144, 6656
Z_AQ, Z_AK, Z_AV = 7168, 7680, 8192
Z_IQ = 8704
Z_SM = 8960
SM_IK, SM_IW, SM_DNB, SM_DNA = 0, 64, 68, 72
Z_COLS = 9216

LANES = 128
SUBLANES = 8
CHUNK = 128
SROWS = 8
S_LO, S_HI = 3, 7
NEG = -1e30
IMIN = -2 ** 31
HALF = 2 ** 15
PACK16 = 2 * SUBLANES
VMEM_LIMIT = 56 * 1024 * 1024


def _cparams(sem):
    return pltpu.CompilerParams(dimension_semantics=sem, vmem_limit_bytes=VMEM_LIMIT)


def _sigmoid(x):
    return 1.0 / (1.0 + jnp.exp(-x))


def _silu(x):
    return x * _sigmoid(x)


def _softplus(x):
    return jnp.maximum(x, 0.0) + jnp.log(1.0 + jnp.exp(-jnp.abs(x)))


_DIMS = {'nn': (((1,), (0,)), ((), ())), 'nt': (((1,), (1,)), ((), ())), 'tn': (((0,), (0,)), ((), ()))}


def _split_bf16(a, n):
    parts = []
    r = a
    for i in range(n):
        p = r.astype(BF16)
        parts.append(p)
        if i + 1 < n:
            r = r - p.astype(F32)
    return parts


def _mm(a, b, dims='nn', mode='bf16'):
    dn = _DIMS[dims]
    dg = lambda x, y: lax.dot_general(x, y, dn, preferred_element_type=F32)
    if mode == 'bf16':
        return dg(a.astype(BF16), b.astype(BF16))
    if mode == 'x3':
        ah, al = _split_bf16(a, 2)
        bh, bl = _split_bf16(b, 2)
        return dg(ah, bh) + dg(ah, bl) + dg(al, bh)
    if mode == 'l01':
        ab = a.astype(BF16)
        b1, b2, b3 = _split_bf16(b, 3)
        return dg(ab, b1) + dg(ab, b2) + dg(ab, b3)
    raise ValueError(mode)


def _rms_mm_kernel(x_ref, g_ref, w_ref, o_ref, u_ref):
    @pl.when(pl.program_id(1) == 0)
    def _():
        x = x_ref[...]
        r = lax.rsqrt(jnp.mean(x * x, axis=-1, keepdims=True) + EPS)
        u_ref[...] = (x * r * g_ref[...]).astype(u_ref.dtype)

    o_ref[...] = jnp.dot(u_ref[...], w_ref[...], preferred_element_type=F32)


def _rms_matmul(x, g, w, layer, tm, tn):
    M, K = x.shape
    N = w.shape[2]
    return pl.pallas_call(
        _rms_mm_kernel,
        grid=(M // tm, N // tn),
        in_specs=[pl.BlockSpec((tm, K), lambda i, j: (i, 0)),
                  pl.BlockSpec((None, 1, K), lambda i, j: (layer, 0, 0)),
                  pl.BlockSpec((None, K, tn), lambda i, j: (layer, 0, j))],
        out_specs=pl.BlockSpec((tm, tn), lambda i, j: (i, j)),
        out_shape=jax.ShapeDtypeStruct((M, N), F32),
        scratch_shapes=[pltpu.VMEM((tm, K), BF16)],
        compiler_params=_cparams(("parallel", "arbitrary")),
        name="rms_matmul",
    )(x, g.reshape(g.shape[0], 1, K), w)


_IN_OFFS = [0] + np.cumsum(np.array(IN_SIZES)).tolist()
_SRC_DNB, _SRC_RQ, _SRC_IK, _SRC_IW, _SRC_GATE, IN_COLS = (_IN_OFFS[2], _IN_OFFS[4], _IN_OFFS[12], _IN_OFFS[13],
                                                           _IN_OFFS[14], _IN_OFFS[15])
IN_COLS_PAD = -(-IN_COLS // LANES) * LANES


def _prep_w_in_kernel(w_ref, o_ref):
    def shifted(src, width):
        a = src // LANES * LANES
        win = -(-(src - a + width) // LANES) * LANES
        return pltpu.roll(w_ref[:, a:a + win], win - (src - a), axis=1)[:, 0:width]

    n_head = Z_GATE
    o_ref[:, 0:n_head] = w_ref[:, 0:n_head].astype(BF16)
    o_ref[:, Z_GATE:Z_GATE + N_BRANCH * D_MODEL] = shifted(_SRC_GATE, N_BRANCH * D_MODEL).astype(BF16)
    o_ref[:, Z_RQ:Z_SM] = shifted(_SRC_RQ, Z_SM - Z_RQ).astype(BF16)
    lane = lax.broadcasted_iota(jnp.int32, (w_ref.shape[0], LANES), 1)
    n_idx = IDX_DIM + IDX_HEADS
    a_ik = _SRC_IK // LANES * LANES
    idx_part = pltpu.roll(w_ref[:, a_ik:a_ik + LANES], LANES - (_SRC_IK - a_ik), axis=1)
    dn_part = pltpu.roll(w_ref[:, _SRC_DNB:_SRC_DNB + LANES], SM_DNB, axis=1)
    small = jnp.where(lane < n_idx, idx_part, jnp.where(lane < n_idx + 2 * DN_HEADS, dn_part, 0.0))
    o_ref[:, Z_SM:Z_SM + LANES] = small.astype(BF16)
    o_ref[:, Z_SM + LANES:Z_COLS] = jnp.zeros((w_ref.shape[0], Z_COLS - Z_SM - LANES), BF16)


def _prep_w_in(w_in):
    depth, K, _ = w_in.shape
    assert _SRC_DNB % LANES == 0 and _IN_OFFS[1] == Z_DNZ and _SRC_DNB == Z_GATE
    assert _SRC_IW - _SRC_IK == IDX_DIM and _SRC_GATE - _SRC_IW == IDX_HEADS
    assert (_SRC_IK % LANES) + IDX_DIM + IDX_HEADS <= LANES and SM_DNB == IDX_DIM + IDX_HEADS
    assert Z_SM - Z_RQ == _SRC_IK - _SRC_RQ and SM_DNA == SM_DNB + DN_HEADS
    tr = 256
    wp = jnp.pad(w_in, ((0, 0), (0, 0), (0, IN_COLS_PAD - IN_COLS)))
    return pl.pallas_call(
        _prep_w_in_kernel,
        grid=(depth, K // tr),
        in_specs=[pl.BlockSpec((None, tr, IN_COLS_PAD), lambda l, i: (l, i, 0))],
        out_specs=pl.BlockSpec((None, tr, Z_COLS), lambda l, i: (l, i, 0)),
        out_shape=jax.ShapeDtypeStruct((depth, K, Z_COLS), BF16),
        compiler_params=_cparams(("parallel", "parallel")),
        name="prep_w_in",
    )(wp)


def _conv_tile(x, prev8, w, width):
    y = x * w[width - 1:width, :]
    for s in range(1, width):
        y = y + pltpu.roll(x, s, axis=0) * w[width - 1 - s:width - s, :]
    x0 = x[0:SUBLANES, :]
    rid = lax.broadcasted_iota(jnp.int32, x0.shape, 0)
    y0 = x0 * w[width - 1:width, :]
    for s in range(1, width):
        xs = jnp.where(rid < s, pltpu.roll(prev8, s, axis=0), pltpu.roll(x0, s, axis=0))
        y0 = y0 + xs * w[width - 1 - s:width - s, :]
    return y, y0


def _idiv(x, n):
    assert n & (n - 1) == 0
    return lax.shift_right_arithmetic(x, jnp.int32(n.bit_length() - 1))


def _imod(x, n):
    assert n & (n - 1) == 0
    return x & jnp.int32(n - 1)


def _chunk_masks(C, seg):
    ri = lax.broadcasted_iota(jnp.int32, (C, C), 0)
    ci = lax.broadcasted_iota(jnp.int32, (C, C), 1)
    if seg == C:
        return ri >= ci, ri > ci, None
    same = _idiv(ri, seg) == _idiv(ci, seg)
    return (ri >= ci) & same, (ri > ci) & same, same


def _valid_col(C, seg, lo, hi):
    r = _imod(lax.broadcasted_iota(jnp.int32, (C, 1), 0), seg)
    return jnp.where((r >= lo) & (r < hi), 1.0, 0.0)


def _tri_inv(ms, span):
    C = ms[0].shape[0]
    eye = jnp.where(lax.broadcasted_iota(jnp.int32, (C, C), 0) == lax.broadcasted_iota(jnp.int32, (C, C), 1),
                    1.0, 0.0)
    invs = [eye - m for m in ms]
    ps = list(ms)
    n = 2
    while n < span:
        ps = [_mm(p, p, 'nn', 'x3') for p in ps]
        invs = [inv + _mm(inv, p, 'nn', 'x3') for inv, p in zip(invs, ps)]
        n *= 2
    return invs


def _state_update(S_scr, h, u, kcum, qd, kd, qk, gtot, C, seg):
    nseg = C // seg
    ws, o1s = [], []
    for sg in range(nseg):
        rs = slice(sg * seg, (sg + 1) * seg)
        S = S_scr[sg, h]
        if kcum is None:
            ws.append(u[rs])
        else:
            ws.append(u[rs] - _mm(kcum[rs], S))
        o1s.append(_mm(qd[rs], S))
    w = ws[0] if nseg == 1 else jnp.concatenate(ws, axis=0)
    o1 = o1s[0] if nseg == 1 else jnp.concatenate(o1s, axis=0)
    o = o1 + _mm(qk, w)
    rowid = lax.broadcasted_iota(jnp.int32, (C, 1), 0)
    for sg in range(nseg):
        kdm = kd if nseg == 1 else jnp.where(_idiv(rowid, seg) == sg, kd, 0.0)
        gt = jnp.exp(gtot[sg * seg:sg * seg + 1, :])
        S_scr[sg, h] = S_scr[sg, h] * gt + _mm(kdm, w, 'tn')
    return o


def _dn_kernel(qkv_ref, dz_ref, sm_ref, cw_ref, hp_ref, nrm_ref, s0_ref, o_ref, sfin_ref,
               S_scr, prev_scr, c_scr, *, C, seg, lo, hi):
    t = pl.program_id(1)
    TL = qkv_ref.shape[0]
    H, DK = DN_HEADS, DN_DK
    masked = seg != C

    @pl.when(t == 0)
    def _():
        S_scr[...] = s0_ref[...]
        prev_scr[...] = jnp.zeros_like(prev_scr)

    x = qkv_ref[...]
    y, y0 = _conv_tile(x, prev_scr[...], cw_ref[...], DN_CONV)
    c_scr[...] = _silu(y)
    c_scr[0:SUBLANES, :] = _silu(y0)
    prev_scr[...] = x[TL - SUBLANES:TL, :]

    lowm, strictm, same = _chunk_masks(C, seg)
    ltri = jnp.where(lowm, 1.0, 0.0)
    valid = _valid_col(C, seg, lo, hi) if masked else None
    span = (hi - lo) if masked else C
    a_coef = -jnp.exp(hp_ref[0:1, :])
    dtb = hp_ref[1:2, :]

    units = []
    for cidx in range(TL // C):
        r0 = cidx * C
        cc = c_scr[r0:r0 + C, :]
        sm = sm_ref[r0:r0 + C, :]
        g128 = a_coef * _softplus(sm + dtb)
        b128 = _sigmoid(sm)
        if masked:
            g128 = g128 * valid
            b128 = b128 * valid
        Gc128 = _mm(ltri, g128, 'nn', 'l01')
        if masked:
            Gt128 = _mm(jnp.where(same, 1.0, 0.0), g128, 'nn', 'l01')
        else:
            Gt128 = jnp.broadcast_to(Gc128[C - 1:C, :], Gc128.shape)
        GT = Gc128.T
        for h in range(H):
            q = cc[:, h * DK:(h + 1) * DK]
            k = cc[:, (H + h) * DK:(H + h + 1) * DK]
            v = cc[:, (2 * H + h) * DK:(2 * H + h + 1) * DK]
            q = q * lax.rsqrt(jnp.sum(q * q, axis=-1, keepdims=True) + EPS) * DK ** -0.5
            k = k * lax.rsqrt(jnp.sum(k * k, axis=-1, keepdims=True) + EPS)
            if masked:
                k = k * valid
            Gc = Gc128[:, SM_DNA + h:SM_DNA + h + 1]
            Gr = GT[SM_DNA + h:SM_DNA + h + 1, :]
            Gt = Gt128[:, SM_DNA + h:SM_DNA + h + 1]
            bc = b128[:, SM_DNB + h:SM_DNB + h + 1]
            decay = jnp.where(lowm, jnp.exp(jnp.where(lowm, Gc - Gr, 0.0)), 0.0)
            eG = jnp.exp(Gc)
            units.append(dict(r0=r0, h=h, Gt=Gt, m=jnp.where(strictm, _mm(k, k, 'nt') * decay * bc, 0.0),
                              qk=_mm(q, k, 'nt') * decay, rhs_u=v * bc, rhs_k=k * (bc * eG),
                              qd=q * eG, kd=k * jnp.exp(Gt - Gc)))
    ainvs = _tri_inv([un['m'] for un in units], span)
    for un, ainv in zip(units, ainvs):
        un['u'] = _mm(ainv, un['rhs_u'], 'nn', 'x3')
        un['kcum'] = _mm(ainv, un['rhs_k'], 'nn', 'x3')

    for un in units:
        r0, h = un['r0'], un['h']
        o = _state_update(S_scr, h, un['u'], un['kcum'], un['qd'], un['kd'], un['qk'], un['Gt'], C, seg)
        on = o * lax.rsqrt(jnp.mean(o * o, axis=-1, keepdims=True) + EPS) * nrm_ref[...]
        zg = dz_ref[r0:r0 + C, h * DN_DV:(h + 1) * DN_DV]
        o_ref[r0:r0 + C, h * DN_DV:(h + 1) * DN_DV] = on * _silu(zg)

    @pl.when(t == pl.num_programs(1) - 1)
    def _():
        sfin_ref[...] = S_scr[...]


def _dn_call(z, conv_w, hp, nrm, s0, n_outer, TL, seg, lo, hi):
    M = z.shape[0]
    nt = M // (n_outer * TL)
    nseg = CHUNK // seg
    rowmap = lambda cb: (lambda b, t: (b * nt + t, cb))
    kern = functools.partial(_dn_kernel, C=CHUNK, seg=seg, lo=lo, hi=hi)
    return pl.pallas_call(
        kern,
        grid=(n_outer, nt),
        in_specs=[pl.BlockSpec((TL, DN_QKV), rowmap(Z_DNQKV // DN_QKV)),
                  pl.BlockSpec((TL, BRANCH_W), rowmap(Z_DNZ // BRANCH_W)),
                  pl.BlockSpec((TL, LANES), rowmap(Z_SM // LANES)),
                  pl.BlockSpec((DN_CONV, DN_QKV), lambda b, t: (0, 0)),
                  pl.BlockSpec((SUBLANES, LANES), lambda b, t: (0, 0)),
                  pl.BlockSpec((1, DN_DV), lambda b, t: (0, 0)),
                  pl.BlockSpec((nseg, DN_HEADS, DN_DK, DN_DV), lambda b, t: (b, 0, 0, 0))],
        out_specs=[pl.BlockSpec((TL, BRANCH_W), lambda b, t: (b * nt + t, 0)),
                   pl.BlockSpec((nseg, DN_HEADS, DN_DK, DN_DV), lambda b, t: (b, 0, 0, 0))],
        out_shape=[jax.ShapeDtypeStruct((M, BRANCH_W), F32),
                   jax.ShapeDtypeStruct(s0.shape, F32)],
        scratch_shapes=[pltpu.VMEM((nseg, DN_HEADS, DN_DK, DN_DV), F32),
                        pltpu.VMEM((SUBLANES, DN_QKV), F32),
                        pltpu.VMEM((TL, DN_QKV), F32)],
        compiler_params=_cparams(("arbitrary", "arbitrary")),
        name="dn",
    )(z, z, z, conv_w, hp, nrm, s0)


_LOG_GAMMA = [float(np.log1p(-np.exp2(-5.0 - h))) for h in range(RET_HEADS)]


def _ret_kernel(q_ref, k_ref, v_ref, g_ref, cos_ref, sin_ref, nrm_ref, s0_ref, o_ref, sfin_ref,
                S_scr, *, C, seg, lo, hi):
    t = pl.program_id(1)
    TL = q_ref.shape[0]
    H, DK = RET_HEADS, RET_DK
    masked = seg != C

    @pl.when(t == 0)
    def _():
        S_scr[...] = s0_ref[...]

    lowm, _, _ = _chunk_masks(C, seg)
    ri = lax.broadcasted_iota(jnp.int32, (C, 1), 0)
    ci = lax.broadcasted_iota(jnp.int32, (1, C), 1)
    if masked:
        valid = _valid_col(C, seg, lo, hi)
        cnt_c = jnp.clip(_imod(ri, seg) - lo + 1, 0, hi - lo).astype(F32)
        cnt_r = jnp.clip(_imod(ci, seg) - lo + 1, 0, hi - lo).astype(F32)
        cnt_t = float(hi - lo)
    else:
        valid = None
        cnt_c = (ri + 1).astype(F32)
        cnt_r = (ci + 1).astype(F32)
        cnt_t = float(C)

    for cidx in range(TL // C):
        r0 = cidx * C
        cosf = cos_ref[r0:r0 + C, :]
        sins = sin_ref[r0:r0 + C, :]
        for h in range(H):
            cs = slice(h * DK, (h + 1) * DK)
            q = q_ref[r0:r0 + C, cs]
            k = k_ref[r0:r0 + C, cs]
            v = v_ref[r0:r0 + C, cs]
            q = (q * cosf + pltpu.roll(q, DK // 2, axis=1) * sins) * DK ** -0.5
            k = k * cosf + pltpu.roll(k, DK // 2, axis=1) * sins
            if masked:
                v = v * valid
            lg = _LOG_GAMMA[h]
            Gc = cnt_c * lg
            decay = jnp.where(lowm, jnp.exp(jnp.where(lowm, (cnt_c - cnt_r) * lg, 0.0)), 0.0)
            qk = _mm(q, k, 'nt') * decay
            Gt = jnp.full((C, 1), cnt_t * lg, F32)
            o = _state_update(S_scr, h, v, None, q * jnp.exp(Gc), k * jnp.exp(Gt - Gc), qk, Gt, C, seg)
            mu = jnp.mean(o, axis=-1, keepdims=True)
            oc = o - mu
            var = jnp.mean(oc * oc, axis=-1, keepdims=True)
            on = oc * lax.rsqrt(var + EPS) * nrm_ref[h:h + 1, :]
            o_ref[r0:r0 + C, cs] = on * _silu(g_ref[r0:r0 + C, cs])

    @pl.when(t == pl.num_programs(1) - 1)
    def _():
        sfin_ref[...] = S_scr[...]


def _ret_call(z, cosf, sins, nrm, s0, n_outer, TL, seg, lo, hi):
    M = z.shape[0]
    nt = M // (n_outer * TL)
    nseg = CHUNK // seg
    W = RET_HEADS * RET_DK
    rowmap = lambda cb: (lambda b, t: (b * nt + t, cb))
    kern = functools.partial(_ret_kernel, C=CHUNK, seg=seg, lo=lo, hi=hi)
    return pl.pallas_call(
        kern,
        grid=(n_outer, nt),
        in_specs=[pl.BlockSpec((TL, W), rowmap(Z_RQ // W)),
                  pl.BlockSpec((TL, W), rowmap(Z_RK // W)),
                  pl.BlockSpec((TL, W), rowmap(Z_RV // W)),
                  pl.BlockSpec((TL, W), rowmap(Z_RG // W)),
                  pl.BlockSpec((TL, RET_DK), lambda b, t: (t, 0)),
                  pl.BlockSpec((TL, RET_DK), lambda b, t: (t, 0)),
                  pl.BlockSpec((RET_HEADS, RET_DV), lambda b, t: (0, 0)),
                  pl.BlockSpec((nseg, RET_HEADS, RET_DK, RET_DV), lambda b, t: (b, 0, 0, 0))],
        out_specs=[pl.BlockSpec((TL, W), lambda b, t: (b * nt + t, 0)),
                   pl.BlockSpec((nseg, RET_HEADS, RET_DK, RET_DV), lambda b, t: (b, 0, 0, 0))],
        out_shape=[jax.ShapeDtypeStruct((M, W), F32),
                   jax.ShapeDtypeStruct(s0.shape, F32)],
        scratch_shapes=[pltpu.VMEM((nseg, RET_HEADS, RET_DK, RET_DV), F32)],
        compiler_params=_cparams(("arbitrary", "arbitrary")),
        name="ret",
    )(z, z, z, z, cosf, sins, nrm, s0)


def _f2key(x):
    b = lax.bitcast_convert_type(x + 0.0, jnp.int32)
    return jnp.where(b >= 0, b, b ^ jnp.int32(0x7FFFFFFF))


def _t5_bucket(d):
    exact = N_BUCKETS // 2
    df = d.astype(F32)
    large = exact + (jnp.log(jnp.maximum(df, 1.0) / exact) / math.log(MAX_DISTANCE / exact)
                     * (N_BUCKETS - exact)).astype(jnp.int32)
    large = jnp.minimum(large, N_BUCKETS - 1)
    return jnp.where(d < exact, d, large)


def _bias_from_dist(d, rb_ref, h):
    bk = _t5_bucket(d)
    r = jnp.zeros(d.shape, F32)
    for jb in range(N_BUCKETS):
        r = jnp.where(bk == jb, rb_ref[jb, h], r)
    return r


def _threshold_search(count_ge, shape, total, kf, nbits=32):
    zero = jnp.zeros(shape, jnp.int32)
    c0 = count_ge(zero)
    ok0 = c0 >= kf
    T = jnp.where(ok0, 0, -2 ** (nbits - 1)).astype(jnp.int32)
    cT = jnp.where(ok0, c0, total)

    def body(it, carry):
        T, cT = carry
        cand = T + lax.shift_left(jnp.int32(1), jnp.int32(nbits - 2) - it)
        c = count_ge(cand)
        ok = c >= kf
        return jnp.where(ok, cand, T), jnp.where(ok, c, cT)

    return lax.fori_loop(0, nbits - 1, body, (T, cT))


def _fold_lanes(x):
    f = x[:, 0:LANES]
    for u in range(1, x.shape[1] // LANES):
        f = f + x[:, u * LANES:(u + 1) * LANES]
    return f


def _fold_rows(x):
    return jnp.sum(x.reshape(x.shape[0] // SUBLANES, SUBLANES, x.shape[1]), axis=0)


def _dsa_prompt_kernel(rb_ref, q_ref, qi_ref, smq_ref, k_ref, vt_ref, smk_ref, o_ref,
                       keys_scr, hi_scr, lo_scr, lg_scr, *, TQ, topk):
    i = pl.program_id(1)
    KC = TQ
    nk = i + 1
    kf = float(topk)
    qi = qi_ref[...]
    wT = smq_ref[...].T
    kpos0 = lax.broadcasted_iota(jnp.int32, (KC, TQ), 0)
    qidx = lax.broadcasted_iota(jnp.int32, (KC, TQ), 1)
    lane = lax.broadcasted_iota(jnp.int32, (KC, LANES), 1)

    def q_operand(e):
        eh = e.astype(BF16)
        hf = eh.astype(F32)
        lf = (e - hf).astype(BF16).astype(F32)
        return jnp.concatenate([(hf + pltpu.roll(lf, IDX_DIM, axis=1)).astype(BF16), eh], axis=1)

    q_ops = []
    for h in range(IDX_HEADS):
        slab = qi[:, (h // 2) * LANES:(h // 2 + 1) * LANES]
        if h % 2 == 0:
            q_ops.append(q_operand(jnp.where(lane < IDX_DIM, slab, 0.0)))
        else:
            q_ops.append(q_operand(pltpu.roll(jnp.where(lane >= IDX_DIM, slab, 0.0), IDX_DIM, axis=1)))

    def p1(j, c):
        r0 = pl.multiple_of(j * KC, KC)
        k0 = jnp.where(lane < IDX_DIM, smk_ref[pl.ds(r0, KC), :], 0.0)
        kh2 = (k0 + pltpu.roll(k0, IDX_DIM, axis=1)).astype(BF16)
        kl = (k0 - k0.astype(BF16).astype(F32)).astype(BF16)
        k_op = jnp.concatenate([kh2, kl], axis=1)
        acc = jnp.zeros((KC, TQ), F32)
        for h in range(IDX_HEADS):
            s = lax.dot_general(k_op, q_ops[h], _DIMS['nt'], preferred_element_type=F32)
            acc = acc + jnp.maximum(s, 0.0) * wT[SM_IW + h:SM_IW + h + 1, :]
        key = _f2key(acc * (IDX_DIM ** -0.5 * IDX_HEADS ** -0.5))
        key = jnp.where(kpos0 + r0 <= qidx + i * TQ, key, IMIN)
        keys_scr[j] = key
        hi_scr[j] = lax.shift_right_arithmetic(key, 16).astype(jnp.int16)
        lo_scr[j] = ((key & 0xFFFF) - HALF).astype(jnp.int16)
        return c

    lax.fori_loop(0, nk, p1, 0)

    def count_ge(cand):
        def body(j, part):
            return part + _fold_rows(jnp.where(keys_scr[j] >= cand, 1.0, 0.0))
        part = lax.fori_loop(0, nk, body, jnp.zeros((SUBLANES, TQ), F32))
        return jnp.sum(part, axis=0, keepdims=True)

    def count_ge16(scr):
        def count(cand):
            c16 = cand.astype(jnp.int16)

            def body(j, part):
                ind = jnp.where(scr[j] >= c16, jnp.int16(1), jnp.int16(0))
                for r in range(KC // PACK16):
                    part = part + ind[r * PACK16:(r + 1) * PACK16, :]
                return part
            part = lax.fori_loop(0, nk, body, jnp.zeros((PACK16, TQ), jnp.int16))
            return jnp.sum(part.astype(F32), axis=0, keepdims=True)
        return count

    total = (nk * KC).astype(F32)
    T_hi, _ = _threshold_search(count_ge16(hi_scr), (1, TQ), total, kf, 16)
    c_gt = jnp.where(T_hi >= HALF - 1, 0.0, count_ge16(hi_scr)(jnp.minimum(T_hi + 1, HALF - 1)))
    t16 = T_hi.astype(jnp.int16)

    def keep_class(j, c):
        lo_scr[j] = jnp.where(hi_scr[j] == t16, lo_scr[j], jnp.int16(-HALF))
        return c

    lax.fori_loop(0, nk, keep_class, 0)
    T_lo, c_lo = _threshold_search(count_ge16(lo_scr), (1, TQ), total, kf - c_gt, 16)
    T = T_hi * (2 * HALF) + (T_lo + HALF)
    cT = c_gt + c_lo

    ties = jnp.max(jnp.where((cT > kf) & (T > IMIN), 1.0, 0.0)) > 0.0

    @pl.when(ties)
    def _():
        need = kf - count_ge(T + 1)
        tril = jnp.where(lax.broadcasted_iota(jnp.int32, (KC, KC), 0)
                         >= lax.broadcasted_iota(jnp.int32, (KC, KC), 1), 1.0, 0.0).astype(BF16)

        def body(j, seen):
            kj = keys_scr[j]
            eq = kj == T
            pre = jnp.dot(tril, jnp.where(eq, 1.0, 0.0).astype(BF16), preferred_element_type=F32)
            keys_scr[j] = jnp.where(eq & (seen + pre > need), IMIN, kj)
            return seen + pre[KC - 1:KC, :]

        lax.fori_loop(0, nk, body, jnp.zeros((1, TQ), F32))

    Tp = jnp.maximum(T, IMIN + 1)

    cidx = lax.broadcasted_iota(jnp.int32, (1, 2 * KC), 1)
    e = jnp.where(cidx < KC, cidx, cidx - 2 * KC)
    scale = ATT_DH ** -0.5
    jprev = jnp.maximum(i - 1, 0)

    def toeplitz(r):
        y = pltpu.roll(jnp.broadcast_to(r, (KC, 2 * KC)), 0, 1, stride=1, stride_axis=0)
        return y[:, 0:TQ]

    def to_mask(j, c):
        keys_scr[j] = lax.bitcast_convert_type(jnp.where(keys_scr[j] >= Tp, 0.0, NEG), jnp.int32)
        return c

    lax.fori_loop(0, nk, to_mask, 0)

    heads = range(ATT_HEADS)
    hcols = [slice(h * ATT_DH, (h + 1) * ATT_DH) for h in heads]
    qhs = [q_ref[:, cs].astype(BF16) for cs in hcols]

    def pass_a(j, biases, ms):
        r0 = pl.multiple_of(j * KC, KC)
        madd = lax.bitcast_convert_type(keys_scr[j], F32)
        out = []
        for h in heads:
            lg = lax.dot_general(k_ref[pl.ds(r0, KC), hcols[h]], qhs[h], _DIMS['nt'], preferred_element_type=F32)
            lg = lg * scale + biases[h] + madd
            lg_scr[h, j] = lg
            out.append(jnp.maximum(ms[h], jnp.max(lg, axis=0, keepdims=True)))
        return tuple(out)

    far_bias = [rb_ref[N_BUCKETS - 1, h] for h in heads]
    ms = lax.fori_loop(0, jprev, lambda j, ms: pass_a(j, far_bias, ms),
                       tuple(jnp.full((1, TQ), NEG, F32) for _ in heads))
    ms = pass_a(jprev, [toeplitz(_bias_from_dist(jnp.maximum(KC + e, 0), rb_ref, h)) for h in heads], ms)
    ms = pass_a(i, [toeplitz(_bias_from_dist(jnp.maximum(e, 0), rb_ref, h)) for h in heads], ms)

    def pass_b(j, carry):
        out = []
        for h in heads:
            l, acc = carry[h]
            p = jnp.exp(lg_scr[h, j] - ms[h])
            out.append((l + jnp.sum(p, axis=0, keepdims=True),
                        acc + jnp.dot(vt_ref[j, hcols[h], :], p.astype(BF16), preferred_element_type=F32)))
        return tuple(out)

    res = lax.fori_loop(0, nk, pass_b,
                        tuple((jnp.zeros((1, TQ), F32), jnp.zeros((ATT_DH, TQ), F32)) for _ in heads))
    for h in heads:
        l, acc = res[h]
        o_ref[:, hcols[h]] = (acc / l).T


def _dsa_prompt_call(rel_bias, z, kb, vt, B, L, TQ, topk):
    nq = L // TQ
    W = ATT_HEADS * ATT_DH
    kern = functools.partial(_dsa_prompt_kernel, TQ=TQ, topk=topk)
    return pl.pallas_call(
        kern,
        grid=(B, nq),
        in_specs=[pl.BlockSpec(memory_space=pltpu.SMEM),
                  pl.BlockSpec((TQ, W), lambda b, i: (b * nq + i, Z_AQ // W)),
                  pl.BlockSpec((TQ, IDX_HEADS * IDX_DIM), lambda b, i: (b * nq + i, Z_IQ // (IDX_HEADS * IDX_DIM))),
                  pl.BlockSpec((TQ, LANES), lambda b, i: (b * nq + i, Z_SM // LANES)),
                  pl.BlockSpec((L, W), lambda b, i: (b, 0)),
                  pl.BlockSpec((None, nq, W, TQ), lambda b, i: (b, 0, 0, 0)),
                  pl.BlockSpec((L, LANES), lambda b, i: (b, Z_SM // LANES))],
        out_specs=pl.BlockSpec((TQ, W), lambda b, i: (b * nq + i, 0)),
        out_shape=jax.ShapeDtypeStruct((B * L, W), F32),
        scratch_shapes=[pltpu.VMEM((nq, TQ, TQ), jnp.int32), pltpu.VMEM((nq, TQ, TQ), jnp.int16),
                        pltpu.VMEM((nq, TQ, TQ), jnp.int16), pltpu.VMEM((ATT_HEADS, nq, TQ, TQ), F32)],
        compiler_params=_cparams(("arbitrary", "arbitrary")),
        name="dsa_prompt",
    )(rel_bias, z, z, z, kb, vt, z)


def _stack_heads(x, nh, w):
    return jnp.concatenate([x[:, h * w:(h + 1) * w] for h in range(nh)], axis=0)


def _page_map(layer, NS, G, g):
    return lambda b, p, pt: (layer, pt[(b * NS + p) * G + g], 0, 0)


def _dsa_s_index_kernel(pt_ref, qi_ref, sm_ref, *rest, NS, G, topk):
    kp_refs = rest[:G]
    keysp_ref, keysn_ref, tp_ref = rest[G:]
    p = pl.program_id(1)
    kf = float(topk)
    R = SROWS
    qs = _stack_heads(qi_ref[...], IDX_HEADS, IDX_DIM)
    wcol = _stack_heads(sm_ref[:, SM_IW:SM_IW + IDX_HEADS], IDX_HEADS, 1)

    def score_keys(kmat, dims):
        s = _mm(qs, kmat, dims, 'x3')
        t = jnp.maximum(s, 0.0) * wcol
        acc = t[0:R]
        for h in range(1, IDX_HEADS):
            acc = acc + t[h * R:(h + 1) * R]
        return _f2key(acc * (IDX_DIM ** -0.5 * IDX_HEADS ** -0.5))

    keysp_ref[p] = score_keys(jnp.concatenate([r[...] for r in kp_refs], axis=1), 'nn')

    @pl.when(p == NS - 1)
    def _():
        rowi = lax.broadcasted_iota(jnp.int32, (R, LANES), 0)
        coli = lax.broadcasted_iota(jnp.int32, (R, LANES), 1)
        knew = jnp.concatenate([sm_ref[:, SM_IK:SM_IK + IDX_DIM],
                                jnp.zeros((PAGE_SIZE - R, IDX_DIM), F32)], axis=0)
        ok = (coli >= S_LO) & (coli < S_HI) & (coli <= rowi)
        keysn_ref[...] = jnp.where(ok, score_keys(knew, 'nt'), IMIN)

        def count_ge(cand):
            a = jnp.sum(jnp.where(keysp_ref[...] >= cand[None], 1.0, 0.0), axis=0)
            b = jnp.where(keysn_ref[...] >= cand, 1.0, 0.0)
            return jnp.sum(_fold_lanes(a) + b, axis=1, keepdims=True)

        total = jnp.full((R, 1), float((NS * G + 1) * PAGE_SIZE), F32)
        T, cT = _threshold_search(count_ge, (R, 1), total, kf)
        rid = lax.broadcasted_iota(jnp.int32, (R, 1), 0)
        token_row = (rid >= S_LO) & (rid < S_HI)
        ties = jnp.max(jnp.where((cT > kf) & (T > IMIN) & token_row, 1.0, 0.0)) > 0.0

        @pl.when(ties)
        def _():
            need = kf - count_ge(T + 1)
            triu = jnp.where(lax.broadcasted_iota(jnp.int32, (LANES, LANES), 0)
                             <= lax.broadcasted_iota(jnp.int32, (LANES, LANES), 1), 1.0, 0.0).astype(BF16)

            def demote(blk, seen):
                eq = blk == T
                pre = jnp.dot(jnp.where(eq, 1.0, 0.0).astype(BF16), triu, preferred_element_type=F32)
                return jnp.where(eq & (seen + pre > need), IMIN, blk), seen + pre[:, LANES - 1:LANES]

            def body(j, seen):
                kj = keysp_ref[j]
                cols = []
                for g in range(G):
                    blk, seen = demote(kj[:, g * LANES:(g + 1) * LANES], seen)
                    cols.append(blk)
                keysp_ref[j] = jnp.concatenate(cols, axis=1)
                return seen

            seen = lax.fori_loop(0, NS, body, jnp.zeros((R, 1), F32))
            blk, _ = demote(keysn_ref[...], seen)
            keysn_ref[...] = blk

        tp_ref[...] = jnp.broadcast_to(jnp.maximum(T, IMIN + 1), (R, LANES))


def _dsa_s_index_call(page_table, z, cache_kidx, layer, DB, NP, G, topk):
    NS = NP // G
    GW = G * PAGE_SIZE
    kern = functools.partial(_dsa_s_index_kernel, NS=NS, G=G, topk=topk)
    QW = IDX_HEADS * IDX_DIM
    grid_spec = pltpu.PrefetchScalarGridSpec(
        num_scalar_prefetch=1,
        grid=(DB, NS),
        in_specs=[pl.BlockSpec((SROWS, QW), lambda b, p, pt: (b, Z_IQ // QW)),
                  pl.BlockSpec((SROWS, LANES), lambda b, p, pt: (b, Z_SM // LANES))]
                 + [pl.BlockSpec((None, None, IDX_DIM, PAGE_SIZE), _page_map(layer, NS, G, g)) for g in range(G)],
        out_specs=[pl.BlockSpec((None, NS, SROWS, GW), lambda b, p, pt: (b, 0, 0, 0)),
                   pl.BlockSpec((None, SROWS, LANES), lambda b, p, pt: (b, 0, 0)),
                   pl.BlockSpec((None, SROWS, LANES), lambda b, p, pt: (b, 0, 0))],
    )
    return pl.pallas_call(
        kern,
        grid_spec=grid_spec,
        out_shape=[jax.ShapeDtypeStruct((DB, NS, SROWS, GW), jnp.int32),
                   jax.ShapeDtypeStruct((DB, SROWS, LANES), jnp.int32),
                   jax.ShapeDtypeStruct((DB, SROWS, LANES), jnp.int32)],
        compiler_params=_cparams(("arbitrary", "arbitrary")),
        name="dsa_s_index",
    )(page_table.reshape(-1), z, z, *([cache_kidx] * G))


def _dsa_s_attend_kernel(pt_ref, rbr_ref, q_ref, keysp_ref, keysn_ref, tp_ref, *rest, NS, G, past):
    kp_refs, vp_refs = rest[:G], rest[G:2 * G]
    kn_ref, vn_ref, o_ref, m_scr, l_scr, acc_scr = rest[2 * G:]
    p = pl.program_id(1)
    R, H = SROWS, ATT_HEADS
    HR, PW = H * R, PAGE_SIZE * H
    rowi = lax.broadcasted_iota(jnp.int32, (HR, PW), 0)
    coli = lax.broadcasted_iota(jnp.int32, (HR, PW), 1)
    headmask = _imod(coli, H) == _idiv(rowi, R)
    qpos = past + _imod(rowi, R) - S_LO
    kin = _idiv(coli, H)
    expand = jnp.where(_idiv(lax.broadcasted_iota(jnp.int32, (PAGE_SIZE, PW), 1), H)
                       == lax.broadcasted_iota(jnp.int32, (PAGE_SIZE, PW), 0), 1.0, 0.0).astype(BF16)
    qa = _stack_heads(q_ref[...], H, ATT_DH).astype(BF16)
    Tp = tp_ref[...]
    scale = ATT_DH ** -0.5

    @pl.when(p == 0)
    def _():
        m_scr[...] = jnp.full(m_scr.shape, NEG, F32)
        l_scr[...] = jnp.zeros_like(l_scr)
        acc_scr[...] = jnp.zeros_like(acc_scr)

    def process(pages, near):
        lgs = []
        for ktile, kbase, xk_ref, _ in pages:
            s = lax.dot_general(qa, xk_ref[...].astype(BF16), _DIMS['nt'], preferred_element_type=F32)
            sel = jnp.dot(jnp.where(ktile >= Tp, 1.0, 0.0).astype(BF16), expand, preferred_element_type=F32)
            ok = (jnp.concatenate([sel] * H, axis=0) > 0.5) & headmask
            if near:
                bk = _t5_bucket(jnp.maximum(qpos - (kbase + kin), 0))
                bias = jnp.zeros((HR, PW), F32)
                for jb in range(N_BUCKETS):
                    bias = jnp.where(bk == jb, rbr_ref[:, jb:jb + 1], bias)
            else:
                bias = rbr_ref[:, N_BUCKETS - 1:N_BUCKETS]
            lgs.append(jnp.where(ok, s * scale + bias, NEG))
        mx = lgs[0]
        for lg in lgs[1:]:
            mx = jnp.maximum(mx, lg)
        m_old = m_scr[...]
        m_new = jnp.maximum(m_old, jnp.max(mx, axis=1, keepdims=True))
        corr = jnp.exp(m_old - m_new)
        tot, pv = None, None
        for lg, (_, _, _, xv_ref) in zip(lgs, pages):
            pr = jnp.exp(lg - m_new)
            d = jnp.dot(pr.astype(BF16), xv_ref[...].astype(BF16), preferred_element_type=F32)
            tot = pr if tot is None else tot + pr
            pv = d if pv is None else pv + d
        l_scr[...] = l_scr[...] * corr + jnp.sum(tot, axis=1, keepdims=True)
        acc_scr[...] = acc_scr[...] * corr + pv
        m_scr[...] = m_new

    def cache_pages():
        kt = keysp_ref[...]
        return [(kt[:, g * PAGE_SIZE:(g + 1) * PAGE_SIZE], (p * G + g) * PAGE_SIZE, kp_refs[g], vp_refs[g])
                for g in range(G)]

    @pl.when(p < NS - 1)
    def _():
        process(cache_pages(), False)

    @pl.when(p == NS - 1)
    def _():
        process(cache_pages(), True)
        process([(keysn_ref[...], past - S_LO, kn_ref, vn_ref)], True)
        inv = 1.0 / l_scr[...]
        for h in range(H):
            o_ref[:, h * ATT_DH:(h + 1) * ATT_DH] = acc_scr[h * R:(h + 1) * R, :] * inv[h * R:(h + 1) * R, :]


def _dsa_s_attend_call(page_table, rbrows, z, keysp, keysn, tp, cache_k, cache_v, knew, vnew, layer, DB, NP, G, past):
    NS = NP // G
    GW = G * PAGE_SIZE
    W = ATT_HEADS * ATT_DH
    PW = PAGE_SIZE * ATT_HEADS
    assert G * PAGE_SIZE >= MAX_DISTANCE
    kern = functools.partial(_dsa_s_attend_kernel, NS=NS, G=G, past=past)
    page_specs = [pl.BlockSpec((None, None, PW, ATT_DH), _page_map(layer, NS, G, g)) for g in range(G)]
    grid_spec = pltpu.PrefetchScalarGridSpec(
        num_scalar_prefetch=1,
        grid=(DB, NS),
        in_specs=[pl.BlockSpec((ATT_HEADS * SROWS, LANES), lambda b, p, pt: (0, 0)),
                  pl.BlockSpec((SROWS, W), lambda b, p, pt: (b, Z_AQ // W)),
                  pl.BlockSpec((None, None, SROWS, GW), lambda b, p, pt: (b, p, 0, 0)),
                  pl.BlockSpec((None, SROWS, LANES), lambda b, p, pt: (b, 0, 0)),
                  pl.BlockSpec((None, SROWS, LANES), lambda b, p, pt: (b, 0, 0))]
                 + page_specs + page_specs
                 + [pl.BlockSpec((None, PW, ATT_DH), lambda b, p, pt: (b, 0, 0)),
                    pl.BlockSpec((None, PW, ATT_DH), lambda b, p, pt: (b, 0, 0))],
        out_specs=pl.BlockSpec((SROWS, W), lambda b, p, pt: (b, 0)),
        scratch_shapes=[pltpu.VMEM((ATT_HEADS * SROWS, 1), F32),
                        pltpu.VMEM((ATT_HEADS * SROWS, 1), F32),
                        pltpu.VMEM((ATT_HEADS * SROWS, ATT_DH), F32)],
    )
    return pl.pallas_call(
        kern,
        grid_spec=grid_spec,
        out_shape=jax.ShapeDtypeStruct((DB * SROWS, W), F32),
        compiler_params=_cparams(("arbitrary", "arbitrary")),
        name="dsa_s_attend",
    )(page_table.reshape(-1), rbrows, z, keysp, keysn, tp, *([cache_k] * G), *([cache_v] * G), knew, vnew)


def _merge_kernel(oa_ref, ob_ref, oc_ref, g0_ref, g1_ref, g2_ref, x_ref, wb_ref, wo_ref, h_ref):
    acc = None
    for i, (o_ref, g_ref) in enumerate(((oa_ref, g0_ref), (ob_ref, g1_ref), (oc_ref, g2_ref))):
        br = jnp.dot(o_ref[...].astype(BF16), wb_ref[i], preferred_element_type=F32)
        term = _sigmoid(g_ref[...]) * br
        acc = term if acc is None else acc + term
    h_ref[...] = x_ref[...] + jnp.dot(acc.astype(BF16), wo_ref[...], preferred_element_type=F32)


def _merge_call(oa, ob, oc, z, x, wb, wo, layer, tm):
    M = x.shape[0]
    W = BRANCH_W
    g0 = Z_GATE // D_MODEL
    row = lambda c: (lambda i: (i, c))
    return pl.pallas_call(
        _merge_kernel,
        grid=(M // tm,),
        in_specs=[pl.BlockSpec((tm, W), row(0)), pl.BlockSpec((tm, W), row(0)), pl.BlockSpec((tm, W), row(0)),
                  pl.BlockSpec((tm, D_MODEL), row(g0)), pl.BlockSpec((tm, D_MODEL), row(g0 + 1)),
                  pl.BlockSpec((tm, D_MODEL), row(g0 + 2)),
                  pl.BlockSpec((tm, D_MODEL), row(0)),
                  pl.BlockSpec((None, N_BRANCH, W, D_MODEL), lambda i: (layer, 0, 0, 0)),
                  pl.BlockSpec((None, D_MODEL, D_MODEL), lambda i: (layer, 0, 0))],
        out_specs=pl.BlockSpec((tm, D_MODEL), row(0)),
        out_shape=jax.ShapeDtypeStruct((M, D_MODEL), F32),
        compiler_params=_cparams(("parallel",)),
        name="merge",
    )(oa, ob, oc, z, z, z, x, wb, wo)


def _ffn_down_kernel(a_ref, h_ref, cw_ref, wd_ref, gf_ref, y_ref, prev_scr, act_scr, *, final_norm):
    t = pl.program_id(1)
    tm = a_ref.shape[0]
    FH = D_FF // 2

    @pl.when(t == 0)
    def _():
        prev_scr[...] = jnp.zeros_like(prev_scr)

    acc = h_ref[...]
    for c in range(2):
        gs = slice(c * FH, (c + 1) * FH)
        vs = slice(D_FF + c * FH, D_FF + (c + 1) * FH)
        yg, yg0 = _conv_tile(a_ref[:, gs], prev_scr[:, gs], cw_ref[:, gs], FFN_CONV)
        yv, yv0 = _conv_tile(a_ref[:, vs], prev_scr[:, vs], cw_ref[:, vs], FFN_CONV)
        act_scr[...] = (_silu(yg) * yv).astype(BF16)
        act_scr[0:2 * SUBLANES, :] = jnp.concatenate(
            [_silu(yg0) * yv0, _silu(yg[SUBLANES:2 * SUBLANES]) * yv[SUBLANES:2 * SUBLANES]], axis=0).astype(BF16)
        acc = acc + jnp.dot(act_scr[...], wd_ref[gs, :], preferred_element_type=F32)
    prev_scr[...] = a_ref[tm - SUBLANES:tm, :]
    if final_norm:
        acc = acc * lax.rsqrt(jnp.mean(acc * acc, axis=-1, keepdims=True) + EPS) * gf_ref[...]
    y_ref[...] = acc


def _ffn_down_call(a, h, conv_w, wd, gf, layer, n_outer, tm, final_norm):
    M = h.shape[0]
    nt = M // (n_outer * tm)
    kern = functools.partial(_ffn_down_kernel, final_norm=final_norm)
    return pl.pallas_call(
        kern,
        grid=(n_outer, nt),
        in_specs=[pl.BlockSpec((tm, 2 * D_FF), lambda b, t: (b * nt + t, 0)),
                  pl.BlockSpec((tm, D_MODEL), lambda b, t: (b * nt + t, 0)),
                  pl.BlockSpec((None, FFN_CONV, 2 * D_FF), lambda b, t: (layer, 0, 0)),
                  pl.BlockSpec((None, D_FF, D_MODEL), lambda b, t: (layer, 0, 0)),
                  pl.BlockSpec((1, D_MODEL), lambda b, t: (0, 0))],
        out_specs=pl.BlockSpec((tm, D_MODEL), lambda b, t: (b * nt + t, 0)),
        out_shape=jax.ShapeDtypeStruct((M, D_MODEL), F32),
        scratch_shapes=[pltpu.VMEM((SUBLANES, 2 * D_FF), F32),
                        pltpu.VMEM((tm, D_FF // 2), BF16)],
        compiler_params=_cparams(("arbitrary", "arbitrary")),
        name="ffn_down",
    )(a, h, conv_w, wd, gf.reshape(1, D_MODEL))


def _rope_tables(pos):
    half = RET_DK // 2
    inv = 1.0 / (ROPE_BASE ** jnp.linspace(0.0, 1.0, half, dtype=F32))
    ang = pos.astype(F32)[:, None] * inv
    cos, sin = jnp.cos(ang), jnp.sin(ang)
    return jnp.concatenate([cos, cos], axis=-1), jnp.concatenate([-sin, sin], axis=-1)


def _dn_params(a_log, dt_bias):
    hp = jnp.zeros((SUBLANES, LANES), F32)
    hp = hp.at[0, SM_DNA:SM_DNA + DN_HEADS].set(a_log.astype(F32))
    hp = hp.at[1, SM_DNA:SM_DNA + DN_HEADS].set(dt_bias.astype(F32))
    return hp


def _mix_and_ffn(x, z, oa, ob, oc, sw, l, final, n_outer, tm_merge, tm_up, tn_up, tm_down, ffn_state=None):
    h = _merge_call(oa, ob, oc, z, x, sw['wb'], sw['wo'], l, tm_merge)
    a = _rms_matmul(h, sw['norm_ffn'], sw['w_up'], l, tm_up, tn_up)
    a_raw = a
    if ffn_state is not None:
        DB = ffn_state.shape[0]
        a = a.reshape(DB, SROWS, 2 * D_FF).at[:, S_LO - (FFN_CONV - 1):S_LO].set(ffn_state)
        a = a.reshape(DB * SROWS, 2 * D_FF)
    y = _ffn_down_call(a, h, sw['ffn_conv_w'], sw['wd'], sw['norm_final'], l, n_outer, tm_down, final)
    return y, a_raw


def kernel(x_prompt, x_sample, cache_k, cache_v, cache_kidx, state_dn_conv, state_dn, state_ret,
           state_ffn_conv, page_table, norm_mix, w_in, dn_conv_w, dn_a_log, dn_dt_bias, dn_norm,
           ret_norm, rel_bias, w_branch, w_o, norm_ffn, w_up, ffn_conv_w, w_down, norm_final):
    B, S, D = x_prompt.shape
    DB, DS, _ = x_sample.shape
    depth = w_in.shape[0]
    NP = page_table.shape[1]
    past = NP * PAGE_SIZE
    n_phys = cache_k.shape[1]
    W = ATT_HEADS * ATT_DH
    assert DS == S_HI - S_LO and S % CHUNK == 0 and (DB * SROWS) % CHUNK == 0

    TL = 256 if S % 256 == 0 else CHUNK
    TQ = 256 if S % 256 == 0 else CHUNK
    tm_p = 512 if (B * S) % 512 == 0 else CHUNK
    tm_mm = 1024 if (B * S) % 1024 == 0 else tm_p
    tm_d = 256 if S % 256 == 0 else CHUNK
    MS = DB * SROWS
    NG = MS // CHUNK
    seg_per = CHUNK // SROWS

    xp = x_prompt.reshape(B * S, D)
    xs = jnp.zeros((DB, SROWS, D), F32).at[:, S_LO:S_HI].set(x_sample).reshape(MS, D)
    cos_p, sin_p = _rope_tables(jnp.arange(S))
    pos_s = past + (jnp.arange(CHUNK) % SROWS) - S_LO
    cos_s, sin_s = _rope_tables(pos_s)
    ck = cache_k.reshape(depth, n_phys, PAGE_SIZE * ATT_HEADS, ATT_DH)
    cv = cache_v.reshape(depth, n_phys, PAGE_SIZE * ATT_HEADS, ATT_DH)
    ckidx_t = jnp.swapaxes(cache_kidx, 2, 3)
    rb = rel_bias.astype(F32)
    rbrows = jnp.pad(jnp.repeat(rb.T, SROWS, axis=0), ((0, 0), (0, LANES - N_BUCKETS)))
    G = next(g for g in (8, 4, 2, 1) if NP % g == 0)
    zeros_p = jnp.zeros((B, DN_HEADS, DN_DK, DN_DV), F32)
    topk_p = min(TOPK_MAX, S // 4)
    topk_s = min(TOPK_MAX, (past + DS) // 4)

    sw = dict(w_in=_prep_w_in(w_in), wb=w_branch.astype(BF16), wo=w_o.astype(BF16), w_up=w_up.astype(BF16),
              wd=w_down.astype(BF16), norm_ffn=norm_ffn, ffn_conv_w=ffn_conv_w, norm_final=norm_final)

    p_states, s_states = [], []
    for l in range(depth):
        lw = dict(dn_conv_w=dn_conv_w[l], hp=_dn_params(dn_a_log[l], dn_dt_bias[l]),
                  dn_norm=dn_norm[l].reshape(1, DN_DV), ret_norm=ret_norm[l])
        final = l == depth - 1

        z = _rms_matmul(xp, norm_mix, sw['w_in'], l, tm_mm, 1024)
        oa, dn_s = _dn_call(z, lw['dn_conv_w'], lw['hp'], lw['dn_norm'], zeros_p, B, TL, CHUNK, 0, CHUNK)
        ob, ret_s = _ret_call(z, cos_p, sin_p, lw['ret_norm'], zeros_p, B, TL, CHUNK, 0, CHUNK)
        k_c = z[:, Z_AK:Z_AK + W]
        v_c = z[:, Z_AV:Z_AV + W]
        vt = jnp.swapaxes(v_c.astype(BF16).reshape(B, S // TQ, TQ, W), 2, 3)
        oc = _dsa_prompt_call(rb, z, k_c.astype(BF16), vt, B, S, TQ, topk_p)
        xp, a_up = _mix_and_ffn(xp, z, oa, ob, oc, sw, l, final, B, tm_p, tm_mm, 1408, tm_d)
        z3 = z.reshape(B, S, Z_COLS)
        p_states.append((z3[:, S - (DN_CONV - 1):, Z_DNQKV:Z_DNQKV + DN_QKV], dn_s, ret_s,
                         k_c.reshape(B, S, ATT_HEADS, ATT_DH), v_c.reshape(B, S, ATT_HEADS, ATT_DH),
                         z3[:, :, Z_SM + SM_IK:Z_SM + SM_IK + IDX_DIM],
                         a_up.reshape(B, S, 2 * D_FF)[:, S - (FFN_CONV - 1):]))

        zs = _rms_matmul(xs, norm_mix, sw['w_in'], l, MS, 1024)
        zs3 = zs.reshape(DB, SROWS, Z_COLS)
        zs_conv = zs3.at[:, S_LO - (DN_CONV - 1):S_LO, Z_DNQKV:Z_DNQKV + DN_QKV].set(state_dn_conv[l])
        zs_conv = zs_conv.reshape(MS, Z_COLS)
        oa, dn_s = _dn_call(zs_conv, lw['dn_conv_w'], lw['hp'], lw['dn_norm'], state_dn[l], NG, CHUNK,
                            SROWS, S_LO, S_HI)
        ob, ret_s = _ret_call(zs, cos_s, sin_s, lw['ret_norm'], state_ret[l], NG, CHUNK, SROWS, S_LO, S_HI)
        keysp, keysn, tp = _dsa_s_index_call(page_table, zs, ckidx_t, l, DB, NP, G, topk_s)
        new_rows = lambda c0: jnp.pad(zs[:, c0:c0 + W].reshape(DB, SROWS * ATT_HEADS, ATT_DH),
                                      ((0, 0), (0, (PAGE_SIZE - SROWS) * ATT_HEADS), (0, 0)))
        oc = _dsa_s_attend_call(page_table, rbrows, zs, keysp, keysn, tp, ck, cv, new_rows(Z_AK), new_rows(Z_AV),
                                l, DB, NP, G, past)
        xs, a_up = _mix_and_ffn(xs, zs, oa, ob, oc, sw, l, final, 1, MS, MS, 1408, MS,
                                ffn_state=state_ffn_conv[l])
        tok = zs3[:, S_LO:S_HI]
        s_states.append((tok[:, DS - (DN_CONV - 1):, Z_DNQKV:Z_DNQKV + DN_QKV], dn_s, ret_s,
                         tok[:, :, Z_AK:Z_AK + W].reshape(DB, DS, ATT_HEADS, ATT_DH),
                         tok[:, :, Z_AV:Z_AV + W].reshape(DB, DS, ATT_HEADS, ATT_DH),
                         tok[:, :, Z_SM + SM_IK:Z_SM + SM_IK + IDX_DIM],
                         a_up.reshape(DB, SROWS, 2 * D_FF)[:, S_HI - (FFN_CONV - 1):S_HI]))

    y_prompt = xp.reshape(B, S, D)
    y_sample = xs.reshape(DB, SROWS, D)[:, S_LO:S_HI]
    stk = lambda states, i: jnp.stack([st[i] for st in states])
    return (y_prompt, y_sample) + tuple(stk(p_states, i) for i in range(7)) + tuple(stk(s_states, i) for i in range(7))
```

```python
import functools
import math

import numpy as np
import jax
import jax.numpy as jnp
from jax import lax
from jax.experimental import pallas as pl
from jax.experimental.pallas import tpu as pltpu

D_MODEL = 1024
DEPTH = 2
PAST_LEN = 8192
PAGE_SIZE = 128
DN_HEADS = 4
DN_DK = 128
DN_DV = 128
DN_CONV = 4
DN_QKV = 2 * DN_HEADS * DN_DK + DN_HEADS * DN_DV
RET_HEADS = 4
RET_DK = 128
RET_DV = 128
ROPE_BASE = 10000.0
ATT_HEADS = 4
ATT_DH = 128
IDX_HEADS = 4
IDX_DIM = 64
TOPK_MAX = 256
N_BUCKETS = 32
MAX_DISTANCE = 128
N_BRANCH = 3
BRANCH_W = DN_HEADS * DN_DV
D_FF = 2816
FFN_CONV = 3
EPS = 1e-6
F32 = jnp.float32
BF16 = jnp.bfloat16
IN_SIZES = (DN_QKV, DN_HEADS * DN_DV, DN_HEADS, DN_HEADS,
            RET_HEADS * RET_DK, RET_HEADS * RET_DK, RET_HEADS * RET_DV, RET_HEADS * RET_DV,
            ATT_HEADS * ATT_DH, ATT_HEADS * ATT_DH, ATT_HEADS * ATT_DH,
            IDX_HEADS * IDX_DIM, IDX_DIM, IDX_HEADS, N_BRANCH * D_MODEL)

Z_DNQKV = 0
Z_DNZ = 1536
Z_GATE = 2048
Z_RQ, Z_RK, Z_RV, Z_RG = 5120, 5632, 6144, 6656
Z_AQ, Z_AK, Z_AV = 7168, 7680, 8192
Z_IQ = 8704
Z_SM = 8960
SM_IK, SM_IW, SM_DNB, SM_DNA = 0, 64, 68, 72
Z_COLS = 9216

LANES = 128
SUBLANES = 8
CHUNK = 128
SROWS = 8
S_LO, S_HI = 3, 7
NEG = -1e30
IMIN = -2 ** 31
HALF = 2 ** 15
PACK16 = 2 * SUBLANES
VMEM_LIMIT = 56 * 1024 * 1024


def _cparams(sem):
    return pltpu.CompilerParams(dimension_semantics=sem, vmem_limit_bytes=VMEM_LIMIT)


def _sigmoid(x):
    return 1.0 / (1.0 + jnp.exp(-x))


def _silu(x):
    return x * _sigmoid(x)


def _softplus(x):
    return jnp.maximum(x, 0.0) + jnp.log(1.0 + jnp.exp(-jnp.abs(x)))


_DIMS = {'nn': (((1,), (0,)), ((), ())), 'nt': (((1,), (1,)), ((), ())), 'tn': (((0,), (0,)), ((), ()))}


def _split_bf16(a, n):
    parts = []
    r = a
    for i in range(n):
        p = r.astype(BF16)
        parts.append(p)
        if i + 1 < n:
            r = r - p.astype(F32)
    return parts


def _mm(a, b, dims='nn', mode='bf16'):
    dn = _DIMS[dims]
    dg = lambda x, y: lax.dot_general(x, y, dn, preferred_element_type=F32)
    if mode == 'bf16':
        return dg(a.astype(BF16), b.astype(BF16))
    if mode == 'x3':
        ah, al = _split_bf16(a, 2)
        bh, bl = _split_bf16(b, 2)
        return dg(ah, bh) + dg(ah, bl) + dg(al, bh)
    if mode == 'l01':
        ab = a.astype(BF16)
        b1, b2, b3 = _split_bf16(b, 3)
        return dg(ab, b1) + dg(ab, b2) + dg(ab, b3)
    raise ValueError(mode)


def _rms_mm_kernel(x_ref, g_ref, w_ref, o_ref, u_ref):
    @pl.when(pl.program_id(1) == 0)
    def _():
        x = x_ref[...]
        r = lax.rsqrt(jnp.mean(x * x, axis=-1, keepdims=True) + EPS)
        u_ref[...] = (x * r * g_ref[...]).astype(u_ref.dtype)

    o_ref[...] = jnp.dot(u_ref[...], w_ref[...], preferred_element_type=F32)


def _rms_matmul(x, g, w, layer, tm, tn):
    M, K = x.shape
    N = w.shape[2]
    return pl.pallas_call(
        _rms_mm_kernel,
        grid=(M // tm, N // tn),
        in_specs=[pl.BlockSpec((tm, K), lambda i, j: (i, 0)),
                  pl.BlockSpec((None, 1, K), lambda i, j: (layer, 0, 0)),
                  pl.BlockSpec((None, K, tn), lambda i, j: (layer, 0, j))],
        out_specs=pl.BlockSpec((tm, tn), lambda i, j: (i, j)),
        out_shape=jax.ShapeDtypeStruct((M, N), F32),
        scratch_shapes=[pltpu.VMEM((tm, K), BF16)],
        compiler_params=_cparams(("parallel", "arbitrary")),
        name="rms_matmul",
    )(x, g.reshape(g.shape[0], 1, K), w)


_IN_OFFS = [0] + np.cumsum(np.array(IN_SIZES)).tolist()
_SRC_DNB, _SRC_RQ, _SRC_IK, _SRC_IW, _SRC_GATE, IN_COLS = (_IN_OFFS[2], _IN_OFFS[4], _IN_OFFS[12], _IN_OFFS[13],
                                                           _IN_OFFS[14], _IN_OFFS[15])
IN_COLS_PAD = -(-IN_COLS // LANES) * LANES


def _prep_w_in_kernel(w_ref, o_ref):
    def shifted(src, width):
        a = src // LANES * LANES
        win = -(-(src - a + width) // LANES) * LANES
        return pltpu.roll(w_ref[:, a:a + win], win - (src - a), axis=1)[:, 0:width]

    n_head = Z_GATE
    o_ref[:, 0:n_head] = w_ref[:, 0:n_head].astype(BF16)
    o_ref[:, Z_GATE:Z_GATE + N_BRANCH * D_MODEL] = shifted(_SRC_GATE, N_BRANCH * D_MODEL).astype(BF16)
    o_ref[:, Z_RQ:Z_SM] = shifted(_SRC_RQ, Z_SM - Z_RQ).astype(BF16)
    lane = lax.broadcasted_iota(jnp.int32, (w_ref.shape[0], LANES), 1)
    n_idx = IDX_DIM + IDX_HEADS
    a_ik = _SRC_IK // LANES * LANES
    idx_part = pltpu.roll(w_ref[:, a_ik:a_ik + LANES], LANES - (_SRC_IK - a_ik), axis=1)
    dn_part = pltpu.roll(w_ref[:, _SRC_DNB:_SRC_DNB + LANES], SM_DNB, axis=1)
    small = jnp.where(lane < n_idx, idx_part, jnp.where(lane < n_idx + 2 * DN_HEADS, dn_part, 0.0))
    o_ref[:, Z_SM:Z_SM + LANES] = small.astype(BF16)
    o_ref[:, Z_SM + LANES:Z_COLS] = jnp.zeros((w_ref.shape[0], Z_COLS - Z_SM - LANES), BF16)


def _prep_w_in(w_in):
    depth, K, _ = w_in.shape
    assert _SRC_DNB % LANES == 0 and _IN_OFFS[1] == Z_DNZ and _SRC_DNB == Z_GATE
    assert _SRC_IW - _SRC_IK == IDX_DIM and _SRC_GATE - _SRC_IW == IDX_HEADS
    assert (_SRC_IK % LANES) + IDX_DIM + IDX_HEADS <= LANES and SM_DNB == IDX_DIM + IDX_HEADS
    assert Z_SM - Z_RQ == _SRC_IK - _SRC_RQ and SM_DNA == SM_DNB + DN_HEADS
    tr = 256
    wp = jnp.pad(w_in, ((0, 0), (0, 0), (0, IN_COLS_PAD - IN_COLS)))
    return pl.pallas_call(
        _prep_w_in_kernel,
        grid=(depth, K // tr),
        in_specs=[pl.BlockSpec((None, tr, IN_COLS_PAD), lambda l, i: (l, i, 0))],
        out_specs=pl.BlockSpec((None, tr, Z_COLS), lambda l, i: (l, i, 0)),
        out_shape=jax.ShapeDtypeStruct((depth, K, Z_COLS), BF16),
        compiler_params=_cparams(("parallel", "parallel")),
        name="prep_w_in",
    )(wp)


def _conv_tile(x, prev8, w, width):
    y = x * w[width - 1:width, :]
    for s in range(1, width):
        y = y + pltpu.roll(x, s, axis=0) * w[width - 1 - s:width - s, :]
    x0 = x[0:SUBLANES, :]
    rid = lax.broadcasted_iota(jnp.int32, x0.shape, 0)
    y0 = x0 * w[width - 1:width, :]
    for s in range(1, width):
        xs = jnp.where(rid < s, pltpu.roll(prev8, s, axis=0), pltpu.roll(x0, s, axis=0))
        y0 = y0 + xs * w[width - 1 - s:width - s, :]
    return y, y0


def _idiv(x, n):
    assert n & (n - 1) == 0
    return lax.shift_right_arithmetic(x, jnp.int32(n.bit_length() - 1))


def _imod(x, n):
    assert n & (n - 1) == 0
    return x & jnp.int32(n - 1)


def _chunk_masks(C, seg):
    ri = lax.broadcasted_iota(jnp.int32, (C, C), 0)
    ci = lax.broadcasted_iota(jnp.int32, (C, C), 1)
    if seg == C:
        return ri >= ci, ri > ci, None
    same = _idiv(ri, seg) == _idiv(ci, seg)
    return (ri >= ci) & same, (ri > ci) & same, same


def _valid_col(C, seg, lo, hi):
    r = _imod(lax.broadcasted_iota(jnp.int32, (C, 1), 0), seg)
    return jnp.where((r >= lo) & (r < hi), 1.0, 0.0)


def _tri_inv(ms, span):
    C = ms[0].shape[0]
    eye = jnp.where(lax.broadcasted_iota(jnp.int32, (C, C), 0) == lax.broadcasted_iota(jnp.int32, (C, C), 1),
                    1.0, 0.0)
    invs = [eye - m for m in ms]
    ps = list(ms)
    n = 2
    while n < span:
        ps = [_mm(p, p, 'nn', 'x3') for p in ps]
        invs = [inv + _mm(inv, p, 'nn', 'x3') for inv, p in zip(invs, ps)]
        n *= 2
    return invs


def _state_update(S_scr, h, u, kcum, qd, kd, qk, gtot, C, seg):
    nseg = C // seg
    ws, o1s = [], []
    for sg in range(nseg):
        rs = slice(sg * seg, (sg + 1) * seg)
        S = S_scr[sg, h]
        if kcum is None:
            ws.append(u[rs])
        else:
            ws.append(u[rs] - _mm(kcum[rs], S))
        o1s.append(_mm(qd[rs], S))
    w = ws[0] if nseg == 1 else jnp.concatenate(ws, axis=0)
    o1 = o1s[0] if nseg == 1 else jnp.concatenate(o1s, axis=0)
    o = o1 + _mm(qk, w)
    rowid = lax.broadcasted_iota(jnp.int32, (C, 1), 0)
    for sg in range(nseg):
        kdm = kd if nseg == 1 else jnp.where(_idiv(rowid, seg) == sg, kd, 0.0)
        gt = jnp.exp(gtot[sg * seg:sg * seg + 1, :])
        S_scr[sg, h] = S_scr[sg, h] * gt + _mm(kdm, w, 'tn')
    return o


def _dn_kernel(qkv_ref, dz_ref, sm_ref, cw_ref, hp_ref, nrm_ref, s0_ref, o_ref, sfin_ref,
               S_scr, prev_scr, c_scr, *, C, seg, lo, hi):
    t = pl.program_id(1)
    TL = qkv_ref.shape[0]
    H, DK = DN_HEADS, DN_DK
    masked = seg != C

    @pl.when(t == 0)
    def _():
        S_scr[...] = s0_ref[...]
        prev_scr[...] = jnp.zeros_like(prev_scr)

    x = qkv_ref[...]
    y, y0 = _conv_tile(x, prev_scr[...], cw_ref[...], DN_CONV)
    c_scr[...] = _silu(y)
    c_scr[0:SUBLANES, :] = _silu(y0)
    prev_scr[...] = x[TL - SUBLANES:TL, :]

    lowm, strictm, same = _chunk_masks(C, seg)
    ltri = jnp.where(lowm, 1.0, 0.0)
    valid = _valid_col(C, seg, lo, hi) if masked else None
    span = (hi - lo) if masked else C
    a_coef = -jnp.exp(hp_ref[0:1, :])
    dtb = hp_ref[1:2, :]

    units = []
    for cidx in range(TL // C):
        r0 = cidx * C
        cc = c_scr[r0:r0 + C, :]
        sm = sm_ref[r0:r0 + C, :]
        g128 = a_coef * _softplus(sm + dtb)
        b128 = _sigmoid(sm)
        if masked:
            g128 = g128 * valid
            b128 = b128 * valid
        Gc128 = _mm(ltri, g128, 'nn', 'l01')
        if masked:
            Gt128 = _mm(jnp.where(same, 1.0, 0.0), g128, 'nn', 'l01')
        else:
            Gt128 = jnp.broadcast_to(Gc128[C - 1:C, :], Gc128.shape)
        GT = Gc128.T
        for h in range(H):
            q = cc[:, h * DK:(h + 1) * DK]
            k = cc[:, (H + h) * DK:(H + h + 1) * DK]
            v = cc[:, (2 * H + h) * DK:(2 * H + h + 1) * DK]
            q = q * lax.rsqrt(jnp.sum(q * q, axis=-1, keepdims=True) + EPS) * DK ** -0.5
            k = k * lax.rsqrt(jnp.sum(k * k, axis=-1, keepdims=True) + EPS)
            if masked:
                k = k * valid
            Gc = Gc128[:, SM_DNA + h:SM_DNA + h + 1]
            Gr = GT[SM_DNA + h:SM_DNA + h + 1, :]
            Gt = Gt128[:, SM_DNA + h:SM_DNA + h + 1]
            bc = b128[:, SM_DNB + h:SM_DNB + h + 1]
            decay = jnp.where(lowm, jnp.exp(jnp.where(lowm, Gc - Gr, 0.0)), 0.0)
            eG = jnp.exp(Gc)
            units.append(dict(r0=r0, h=h, Gt=Gt, m=jnp.where(strictm, _mm(k, k, 'nt') * decay * bc, 0.0),
                              qk=_mm(q, k, 'nt') * decay, rhs_u=v * bc, rhs_k=k * (bc * eG),
                              qd=q * eG, kd=k * jnp.exp(Gt - Gc)))
    ainvs = _tri_inv([un['m'] for un in units], span)
    for un, ainv in zip(units, ainvs):
        un['u'] = _mm(ainv, un['rhs_u'], 'nn', 'x3')
        un['kcum'] = _mm(ainv, un['rhs_k'], 'nn', 'x3')

    for un in units:
        r0, h = un['r0'], un['h']
        o = _state_update(S_scr, h, un['u'], un['kcum'], un['qd'], un['kd'], un['qk'], un['Gt'], C, seg)
        on = o * lax.rsqrt(jnp.mean(o * o, axis=-1, keepdims=True) + EPS) * nrm_ref[...]
        zg = dz_ref[r0:r0 + C, h * DN_DV:(h + 1) * DN_DV]
        o_ref[r0:r0 + C, h * DN_DV:(h + 1) * DN_DV] = on * _silu(zg)

    @pl.when(t == pl.num_programs(1) - 1)
    def _():
        sfin_ref[...] = S_scr[...]


def _dn_call(z, conv_w, hp, nrm, s0, n_outer, TL, seg, lo, hi):
    M = z.shape[0]
    nt = M // (n_outer * TL)
    nseg = CHUNK // seg
    rowmap = lambda cb: (lambda b, t: (b * nt + t, cb))
    kern = functools.partial(_dn_kernel, C=CHUNK, seg=seg, lo=lo, hi=hi)
    return pl.pallas_call(
        kern,
        grid=(n_outer, nt),
        in_specs=[pl.BlockSpec((TL, DN_QKV), rowmap(Z_DNQKV // DN_QKV)),
                  pl.BlockSpec((TL, BRANCH_W), rowmap(Z_DNZ // BRANCH_W)),
                  pl.BlockSpec((TL, LANES), rowmap(Z_SM // LANES)),
                  pl.BlockSpec((DN_CONV, DN_QKV), lambda b, t: (0, 0)),
                  pl.BlockSpec((SUBLANES, LANES), lambda b, t: (0, 0)),
                  pl.BlockSpec((1, DN_DV), lambda b, t: (0, 0)),
                  pl.BlockSpec((nseg, DN_HEADS, DN_DK, DN_DV), lambda b, t: (b, 0, 0, 0))],
        out_specs=[pl.BlockSpec((TL, BRANCH_W), lambda b, t: (b * nt + t, 0)),
                   pl.BlockSpec((nseg, DN_HEADS, DN_DK, DN_DV), lambda b, t: (b, 0, 0, 0))],
        out_shape=[jax.ShapeDtypeStruct((M, BRANCH_W), F32),
                   jax.ShapeDtypeStruct(s0.shape, F32)],
        scratch_shapes=[pltpu.VMEM((nseg, DN_HEADS, DN_DK, DN_DV), F32),
                        pltpu.VMEM((SUBLANES, DN_QKV), F32),
                        pltpu.VMEM((TL, DN_QKV), F32)],
        compiler_params=_cparams(("arbitrary", "arbitrary")),
        name="dn",
    )(z, z, z, conv_w, hp, nrm, s0)


_LOG_GAMMA = [float(np.log1p(-np.exp2(-5.0 - h))) for h in range(RET_HEADS)]


def _ret_kernel(q_ref, k_ref, v_ref, g_ref, cos_ref, sin_ref, nrm_ref, s0_ref, o_ref, sfin_ref,
                S_scr, *, C, seg, lo, hi):
    t = pl.program_id(1)
    TL = q_ref.shape[0]
    H, DK = RET_HEADS, RET_DK
    masked = seg != C

    @pl.when(t == 0)
    def _():
        S_scr[...] = s0_ref[...]

    lowm, _, _ = _chunk_masks(C, seg)
    ri = lax.broadcasted_iota(jnp.int32, (C, 1), 0)
    ci = lax.broadcasted_iota(jnp.int32, (1, C), 1)
    if masked:
        valid = _valid_col(C, seg, lo, hi)
        cnt_c = jnp.clip(_imod(ri, seg) - lo + 1, 0, hi - lo).astype(F32)
        cnt_r = jnp.clip(_imod(ci, seg) - lo + 1, 0, hi - lo).astype(F32)
        cnt_t = float(hi - lo)
    else:
        valid = None
        cnt_c = (ri + 1).astype(F32)
        cnt_r = (ci + 1).astype(F32)
        cnt_t = float(C)

    for cidx in range(TL // C):
        r0 = cidx * C
        cosf = cos_ref[r0:r0 + C, :]
        sins = sin_ref[r0:r0 + C, :]
        for h in range(H):
            cs = slice(h * DK, (h + 1) * DK)
            q = q_ref[r0:r0 + C, cs]
            k = k_ref[r0:r0 + C, cs]
            v = v_ref[r0:r0 + C, cs]
            q = (q * cosf + pltpu.roll(q, DK // 2, axis=1) * sins) * DK ** -0.5
            k = k * cosf + pltpu.roll(k, DK // 2, axis=1) * sins
            if masked:
                v = v * valid
            lg = _LOG_GAMMA[h]
            Gc = cnt_c * lg
            decay = jnp.where(lowm, jnp.exp(jnp.where(lowm, (cnt_c - cnt_r) * lg, 0.0)), 0.0)
            qk = _mm(q, k, 'nt') * decay
            Gt = jnp.full((C, 1), cnt_t * lg, F32)
            o = _state_update(S_scr, h, v, None, q * jnp.exp(Gc), k * jnp.exp(Gt - Gc), qk, Gt, C, seg)
            mu = jnp.mean(o, axis=-1, keepdims=True)
            oc = o - mu
            var = jnp.mean(oc * oc, axis=-1, keepdims=True)
            on = oc * lax.rsqrt(var + EPS) * nrm_ref[h:h + 1, :]
            o_ref[r0:r0 + C, cs] = on * _silu(g_ref[r0:r0 + C, cs])

    @pl.when(t == pl.num_programs(1) - 1)
    def _():
        sfin_ref[...] = S_scr[...]


def _ret_call(z, cosf, sins, nrm, s0, n_outer, TL, seg, lo, hi):
    M = z.shape[0]
    nt = M // (n_outer * TL)
    nseg = CHUNK // seg
    W = RET_HEADS * RET_DK
    rowmap = lambda cb: (lambda b, t: (b * nt + t, cb))
    kern = functools.partial(_ret_kernel, C=CHUNK, seg=seg, lo=lo, hi=hi)
    return pl.pallas_call(
        kern,
        grid=(n_outer, nt),
        in_specs=[pl.BlockSpec((TL, W), rowmap(Z_RQ // W)),
                  pl.BlockSpec((TL, W), rowmap(Z_RK // W)),
                  pl.BlockSpec((TL, W), rowmap(Z_RV // W)),
                  pl.BlockSpec((TL, W), rowmap(Z_RG // W)),
                  pl.BlockSpec((TL, RET_DK), lambda b, t: (t, 0)),
                  pl.BlockSpec((TL, RET_DK), lambda b, t: (t, 0)),
                  pl.BlockSpec((RET_HEADS, RET_DV), lambda b, t: (0, 0)),
                  pl.BlockSpec((nseg, RET_HEADS, RET_DK, RET_DV), lambda b, t: (b, 0, 0, 0))],
        out_specs=[pl.BlockSpec((TL, W), lambda b, t: (b * nt + t, 0)),
                   pl.BlockSpec((nseg, RET_HEADS, RET_DK, RET_DV), lambda b, t: (b, 0, 0, 0))],
        out_shape=[jax.ShapeDtypeStruct((M, W), F32),
                   jax.ShapeDtypeStruct(s0.shape, F32)],
        scratch_shapes=[pltpu.VMEM((nseg, RET_HEADS, RET_DK, RET_DV), F32)],
        compiler_params=_cparams(("arbitrary", "arbitrary")),
        name="ret",
    )(z, z, z, z, cosf, sins, nrm, s0)


def _f2key(x):
    b = lax.bitcast_convert_type(x + 0.0, jnp.int32)
    return jnp.where(b >= 0, b, b ^ jnp.int32(0x7FFFFFFF))


def _t5_bucket(d):
    exact = N_BUCKETS // 2
    df = d.astype(F32)
    large = exact + (jnp.log(jnp.maximum(df, 1.0) / exact) / math.log(MAX_DISTANCE / exact)
                     * (N_BUCKETS - exact)).astype(jnp.int32)
    large = jnp.minimum(large, N_BUCKETS - 1)
    return jnp.where(d < exact, d, large)


def _bias_from_dist(d, rb_ref, h):
    bk = _t5_bucket(d)
    r = jnp.zeros(d.shape, F32)
    for jb in range(N_BUCKETS):
        r = jnp.where(bk == jb, rb_ref[jb, h], r)
    return r


def _threshold_search(count_ge, shape, total, kf, nbits=32):
    zero = jnp.zeros(shape, jnp.int32)
    c0 = count_ge(zero)
    ok0 = c0 >= kf
    T = jnp.where(ok0, 0, -2 ** (nbits - 1)).astype(jnp.int32)
    cT = jnp.where(ok0, c0, total)

    def body(it, carry):
        T, cT = carry
        cand = T + lax.shift_left(jnp.int32(1), jnp.int32(nbits - 2) - it)
        c = count_ge(cand)
        ok = c >= kf
        return jnp.where(ok, cand, T), jnp.where(ok, c, cT)

    return lax.fori_loop(0, nbits - 1, body, (T, cT))


def _fold_lanes(x):
    f = x[:, 0:LANES]
    for u in range(1, x.shape[1] // LANES):
        f = f + x[:, u * LANES:(u + 1) * LANES]
    return f


def _fold_rows(x):
    return jnp.sum(x.reshape(x.shape[0] // SUBLANES, SUBLANES, x.shape[1]), axis=0)


def _dsa_prompt_kernel(rb_ref, q_ref, qi_ref, smq_ref, k_ref, vt_ref, smk_ref, o_ref,
                       keys_scr, hi_scr, lo_scr, lg_scr, *, TQ, topk):
    i = pl.program_id(1)
    KC = TQ
    nk = i + 1
    kf = float(topk)
    qi = qi_ref[...]
    wT = smq_ref[...].T
    kpos0 = lax.broadcasted_iota(jnp.int32, (KC, TQ), 0)
    qidx = lax.broadcasted_iota(jnp.int32, (KC, TQ), 1)
    lane = lax.broadcasted_iota(jnp.int32, (KC, LANES), 1)

    def q_operand(e):
        eh = e.astype(BF16)
        hf = eh.astype(F32)
        lf = (e - hf).astype(BF16).astype(F32)
        return jnp.concatenate([(hf + pltpu.roll(lf, IDX_DIM, axis=1)).astype(BF16), eh], axis=1)

    q_ops = []
    for h in range(IDX_HEADS):
        slab = qi[:, (h // 2) * LANES:(h // 2 + 1) * LANES]
        if h % 2 == 0:
            q_ops.append(q_operand(jnp.where(lane < IDX_DIM, slab, 0.0)))
        else:
            q_ops.append(q_operand(pltpu.roll(jnp.where(lane >= IDX_DIM, slab, 0.0), IDX_DIM, axis=1)))

    def p1(j, c):
        r0 = pl.multiple_of(j * KC, KC)
        k0 = jnp.where(lane < IDX_DIM, smk_ref[pl.ds(r0, KC), :], 0.0)
        kh2 = (k0 + pltpu.roll(k0, IDX_DIM, axis=1)).astype(BF16)
        kl = (k0 - k0.astype(BF16).astype(F32)).astype(BF16)
        k_op = jnp.concatenate([kh2, kl], axis=1)
        acc = jnp.zeros((KC, TQ), F32)
        for h in range(IDX_HEADS):
            s = lax.dot_general(k_op, q_ops[h], _DIMS['nt'], preferred_element_type=F32)
            acc = acc + jnp.maximum(s, 0.0) * wT[SM_IW + h:SM_IW + h + 1, :]
        key = _f2key(acc * (IDX_DIM ** -0.5 * IDX_HEADS ** -0.5))
        key = jnp.where(kpos0 + r0 <= qidx + i * TQ, key, IMIN)
        keys_scr[j] = key
        hi_scr[j] = lax.shift_right_arithmetic(key, 16).astype(jnp.int16)
        lo_scr[j] = ((key & 0xFFFF) - HALF).astype(jnp.int16)
        return c

    lax.fori_loop(0, nk, p1, 0)

    def count_ge(cand):
        def body(j, part):
            return part + _fold_rows(jnp.where(keys_scr[j] >= cand, 1.0, 0.0))
        part = lax.fori_loop(0, nk, body, jnp.zeros((SUBLANES, TQ), F32))
        return jnp.sum(part, axis=0, keepdims=True)

    def count_ge16(scr):
        def count(cand):
            c16 = cand.astype(jnp.int16)

            def body(j, part):
                ind = jnp.where(scr[j] >= c16, jnp.int16(1), jnp.int16(0))
                for r in range(KC // PACK16):
                    part = part + ind[r * PACK16:(r + 1) * PACK16, :]
                return part
            part = lax.fori_loop(0, nk, body, jnp.zeros((PACK16, TQ), jnp.int16))
            return jnp.sum(part.astype(F32), axis=0, keepdims=True)
        return count

    total = (nk * KC).astype(F32)
    T_hi, _ = _threshold_search(count_ge16(hi_scr), (1, TQ), total, kf, 16)
    c_gt = jnp.where(T_hi >= HALF - 1, 0.0, count_ge16(hi_scr)(jnp.minimum(T_hi + 1, HALF - 1)))
    t16 = T_hi.astype(jnp.int16)

    def keep_class(j, c):
        lo_scr[j] = jnp.where(hi_scr[j] == t16, lo_scr[j], jnp.int16(-HALF))
        return c

    lax.fori_loop(0, nk, keep_class, 0)
    T_lo, c_lo = _threshold_search(count_ge16(lo_scr), (1, TQ), total, kf - c_gt, 16)
    T = T_hi * (2 * HALF) + (T_lo + HALF)
    cT = c_gt + c_lo

    ties = jnp.max(jnp.where((cT > kf) & (T > IMIN), 1.0, 0.0)) > 0.0

    @pl.when(ties)
    def _():
        need = kf - count_ge(T + 1)
        tril = jnp.where(lax.broadcasted_iota(jnp.int32, (KC, KC), 0)
                         >= lax.broadcasted_iota(jnp.int32, (KC, KC), 1), 1.0, 0.0).astype(BF16)

        def body(j, seen):
            kj = keys_scr[j]
            eq = kj == T
            pre = jnp.dot(tril, jnp.where(eq, 1.0, 0.0).astype(BF16), preferred_element_type=F32)
            keys_scr[j] = jnp.where(eq & (seen + pre > need), IMIN, kj)
            return seen + pre[KC - 1:KC, :]

        lax.fori_loop(0, nk, body, jnp.zeros((1, TQ), F32))

    Tp = jnp.maximum(T, IMIN + 1)

    cidx = lax.broadcasted_iota(jnp.int32, (1, 2 * KC), 1)
    e = jnp.where(cidx < KC, cidx, cidx - 2 * KC)
    scale = ATT_DH ** -0.5
    jprev = jnp.maximum(i - 1, 0)

    def toeplitz(r):
        y = pltpu.roll(jnp.broadcast_to(r, (KC, 2 * KC)), 0, 1, stride=1, stride_axis=0)
        return y[:, 0:TQ]

    def to_mask(j, c):
        keys_scr[j] = lax.bitcast_convert_type(jnp.where(keys_scr[j] >= Tp, 0.0, NEG), jnp.int32)
        return c

    lax.fori_loop(0, nk, to_mask, 0)

    heads = range(ATT_HEADS)
    hcols = [slice(h * ATT_DH, (h + 1) * ATT_DH) for h in heads]
    qhs = [q_ref[:, cs].astype(BF16) for cs in hcols]

    def pass_a(j, biases, ms):
        r0 = pl.multiple_of(j * KC, KC)
        madd = lax.bitcast_convert_type(keys_scr[j], F32)
        out = []
        for h in heads:
            lg = lax.dot_general(k_ref[pl.ds(r0, KC), hcols[h]], qhs[h], _DIMS['nt'], preferred_element_type=F32)
            lg = lg * scale + biases[h] + madd
            lg_scr[h, j] = lg
            out.append(jnp.maximum(ms[h], jnp.max(lg, axis=0, keepdims=True)))
        return tuple(out)

    far_bias = [rb_ref[N_BUCKETS - 1, h] for h in heads]
    ms = lax.fori_loop(0, jprev, lambda j, ms: pass_a(j, far_bias, ms),
                       tuple(jnp.full((1, TQ), NEG, F32) for _ in heads))
    ms = pass_a(jprev, [toeplitz(_bias_from_dist(jnp.maximum(KC + e, 0), rb_ref, h)) for h in heads], ms)
    ms = pass_a(i, [toeplitz(_bias_from_dist(jnp.maximum(e, 0), rb_ref, h)) for h in heads], ms)

    def pass_b(j, carry):
        out = []
        for h in heads:
            l, acc = carry[h]
            p = jnp.exp(lg_scr[h, j] - ms[h])
            out.append((l + jnp.sum(p, axis=0, keepdims=True),
                        acc + jnp.dot(vt_ref[j, hcols[h], :], p.astype(BF16), preferred_element_type=F32)))
        return tuple(out)

    res = lax.fori_loop(0, nk, pass_b,
                        tuple((jnp.zeros((1, TQ), F32), jnp.zeros((ATT_DH, TQ), F32)) for _ in heads))
    for h in heads:
        l, acc = res[h]
        o_ref[:, hcols[h]] = (acc / l).T


def _dsa_prompt_call(rel_bias, z, kb, vt, B, L, TQ, topk):
    nq = L // TQ
    W = ATT_HEADS * ATT_DH
    kern = functools.partial(_dsa_prompt_kernel, TQ=TQ, topk=topk)
    return pl.pallas_call(
        kern,
        grid=(B, nq),
        in_specs=[pl.BlockSpec(memory_space=pltpu.SMEM),
                  pl.BlockSpec((TQ, W), lambda b, i: (b * nq + i, Z_AQ // W)),
                  pl.BlockSpec((TQ, IDX_HEADS * IDX_DIM), lambda b, i: (b * nq + i, Z_IQ // (IDX_HEADS * IDX_DIM))),
                  pl.BlockSpec((TQ, LANES), lambda b, i: (b * nq + i, Z_SM // LANES)),
                  pl.BlockSpec((L, W), lambda b, i: (b, 0)),
                  pl.BlockSpec((None, nq, W, TQ), lambda b, i: (b, 0, 0, 0)),
                  pl.BlockSpec((L, LANES), lambda b, i: (b, Z_SM // LANES))],
        out_specs=pl.BlockSpec((TQ, W), lambda b, i: (b * nq + i, 0)),
        out_shape=jax.ShapeDtypeStruct((B * L, W), F32),
        scratch_shapes=[pltpu.VMEM((nq, TQ, TQ), jnp.int32), pltpu.VMEM((nq, TQ, TQ), jnp.int16),
                        pltpu.VMEM((nq, TQ, TQ), jnp.int16), pltpu.VMEM((ATT_HEADS, nq, TQ, TQ), F32)],
        compiler_params=_cparams(("arbitrary", "arbitrary")),
        name="dsa_prompt",
    )(rel_bias, z, z, z, kb, vt, z)


def _stack_heads(x, nh, w):
    return jnp.concatenate([x[:, h * w:(h + 1) * w] for h in range(nh)], axis=0)


def _page_map(layer, NS, G, g):
    return lambda b, p, pt: (layer, pt[(b * NS + p) * G + g], 0, 0)


def _dsa_s_index_kernel(pt_ref, qi_ref, sm_ref, *rest, NS, G, topk):
    kp_refs = rest[:G]
    keysp_ref, keysn_ref, tp_ref = rest[G:]
    p = pl.program_id(1)
    kf = float(topk)
    R = SROWS
    qs = _stack_heads(qi_ref[...], IDX_HEADS, IDX_DIM)
    wcol = _stack_heads(sm_ref[:, SM_IW:SM_IW + IDX_HEADS], IDX_HEADS, 1)

    def score_keys(kmat, dims):
        s = _mm(qs, kmat, dims, 'x3')
        t = jnp.maximum(s, 0.0) * wcol
        acc = t[0:R]
        for h in range(1, IDX_HEADS):
            acc = acc + t[h * R:(h + 1) * R]
        return _f2key(acc * (IDX_DIM ** -0.5 * IDX_HEADS ** -0.5))

    keysp_ref[p] = score_keys(jnp.concatenate([r[...] for r in kp_refs], axis=1), 'nn')

    @pl.when(p == NS - 1)
    def _():
        rowi = lax.broadcasted_iota(jnp.int32, (R, LANES), 0)
        coli = lax.broadcasted_iota(jnp.int32, (R, LANES), 1)
        knew = jnp.concatenate([sm_ref[:, SM_IK:SM_IK + IDX_DIM],
                                jnp.zeros((PAGE_SIZE - R, IDX_DIM), F32)], axis=0)
        ok = (coli >= S_LO) & (coli < S_HI) & (coli <= rowi)
        keysn_ref[...] = jnp.where(ok, score_keys(knew, 'nt'), IMIN)

        def count_ge(cand):
            a = jnp.sum(jnp.where(keysp_ref[...] >= cand[None], 1.0, 0.0), axis=0)
            b = jnp.where(keysn_ref[...] >= cand, 1.0, 0.0)
            return jnp.sum(_fold_lanes(a) + b, axis=1, keepdims=True)

        total = jnp.full((R, 1), float((NS * G + 1) * PAGE_SIZE), F32)
        T, cT = _threshold_search(count_ge, (R, 1), total, kf)
        rid = lax.broadcasted_iota(jnp.int32, (R, 1), 0)
        token_row = (rid >= S_LO) & (rid < S_HI)
        ties = jnp.max(jnp.where((cT > kf) & (T > IMIN) & token_row, 1.0, 0.0)) > 0.0

        @pl.when(ties)
        def _():
            need = kf - count_ge(T + 1)
            triu = jnp.where(lax.broadcasted_iota(jnp.int32, (LANES, LANES), 0)
                             <= lax.broadcasted_iota(jnp.int32, (LANES, LANES), 1), 1.0, 0.0).astype(BF16)

            def demote(blk, seen):
                eq = blk == T
                pre = jnp.dot(jnp.where(eq, 1.0, 0.0).astype(BF16), triu, preferred_element_type=F32)
                return jnp.where(eq & (seen + pre > need), IMIN, blk), seen + pre[:, LANES - 1:LANES]

            def body(j, seen):
                kj = keysp_ref[j]
                cols = []
                for g in range(G):
                    blk, seen = demote(kj[:, g * LANES:(g + 1) * LANES], seen)
                    cols.append(blk)
                keysp_ref[j] = jnp.concatenate(cols, axis=1)
                return seen

            seen = lax.fori_loop(0, NS, body, jnp.zeros((R, 1), F32))
            blk, _ = demote(keysn_ref[...], seen)
            keysn_ref[...] = blk

        tp_ref[...] = jnp.broadcast_to(jnp.maximum(T, IMIN + 1), (R, LANES))


def _dsa_s_index_call(page_table, z, cache_kidx, layer, DB, NP, G, topk):
    NS = NP // G
    GW = G * PAGE_SIZE
    kern = functools.partial(_dsa_s_index_kernel, NS=NS, G=G, topk=topk)
    QW = IDX_HEADS * IDX_DIM
    grid_spec = pltpu.PrefetchScalarGridSpec(
        num_scalar_prefetch=1,
        grid=(DB, NS),
        in_specs=[pl.BlockSpec((SROWS, QW), lambda b, p, pt: (b, Z_IQ // QW)),
                  pl.BlockSpec((SROWS, LANES), lambda b, p, pt: (b, Z_SM // LANES))]
                 + [pl.BlockSpec((None, None, IDX_DIM, PAGE_SIZE), _page_map(layer, NS, G, g)) for g in range(G)],
        out_specs=[pl.BlockSpec((None, NS, SROWS, GW), lambda b, p, pt: (b, 0, 0, 0)),
                   pl.BlockSpec((None, SROWS, LANES), lambda b, p, pt: (b, 0, 0)),
                   pl.BlockSpec((None, SROWS, LANES), lambda b, p, pt: (b, 0, 0))],
    )
    return pl.pallas_call(
        kern,
        grid_spec=grid_spec,
        out_shape=[jax.ShapeDtypeStruct((DB, NS, SROWS, GW), jnp.int32),
                   jax.ShapeDtypeStruct((DB, SROWS, LANES), jnp.int32),
                   jax.ShapeDtypeStruct((DB, SROWS, LANES), jnp.int32)],
        compiler_params=_cparams(("arbitrary", "arbitrary")),
        name="dsa_s_index",
    )(page_table.reshape(-1), z, z, *([cache_kidx] * G))


def _dsa_s_attend_kernel(pt_ref, rbr_ref, q_ref, keysp_ref, keysn_ref, tp_ref, *rest, NS, G, past):
    kp_refs, vp_refs = rest[:G], rest[G:2 * G]
    kn_ref, vn_ref, o_ref, m_scr, l_scr, acc_scr = rest[2 * G:]
    p = pl.program_id(1)
    R, H = SROWS, ATT_HEADS
    HR, W = H * R, H * ATT_DH
    rowi = lax.broadcasted_iota(jnp.int32, (HR, PAGE_SIZE), 0)
    coli = lax.broadcasted_iota(jnp.int32, (HR, PAGE_SIZE), 1)
    qpos = past + _imod(rowi, R) - S_LO
    rowhead = _idiv(lax.broadcasted_iota(jnp.int32, (HR, ATT_DH), 0), R)
    qa = _stack_heads(q_ref[...], H, ATT_DH)
    q_bd = jnp.concatenate([jnp.where(rowhead == h, qa, 0.0) for h in range(H)], axis=1).astype(BF16)
    Tp = tp_ref[...]
    scale = ATT_DH ** -0.5

    @pl.when(p == 0)
    def _():
        m_scr[...] = jnp.full(m_scr.shape, NEG, F32)
        l_scr[...] = jnp.zeros_like(l_scr)
        acc_scr[...] = jnp.zeros_like(acc_scr)

    def page_matrix(ref):
        return jnp.concatenate([ref[pl.ds(h, PAGE_SIZE, stride=H), :] for h in range(H)], axis=1).astype(BF16)

    def new_matrix(ref):
        return jnp.concatenate([ref[...], jnp.zeros((PAGE_SIZE - R, W), F32)], axis=0).astype(BF16)

    def process(pages, near):
        lgs = []
        for ktile, kbase, kmat, _ in pages:
            s = lax.dot_general(q_bd, kmat(), _DIMS['nt'], preferred_element_type=F32)
            madd = jnp.concatenate([jnp.where(ktile >= Tp, 0.0, NEG)] * H, axis=0)
            if near:
                bk = _t5_bucket(jnp.maximum(qpos - (kbase + coli), 0))
                bias = jnp.zeros((HR, PAGE_SIZE), F32)
                for jb in range(N_BUCKETS):
                    bias = jnp.where(bk == jb, rbr_ref[:, jb:jb + 1], bias)
            else:
                bias = rbr_ref[:, N_BUCKETS - 1:N_BUCKETS]
            lgs.append(s * scale + bias + madd)
        mx = lgs[0]
        for lg in lgs[1:]:
            mx = jnp.maximum(mx, lg)
        m_old = m_scr[...]
        m_new = jnp.maximum(m_old, jnp.max(mx, axis=1, keepdims=True))
        corr = jnp.exp(m_old - m_new)
        tot, pv = None, None
        for lg, (_, _, _, vmat) in zip(lgs, pages):
            pr = jnp.exp(lg - m_new)
            d = jnp.dot(pr.astype(BF16), vmat(), preferred_element_type=F32)
            tot = pr if tot is None else tot + pr
            pv = d if pv is None else pv + d
        l_scr[...] = l_scr[...] * corr + jnp.sum(tot, axis=1, keepdims=True)
        acc_scr[...] = acc_scr[...] * corr + pv
        m_scr[...] = m_new

    def cache_pages():
        kt = keysp_ref[...]
        return [(kt[:, g * PAGE_SIZE:(g + 1) * PAGE_SIZE], (p * G + g) * PAGE_SIZE,
                 functools.partial(page_matrix, kp_refs[g]), functools.partial(page_matrix, vp_refs[g]))
                for g in range(G)]

    @pl.when(p < NS - 1)
    def _():
        process(cache_pages(), False)

    @pl.when(p == NS - 1)
    def _():
        process(cache_pages(), True)
        process([(keysn_ref[...], past - S_LO, functools.partial(new_matrix, kn_ref),
                  functools.partial(new_matrix, vn_ref))], True)
        inv = 1.0 / l_scr[...]
        for h in range(H):
            hs = slice(h * ATT_DH, (h + 1) * ATT_DH)
            o_ref[:, hs] = acc_scr[h * R:(h + 1) * R, hs] * inv[h * R:(h + 1) * R, :]


def _dsa_s_attend_call(page_table, rbrows, z, keysp, keysn, tp, cache_k, cache_v, layer, DB, NP, G, past):
    NS = NP // G
    GW = G * PAGE_SIZE
    W = ATT_HEADS * ATT_DH
    PW = PAGE_SIZE * ATT_HEADS
    assert G * PAGE_SIZE >= MAX_DISTANCE
    kern = functools.partial(_dsa_s_attend_kernel, NS=NS, G=G, past=past)
    page_specs = [pl.BlockSpec((None, None, PW, ATT_DH), _page_map(layer, NS, G, g)) for g in range(G)]
    grid_spec = pltpu.PrefetchScalarGridSpec(
        num_scalar_prefetch=1,
        grid=(DB, NS),
        in_specs=[pl.BlockSpec((ATT_HEADS * SROWS, LANES), lambda b, p, pt: (0, 0)),
                  pl.BlockSpec((SROWS, W), lambda b, p, pt: (b, Z_AQ // W)),
                  pl.BlockSpec((None, None, SROWS, GW),
                               lambda b, p, pt: (b, p // (keysp.shape[3] // GW), 0, p % (keysp.shape[3] // GW))),
                  pl.BlockSpec((None, SROWS, LANES), lambda b, p, pt: (b, 0, 0)),
                  pl.BlockSpec((None, SROWS, LANES), lambda b, p, pt: (b, 0, 0))]
                 + page_specs + page_specs
                 + [pl.BlockSpec((SROWS, W), lambda b, p, pt: (b, Z_AK // W)),
                    pl.BlockSpec((SROWS, W), lambda b, p, pt: (b, Z_AV // W))],
        out_specs=pl.BlockSpec((SROWS, W), lambda b, p, pt: (b, 0)),
        scratch_shapes=[pltpu.VMEM((ATT_HEADS * SROWS, 1), F32),
                        pltpu.VMEM((ATT_HEADS * SROWS, 1), F32),
                        pltpu.VMEM((ATT_HEADS * SROWS, W), F32)],
    )
    return pl.pallas_call(
        kern,
        grid_spec=grid_spec,
        out_shape=jax.ShapeDtypeStruct((DB * SROWS, W), F32),
        compiler_params=_cparams(("arbitrary", "arbitrary")),
        name="dsa_s_attend",
    )(page_table.reshape(-1), rbrows, z, keysp, keysn, tp, *([cache_k] * G), *([cache_v] * G), z, z)


def _merge_kernel(oa_ref, ob_ref, oc_ref, g0_ref, g1_ref, g2_ref, x_ref, wb_ref, wo_ref, h_ref):
    acc = None
    for i, (o_ref, g_ref) in enumerate(((oa_ref, g0_ref), (ob_ref, g1_ref), (oc_ref, g2_ref))):
        br = jnp.dot(o_ref[...].astype(BF16), wb_ref[i], preferred_element_type=F32)
        term = _sigmoid(g_ref[...]) * br
        acc = term if acc is None else acc + term
    h_ref[...] = x_ref[...] + jnp.dot(acc.astype(BF16), wo_ref[...], preferred_element_type=F32)


def _merge_call(oa, ob, oc, z, x, wb, wo, layer, tm):
    M = x.shape[0]
    W = BRANCH_W
    g0 = Z_GATE // D_MODEL
    row = lambda c: (lambda i: (i, c))
    return pl.pallas_call(
        _merge_kernel,
        grid=(M // tm,),
        in_specs=[pl.BlockSpec((tm, W), row(0)), pl.BlockSpec((tm, W), row(0)), pl.BlockSpec((tm, W), row(0)),
                  pl.BlockSpec((tm, D_MODEL), row(g0)), pl.BlockSpec((tm, D_MODEL), row(g0 + 1)),
                  pl.BlockSpec((tm, D_MODEL), row(g0 + 2)),
                  pl.BlockSpec((tm, D_MODEL), row(0)),
                  pl.BlockSpec((None, N_BRANCH, W, D_MODEL), lambda i: (layer, 0, 0, 0)),
                  pl.BlockSpec((None, D_MODEL, D_MODEL), lambda i: (layer, 0, 0))],
        out_specs=pl.BlockSpec((tm, D_MODEL), row(0)),
        out_shape=jax.ShapeDtypeStruct((M, D_MODEL), F32),
        compiler_params=_cparams(("parallel",)),
        name="merge",
    )(oa, ob, oc, z, z, z, x, wb, wo)


def _ffn_down_kernel(a_ref, h_ref, cw_ref, wd_ref, gf_ref, y_ref, prev_scr, act_scr, *, final_norm):
    t = pl.program_id(1)
    tm = a_ref.shape[0]
    FH = D_FF // 2

    @pl.when(t == 0)
    def _():
        prev_scr[...] = jnp.zeros_like(prev_scr)

    acc = h_ref[...]
    for c in range(2):
        gs = slice(c * FH, (c + 1) * FH)
        vs = slice(D_FF + c * FH, D_FF + (c + 1) * FH)
        yg, yg0 = _conv_tile(a_ref[:, gs], prev_scr[:, gs], cw_ref[:, gs], FFN_CONV)
        yv, yv0 = _conv_tile(a_ref[:, vs], prev_scr[:, vs], cw_ref[:, vs], FFN_CONV)
        act_scr[...] = (_silu(yg) * yv).astype(BF16)
        act_scr[0:2 * SUBLANES, :] = jnp.concatenate(
            [_silu(yg0) * yv0, _silu(yg[SUBLANES:2 * SUBLANES]) * yv[SUBLANES:2 * SUBLANES]], axis=0).astype(BF16)
        acc = acc + jnp.dot(act_scr[...], wd_ref[gs, :], preferred_element_type=F32)
    prev_scr[...] = a_ref[tm - SUBLANES:tm, :]
    if final_norm:
        acc = acc * lax.rsqrt(jnp.mean(acc * acc, axis=-1, keepdims=True) + EPS) * gf_ref[...]
    y_ref[...] = acc


def _ffn_down_call(a, h, conv_w, wd, gf, layer, n_outer, tm, final_norm):
    M = h.shape[0]
    nt = M // (n_outer * tm)
    kern = functools.partial(_ffn_down_kernel, final_norm=final_norm)
    return pl.pallas_call(
        kern,
        grid=(n_outer, nt),
        in_specs=[pl.BlockSpec((tm, 2 * D_FF), lambda b, t: (b * nt + t, 0)),
                  pl.BlockSpec((tm, D_MODEL), lambda b, t: (b * nt + t, 0)),
                  pl.BlockSpec((None, FFN_CONV, 2 * D_FF), lambda b, t: (layer, 0, 0)),
                  pl.BlockSpec((None, D_FF, D_MODEL), lambda b, t: (layer, 0, 0)),
                  pl.BlockSpec((1, D_MODEL), lambda b, t: (0, 0))],
        out_specs=pl.BlockSpec((tm, D_MODEL), lambda b, t: (b * nt + t, 0)),
        out_shape=jax.ShapeDtypeStruct((M, D_MODEL), F32),
        scratch_shapes=[pltpu.VMEM((SUBLANES, 2 * D_FF), F32),
                        pltpu.VMEM((tm, D_FF // 2), BF16)],
        compiler_params=_cparams(("arbitrary", "arbitrary")),
        name="ffn_down",
    )(a, h, conv_w, wd, gf.reshape(1, D_MODEL))


def _kv_cast_kernel(k_ref, v_ref, kb_ref, vt_ref):
    kb_ref[...] = k_ref[...].astype(BF16)
    vt_ref[...] = v_ref[...].T.astype(BF16)


def _kv_cast_call(z, TQ):
    M = z.shape[0]
    W = ATT_HEADS * ATT_DH
    return pl.pallas_call(
        _kv_cast_kernel,
        grid=(M // TQ,),
        in_specs=[pl.BlockSpec((TQ, W), lambda i: (i, Z_AK // W)),
                  pl.BlockSpec((TQ, W), lambda i: (i, Z_AV // W))],
        out_specs=[pl.BlockSpec((TQ, W), lambda i: (i, 0)),
                   pl.BlockSpec((None, W, TQ), lambda i: (i, 0, 0))],
        out_shape=[jax.ShapeDtypeStruct((M, W), BF16), jax.ShapeDtypeStruct((M // TQ, W, TQ), BF16)],
        compiler_params=_cparams(("parallel",)),
        name="kv_cast",
    )(z, z)


def _kv_rows_kernel(*refs, depth):
    ins, (ko_ref, vo_ref) = refs[:2 * depth], refs[2 * depth:]
    l = pl.program_id(0)
    tm = ko_ref.shape[0] // ATT_HEADS
    for d in range(depth):
        @pl.when(l == d)
        def _():
            for src, dst in ((ins[2 * d], ko_ref), (ins[2 * d + 1], vo_ref)):
                x = src[...]
                for h in range(ATT_HEADS):
                    dst[pl.ds(h, tm, stride=ATT_HEADS), :] = x[:, h * ATT_DH:(h + 1) * ATT_DH]


def _kv_rows_call(zs_per_layer, tm):
    depth = len(zs_per_layer)
    M = zs_per_layer[0].shape[0]
    W = ATT_HEADS * ATT_DH
    in_specs, args = [], []
    for d, z in enumerate(zs_per_layer):
        rows = lambda l, i, d=d: jnp.where(l == d, i, 0)
        in_specs += [pl.BlockSpec((tm, W), lambda l, i, rows=rows: (rows(l, i), Z_AK // W)),
                     pl.BlockSpec((tm, W), lambda l, i, rows=rows: (rows(l, i), Z_AV // W))]
        args += [z, z]
    out = jax.ShapeDtypeStruct((depth, M * ATT_HEADS, ATT_DH), F32)
    return pl.pallas_call(
        functools.partial(_kv_rows_kernel, depth=depth),
        grid=(depth, M // tm),
        in_specs=in_specs,
        out_specs=[pl.BlockSpec((None, tm * ATT_HEADS, ATT_DH), lambda l, i: (l, i, 0))] * 2,
        out_shape=[out, out],
        compiler_params=_cparams(("parallel", "parallel")),
        name="kv_rows",
    )(*args)


def _rope_tables(pos):
    half = RET_DK // 2
    inv = 1.0 / (ROPE_BASE ** jnp.linspace(0.0, 1.0, half, dtype=F32))
    ang = pos.astype(F32)[:, None] * inv
    cos, sin = jnp.cos(ang), jnp.sin(ang)
    return jnp.concatenate([cos, cos], axis=-1), jnp.concatenate([-sin, sin], axis=-1)


def _dn_params(a_log, dt_bias):
    hp = jnp.zeros((SUBLANES, LANES), F32)
    hp = hp.at[0, SM_DNA:SM_DNA + DN_HEADS].set(a_log.astype(F32))
    hp = hp.at[1, SM_DNA:SM_DNA + DN_HEADS].set(dt_bias.astype(F32))
    return hp


def _mix_and_ffn(x, z, oa, ob, oc, sw, l, final, n_outer, tm_merge, tm_up, tn_up, tm_down, ffn_state=None):
    h = _merge_call(oa, ob, oc, z, x, sw['wb'], sw['wo'], l, tm_merge)
    a = _rms_matmul(h, sw['norm_ffn'], sw['w_up'], l, tm_up, tn_up)
    a_raw = a
    if ffn_state is not None:
        DB = ffn_state.shape[0]
        a = a.reshape(DB, SROWS, 2 * D_FF).at[:, S_LO - (FFN_CONV - 1):S_LO].set(ffn_state)
        a = a.reshape(DB * SROWS, 2 * D_FF)
    y = _ffn_down_call(a, h, sw['ffn_conv_w'], sw['wd'], sw['norm_final'], l, n_outer, tm_down, final)
    return y, a_raw


def kernel(x_prompt, x_sample, cache_k, cache_v, cache_kidx, state_dn_conv, state_dn, state_ret,
           state_ffn_conv, page_table, norm_mix, w_in, dn_conv_w, dn_a_log, dn_dt_bias, dn_norm,
           ret_norm, rel_bias, w_branch, w_o, norm_ffn, w_up, ffn_conv_w, w_down, norm_final):
    B, S, D = x_prompt.shape
    DB, DS, _ = x_sample.shape
    depth = w_in.shape[0]
    NP = page_table.shape[1]
    past = NP * PAGE_SIZE
    n_phys = cache_k.shape[1]
    W = ATT_HEADS * ATT_DH
    assert DS == S_HI - S_LO and S % CHUNK == 0 and (DB * SROWS) % CHUNK == 0

    TL = 256 if S % 256 == 0 else CHUNK
    TQ = 256 if S % 256 == 0 else CHUNK
    tm_p = 512 if (B * S) % 512 == 0 else CHUNK
    tm_mm = 1024 if (B * S) % 1024 == 0 else tm_p
    tm_d = 256 if S % 256 == 0 else CHUNK
    MS = DB * SROWS
    NG = MS // CHUNK
    seg_per = CHUNK // SROWS

    xp = x_prompt.reshape(B * S, D)
    xs = jnp.zeros((DB, SROWS, D), F32).at[:, S_LO:S_HI].set(x_sample).reshape(MS, D)
    cos_p, sin_p = _rope_tables(jnp.arange(S))
    pos_s = past + (jnp.arange(CHUNK) % SROWS) - S_LO
    cos_s, sin_s = _rope_tables(pos_s)
    ck = cache_k.reshape(depth, n_phys, PAGE_SIZE * ATT_HEADS, ATT_DH)
    cv = cache_v.reshape(depth, n_phys, PAGE_SIZE * ATT_HEADS, ATT_DH)
    ckidx_t = jnp.swapaxes(cache_kidx, 2, 3)
    rb = rel_bias.astype(F32)
    rbrows = jnp.pad(jnp.repeat(rb.T, SROWS, axis=0), ((0, 0), (0, LANES - N_BUCKETS)))
    G = next(g for g in (8, 4, 2, 1) if NP % g == 0)
    G_idx = 2 * G if NP % (2 * G) == 0 else G
    zeros_p = jnp.zeros((B, DN_HEADS, DN_DK, DN_DV), F32)
    topk_p = min(TOPK_MAX, S // 4)
    topk_s = min(TOPK_MAX, (past + DS) // 4)

    sw = dict(w_in=_prep_w_in(w_in), wb=w_branch.astype(BF16), wo=w_o.astype(BF16), w_up=w_up.astype(BF16),
              wd=w_down.astype(BF16), norm_ffn=norm_ffn, ffn_conv_w=ffn_conv_w, norm_final=norm_final)

    p_states, s_states, z_prompt = [], [], []
    for l in range(depth):
        lw = dict(dn_conv_w=dn_conv_w[l], hp=_dn_params(dn_a_log[l], dn_dt_bias[l]),
                  dn_norm=dn_norm[l].reshape(1, DN_DV), ret_norm=ret_norm[l])
        final = l == depth - 1

        z = _rms_matmul(xp, norm_mix, sw['w_in'], l, tm_mm, 1536)
        oa, dn_s = _dn_call(z, lw['dn_conv_w'], lw['hp'], lw['dn_norm'], zeros_p, B, TL, CHUNK, 0, CHUNK)
        ob, ret_s = _ret_call(z, cos_p, sin_p, lw['ret_norm'], zeros_p, B, TL, CHUNK, 0, CHUNK)
        kb, vt = _kv_cast_call(z, TQ)
        oc = _dsa_prompt_call(rb, z, kb, vt.reshape(B, S // TQ, W, TQ), B, S, TQ, topk_p)
        xp, a_up = _mix_and_ffn(xp, z, oa, ob, oc, sw, l, final, B, tm_p, tm_mm, 1408, tm_d)
        z3 = z.reshape(B, S, Z_COLS)
        z_prompt.append(z)
        p_states.append((z3[:, S - (DN_CONV - 1):, Z_DNQKV:Z_DNQKV + DN_QKV], dn_s, ret_s, None, None,
                         z3[:, :, Z_SM + SM_IK:Z_SM + SM_IK + IDX_DIM],
                         a_up.reshape(B, S, 2 * D_FF)[:, S - (FFN_CONV - 1):]))

        zs = _rms_matmul(xs, norm_mix, sw['w_in'], l, MS, 1024)
        zs3 = zs.reshape(DB, SROWS, Z_COLS)
        zs_conv = zs3.at[:, S_LO - (DN_CONV - 1):S_LO, Z_DNQKV:Z_DNQKV + DN_QKV].set(state_dn_conv[l])
        zs_conv = zs_conv.reshape(MS, Z_COLS)
        oa, dn_s = _dn_call(zs_conv, lw['dn_conv_w'], lw['hp'], lw['dn_norm'], state_dn[l], NG, CHUNK,
                            SROWS, S_LO, S_HI)
        ob, ret_s = _ret_call(zs, cos_s, sin_s, lw['ret_norm'], state_ret[l], NG, CHUNK, SROWS, S_LO, S_HI)
        keysp, keysn, tp = _dsa_s_index_call(page_table, zs, ckidx_t, l, DB, NP, G_idx, topk_s)
        oc = _dsa_s_attend_call(page_table, rbrows, zs, keysp, keysn, tp, ck, cv, l, DB, NP, G, past)
        xs, a_up = _mix_and_ffn(xs, zs, oa, ob, oc, sw, l, final, 1, MS, MS, 1408, MS,
                                ffn_state=state_ffn_conv[l])
        tok = zs3[:, S_LO:S_HI]
        s_states.append((tok[:, DS - (DN_CONV - 1):, Z_DNQKV:Z_DNQKV + DN_QKV], dn_s, ret_s,
                         tok[:, :, Z_AK:Z_AK + W].reshape(DB, DS, ATT_HEADS, ATT_DH),
                         tok[:, :, Z_AV:Z_AV + W].reshape(DB, DS, ATT_HEADS, ATT_DH),
                         tok[:, :, Z_SM + SM_IK:Z_SM + SM_IK + IDX_DIM],
                         a_up.reshape(DB, SROWS, 2 * D_FF)[:, S_HI - (FFN_CONV - 1):S_HI]))

    y_prompt = xp.reshape(B, S, D)
    y_sample = xs.reshape(DB, SROWS, D)[:, S_LO:S_HI]
    p_k, p_v = (t.reshape(depth, B, S, ATT_HEADS, ATT_DH) for t in _kv_rows_call(z_prompt, tm_p))
    stk = lambda states, i: jnp.stack([st[i] for st in states])
    p_out = [stk(p_states, i) for i in (0, 1, 2)] + [p_k, p_v] + [stk(p_states, i) for i in (5, 6)]
    return (y_prompt, y_sample) + tuple(p_out) + tuple(stk(s_states, i) for i in range(7))
```

```python
import functools
import math

import numpy as np
import jax
import jax.numpy as jnp
from jax import lax
from jax.experimental import pallas as pl
from jax.experimental.pallas import tpu as pltpu

D_MODEL = 1024
DEPTH = 2
PAST_LEN = 8192
PAGE_SIZE = 128
DN_HEADS = 4
DN_DK = 128
DN_DV = 128
DN_CONV = 4
DN_QKV = 2 * DN_HEADS * DN_DK + DN_HEADS * DN_DV
RET_HEADS = 4
RET_DK = 128
RET_DV = 128
ROPE_BASE = 10000.0
ATT_HEADS = 4
ATT_DH = 128
IDX_HEADS = 4
IDX_DIM = 64
TOPK_MAX = 256
N_BUCKETS = 32
MAX_DISTANCE = 128
N_BRANCH = 3
BRANCH_W = DN_HEADS * DN_DV
D_FF = 2816
FFN_CONV = 3
EPS = 1e-6
F32 = jnp.float32
BF16 = jnp.bfloat16
IN_SIZES = (DN_QKV, DN_HEADS * DN_DV, DN_HEADS, DN_HEADS,
            RET_HEADS * RET_DK, RET_HEADS * RET_DK, RET_HEADS * RET_DV, RET_HEADS * RET_DV,
            ATT_HEADS * ATT_DH, ATT_HEADS * ATT_DH, ATT_HEADS * ATT_DH,
            IDX_HEADS * IDX_DIM, IDX_DIM, IDX_HEADS, N_BRANCH * D_MODEL)

Z_DNQKV = 0
Z_DNZ = 1536
Z_GATE = 2048
Z_RQ, Z_RK, Z_RV, Z_RG = 5120, 5632, 6144, 6656
Z_AQ, Z_AK, Z_AV = 7168, 7680, 8192
Z_IQ = 8704
Z_SM = 8960
SM_IK, SM_IW, SM_DNB, SM_DNA = 0, 64, 68, 72
Z_COLS = 9216

LANES = 128
SUBLANES = 8
CHUNK = 128
SROWS = 8
S_LO, S_HI = 3, 7
NEG = -1e30
IMIN = -2 ** 31
HALF = 2 ** 15
PACK16 = 2 * SUBLANES
FFN_CHUNK = 2 * LANES
VMEM_LIMIT = 56 * 1024 * 1024


def _cparams(sem):
    return pltpu.CompilerParams(dimension_semantics=sem, vmem_limit_bytes=VMEM_LIMIT)


def _sigmoid(x):
    return 1.0 / (1.0 + jnp.exp(-x))


def _silu(x):
    return x * _sigmoid(x)


def _softplus(x):
    return jnp.maximum(x, 0.0) + jnp.log(1.0 + jnp.exp(-jnp.abs(x)))


_DIMS = {'nn': (((1,), (0,)), ((), ())), 'nt': (((1,), (1,)), ((), ())), 'tn': (((0,), (0,)), ((), ()))}


def _split_bf16(a, n):
    parts = []
    r = a
    for i in range(n):
        p = r.astype(BF16)
        parts.append(p)
        if i + 1 < n:
            r = r - p.astype(F32)
    return parts


def _mm(a, b, dims='nn', mode='bf16'):
    dn = _DIMS[dims]
    dg = lambda x, y: lax.dot_general(x, y, dn, preferred_element_type=F32)
    if mode == 'bf16':
        return dg(a.astype(BF16), b.astype(BF16))
    if mode == 'x3':
        ah, al = _split_bf16(a, 2)
        bh, bl = _split_bf16(b, 2)
        return dg(ah, bh) + dg(ah, bl) + dg(al, bh)
    if mode == 'l01':
        ab = a.astype(BF16)
        b1, b2, b3 = _split_bf16(b, 3)
        return dg(ab, b1) + dg(ab, b2) + dg(ab, b3)
    raise ValueError(mode)


def _rms_mm_kernel(x_ref, g_ref, w_ref, o_ref, u_ref):
    @pl.when(pl.program_id(1) == 0)
    def _():
        x = x_ref[...]
        r = lax.rsqrt(jnp.mean(x * x, axis=-1, keepdims=True) + EPS)
        u_ref[...] = (x * r * g_ref[...]).astype(u_ref.dtype)

    o_ref[...] = jnp.dot(u_ref[...], w_ref[...], preferred_element_type=F32)


def _rms_matmul(x, g, w, layer, tm, tn):
    M, K = x.shape
    N = w.shape[2]
    return pl.pallas_call(
        _rms_mm_kernel,
        grid=(M // tm, N // tn),
        in_specs=[pl.BlockSpec((tm, K), lambda i, j: (i, 0)),
                  pl.BlockSpec((None, 1, K), lambda i, j: (layer, 0, 0)),
                  pl.BlockSpec((None, K, tn), lambda i, j: (layer, 0, j))],
        out_specs=pl.BlockSpec((tm, tn), lambda i, j: (i, j)),
        out_shape=jax.ShapeDtypeStruct((M, N), F32),
        scratch_shapes=[pltpu.VMEM((tm, K), BF16)],
        compiler_params=_cparams(("parallel", "arbitrary")),
        name="rms_matmul",
    )(x, g.reshape(g.shape[0], 1, K), w)


_IN_OFFS = [0] + np.cumsum(np.array(IN_SIZES)).tolist()
_SRC_DNB, _SRC_RQ, _SRC_IK, _SRC_IW, _SRC_GATE, IN_COLS = (_IN_OFFS[2], _IN_OFFS[4], _IN_OFFS[12], _IN_OFFS[13],
                                                           _IN_OFFS[14], _IN_OFFS[15])
IN_COLS_PAD = -(-IN_COLS // LANES) * LANES


def _prep_w_in_kernel(w_ref, o_ref):
    def shifted(src, width):
        a = src // LANES * LANES
        win = -(-(src - a + width) // LANES) * LANES
        return pltpu.roll(w_ref[:, a:a + win], win - (src - a), axis=1)[:, 0:width]

    n_head = Z_GATE
    o_ref[:, 0:n_head] = w_ref[:, 0:n_head].astype(BF16)
    o_ref[:, Z_GATE:Z_GATE + N_BRANCH * D_MODEL] = shifted(_SRC_GATE, N_BRANCH * D_MODEL).astype(BF16)
    o_ref[:, Z_RQ:Z_SM] = shifted(_SRC_RQ, Z_SM - Z_RQ).astype(BF16)
    lane = lax.broadcasted_iota(jnp.int32, (w_ref.shape[0], LANES), 1)
    n_idx = IDX_DIM + IDX_HEADS
    a_ik = _SRC_IK // LANES * LANES
    idx_part = pltpu.roll(w_ref[:, a_ik:a_ik + LANES], LANES - (_SRC_IK - a_ik), axis=1)
    dn_part = pltpu.roll(w_ref[:, _SRC_DNB:_SRC_DNB + LANES], SM_DNB, axis=1)
    small = jnp.where(lane < n_idx, idx_part, jnp.where(lane < n_idx + 2 * DN_HEADS, dn_part, 0.0))
    o_ref[:, Z_SM:Z_SM + LANES] = small.astype(BF16)
    o_ref[:, Z_SM + LANES:Z_COLS] = jnp.zeros((w_ref.shape[0], Z_COLS - Z_SM - LANES), BF16)


def _prep_w_in(w_in):
    depth, K, _ = w_in.shape
    assert _SRC_DNB % LANES == 0 and _IN_OFFS[1] == Z_DNZ and _SRC_DNB == Z_GATE
    assert _SRC_IW - _SRC_IK == IDX_DIM and _SRC_GATE - _SRC_IW == IDX_HEADS
    assert (_SRC_IK % LANES) + IDX_DIM + IDX_HEADS <= LANES and SM_DNB == IDX_DIM + IDX_HEADS
    assert Z_SM - Z_RQ == _SRC_IK - _SRC_RQ and SM_DNA == SM_DNB + DN_HEADS
    tr = 256
    wp = jnp.pad(w_in, ((0, 0), (0, 0), (0, IN_COLS_PAD - IN_COLS)))
    return pl.pallas_call(
        _prep_w_in_kernel,
        grid=(depth, K // tr),
        in_specs=[pl.BlockSpec((None, tr, IN_COLS_PAD), lambda l, i: (l, i, 0))],
        out_specs=pl.BlockSpec((None, tr, Z_COLS), lambda l, i: (l, i, 0)),
        out_shape=jax.ShapeDtypeStruct((depth, K, Z_COLS), BF16),
        compiler_params=_cparams(("parallel", "parallel")),
        name="prep_w_in",
    )(wp)


def _conv_tile(x, prev8, w, width):
    y = x * w[width - 1:width, :]
    for s in range(1, width):
        y = y + pltpu.roll(x, s, axis=0) * w[width - 1 - s:width - s, :]
    x0 = x[0:SUBLANES, :]
    rid = lax.broadcasted_iota(jnp.int32, x0.shape, 0)
    y0 = x0 * w[width - 1:width, :]
    for s in range(1, width):
        xs = jnp.where(rid < s, pltpu.roll(prev8, s, axis=0), pltpu.roll(x0, s, axis=0))
        y0 = y0 + xs * w[width - 1 - s:width - s, :]
    return y, y0


def _idiv(x, n):
    assert n & (n - 1) == 0
    return lax.shift_right_arithmetic(x, jnp.int32(n.bit_length() - 1))


def _imod(x, n):
    assert n & (n - 1) == 0
    return x & jnp.int32(n - 1)


def _chunk_masks(C, seg):
    ri = lax.broadcasted_iota(jnp.int32, (C, C), 0)
    ci = lax.broadcasted_iota(jnp.int32, (C, C), 1)
    if seg == C:
        return ri >= ci, ri > ci, None
    same = _idiv(ri, seg) == _idiv(ci, seg)
    return (ri >= ci) & same, (ri > ci) & same, same


def _valid_col(C, seg, lo, hi):
    r = _imod(lax.broadcasted_iota(jnp.int32, (C, 1), 0), seg)
    return jnp.where((r >= lo) & (r < hi), 1.0, 0.0)


def _tri_inv(ms, span):
    C = ms[0].shape[0]
    eye = jnp.where(lax.broadcasted_iota(jnp.int32, (C, C), 0) == lax.broadcasted_iota(jnp.int32, (C, C), 1),
                    1.0, 0.0)
    invs = [eye - m for m in ms]
    ps = list(ms)
    n = 2
    while n < span:
        ps = [_mm(p, p, 'nn', 'x3') for p in ps]
        invs = [inv + _mm(inv, p, 'nn', 'x3') for inv, p in zip(invs, ps)]
        n *= 2
    return invs


def _state_update(S_scr, h, u, kcum, qd, kd, qk, gtot, C, seg):
    nseg = C // seg
    ws, o1s = [], []
    for sg in range(nseg):
        rs = slice(sg * seg, (sg + 1) * seg)
        S = S_scr[sg, h]
        if kcum is None:
            ws.append(u[rs])
        else:
            ws.append(u[rs] - _mm(kcum[rs], S))
        o1s.append(_mm(qd[rs], S))
    w = ws[0] if nseg == 1 else jnp.concatenate(ws, axis=0)
    o1 = o1s[0] if nseg == 1 else jnp.concatenate(o1s, axis=0)
    o = o1 + _mm(qk, w)
    rowid = lax.broadcasted_iota(jnp.int32, (C, 1), 0)
    for sg in range(nseg):
        kdm = kd if nseg == 1 else jnp.where(_idiv(rowid, seg) == sg, kd, 0.0)
        gt = jnp.exp(gtot[sg * seg:sg * seg + 1, :])
        S_scr[sg, h] = S_scr[sg, h] * gt + _mm(kdm, w, 'tn')
    return o


def _dn_kernel(qkv_ref, dz_ref, sm_ref, cw_ref, hp_ref, nrm_ref, s0_ref, o_ref, sfin_ref,
               S_scr, prev_scr, c_scr, *, C, seg, lo, hi):
    t = pl.program_id(1)
    TL = qkv_ref.shape[0]
    H, DK = DN_HEADS, DN_DK
    masked = seg != C

    @pl.when(t == 0)
    def _():
        S_scr[...] = s0_ref[...]
        prev_scr[...] = jnp.zeros_like(prev_scr)

    x = qkv_ref[...]
    y, y0 = _conv_tile(x, prev_scr[...], cw_ref[...], DN_CONV)
    c_scr[...] = _silu(y)
    c_scr[0:SUBLANES, :] = _silu(y0)
    prev_scr[...] = x[TL - SUBLANES:TL, :]

    lowm, strictm, same = _chunk_masks(C, seg)
    ltri = jnp.where(lowm, 1.0, 0.0)
    valid = _valid_col(C, seg, lo, hi) if masked else None
    span = (hi - lo) if masked else C
    a_coef = -jnp.exp(hp_ref[0:1, :])
    dtb = hp_ref[1:2, :]

    units = []
    for cidx in range(TL // C):
        r0 = cidx * C
        cc = c_scr[r0:r0 + C, :]
        sm = sm_ref[r0:r0 + C, :]
        g128 = a_coef * _softplus(sm + dtb)
        b128 = _sigmoid(sm)
        if masked:
            g128 = g128 * valid
            b128 = b128 * valid
        Gc128 = _mm(ltri, g128, 'nn', 'l01')
        if masked:
            Gt128 = _mm(jnp.where(same, 1.0, 0.0), g128, 'nn', 'l01')
        else:
            Gt128 = jnp.broadcast_to(Gc128[C - 1:C, :], Gc128.shape)
        GT = Gc128.T
        for h in range(H):
            q = cc[:, h * DK:(h + 1) * DK]
            k = cc[:, (H + h) * DK:(H + h + 1) * DK]
            v = cc[:, (2 * H + h) * DK:(2 * H + h + 1) * DK]
            q = q * lax.rsqrt(jnp.sum(q * q, axis=-1, keepdims=True) + EPS) * DK ** -0.5
            k = k * lax.rsqrt(jnp.sum(k * k, axis=-1, keepdims=True) + EPS)
            if masked:
                k = k * valid
            Gc = Gc128[:, SM_DNA + h:SM_DNA + h + 1]
            Gr = GT[SM_DNA + h:SM_DNA + h + 1, :]
            Gt = Gt128[:, SM_DNA + h:SM_DNA + h + 1]
            bc = b128[:, SM_DNB + h:SM_DNB + h + 1]
            decay = jnp.where(lowm, jnp.exp(jnp.where(lowm, Gc - Gr, 0.0)), 0.0)
            eG = jnp.exp(Gc)
            units.append(dict(r0=r0, h=h, Gt=Gt, m=jnp.where(strictm, _mm(k, k, 'nt') * decay * bc, 0.0),
                              qk=_mm(q, k, 'nt') * decay, rhs_u=v * bc, rhs_k=k * (bc * eG),
                              qd=q * eG, kd=k * jnp.exp(Gt - Gc)))
    ainvs = _tri_inv([un['m'] for un in units], span)
    for un, ainv in zip(units, ainvs):
        un['u'] = _mm(ainv, un['rhs_u'], 'nn', 'x3')
        un['kcum'] = _mm(ainv, un['rhs_k'], 'nn', 'x3')

    for un in units:
        r0, h = un['r0'], un['h']
        o = _state_update(S_scr, h, un['u'], un['kcum'], un['qd'], un['kd'], un['qk'], un['Gt'], C, seg)
        on = o * lax.rsqrt(jnp.mean(o * o, axis=-1, keepdims=True) + EPS) * nrm_ref[...]
        zg = dz_ref[r0:r0 + C, h * DN_DV:(h + 1) * DN_DV]
        o_ref[r0:r0 + C, h * DN_DV:(h + 1) * DN_DV] = on * _silu(zg)

    @pl.when(t == pl.num_programs(1) - 1)
    def _():
        sfin_ref[...] = S_scr[...]


def _dn_call(z, conv_w, hp, nrm, s0, n_outer, TL, seg, lo, hi):
    M = z.shape[0]
    nt = M // (n_outer * TL)
    nseg = CHUNK // seg
    rowmap = lambda cb: (lambda b, t: (b * nt + t, cb))
    kern = functools.partial(_dn_kernel, C=CHUNK, seg=seg, lo=lo, hi=hi)
    return pl.pallas_call(
        kern,
        grid=(n_outer, nt),
        in_specs=[pl.BlockSpec((TL, DN_QKV), rowmap(Z_DNQKV // DN_QKV)),
                  pl.BlockSpec((TL, BRANCH_W), rowmap(Z_DNZ // BRANCH_W)),
                  pl.BlockSpec((TL, LANES), rowmap(Z_SM // LANES)),
                  pl.BlockSpec((DN_CONV, DN_QKV), lambda b, t: (0, 0)),
                  pl.BlockSpec((SUBLANES, LANES), lambda b, t: (0, 0)),
                  pl.BlockSpec((1, DN_DV), lambda b, t: (0, 0)),
                  pl.BlockSpec((nseg, DN_HEADS, DN_DK, DN_DV), lambda b, t: (b, 0, 0, 0))],
        out_specs=[pl.BlockSpec((TL, BRANCH_W), lambda b, t: (b * nt + t, 0)),
                   pl.BlockSpec((nseg, DN_HEADS, DN_DK, DN_DV), lambda b, t: (b, 0, 0, 0))],
        out_shape=[jax.ShapeDtypeStruct((M, BRANCH_W), F32),
                   jax.ShapeDtypeStruct(s0.shape, F32)],
        scratch_shapes=[pltpu.VMEM((nseg, DN_HEADS, DN_DK, DN_DV), F32),
                        pltpu.VMEM((SUBLANES, DN_QKV), F32),
                        pltpu.VMEM((TL, DN_QKV), F32)],
        compiler_params=_cparams(("arbitrary", "arbitrary")),
        name="dn",
    )(z, z, z, conv_w, hp, nrm, s0)


_LOG_GAMMA = [float(np.log1p(-np.exp2(-5.0 - h))) for h in range(RET_HEADS)]


def _ret_kernel(q_ref, k_ref, v_ref, g_ref, cos_ref, sin_ref, nrm_ref, s0_ref, o_ref, sfin_ref,
                S_scr, *, C, seg, lo, hi):
    t = pl.program_id(1)
    TL = q_ref.shape[0]
    H, DK = RET_HEADS, RET_DK
    masked = seg != C

    @pl.when(t == 0)
    def _():
        S_scr[...] = s0_ref[...]

    lowm, _, _ = _chunk_masks(C, seg)
    ri = lax.broadcasted_iota(jnp.int32, (C, 1), 0)
    ci = lax.broadcasted_iota(jnp.int32, (1, C), 1)
    if masked:
        valid = _valid_col(C, seg, lo, hi)
        cnt_c = jnp.clip(_imod(ri, seg) - lo + 1, 0, hi - lo).astype(F32)
        cnt_r = jnp.clip(_imod(ci, seg) - lo + 1, 0, hi - lo).astype(F32)
        cnt_t = float(hi - lo)
    else:
        valid = None
        cnt_c = (ri + 1).astype(F32)
        cnt_r = (ci + 1).astype(F32)
        cnt_t = float(C)

    for cidx in range(TL // C):
        r0 = cidx * C
        cosf = cos_ref[r0:r0 + C, :]
        sins = sin_ref[r0:r0 + C, :]
        for h in range(H):
            cs = slice(h * DK, (h + 1) * DK)
            q = q_ref[r0:r0 + C, cs]
            k = k_ref[r0:r0 + C, cs]
            v = v_ref[r0:r0 + C, cs]
            q = (q * cosf + pltpu.roll(q, DK // 2, axis=1) * sins) * DK ** -0.5
            k = k * cosf + pltpu.roll(k, DK // 2, axis=1) * sins
            if masked:
                v = v * valid
            lg = _LOG_GAMMA[h]
            Gc = cnt_c * lg
            decay = jnp.where(lowm, jnp.exp(jnp.where(lowm, (cnt_c - cnt_r) * lg, 0.0)), 0.0)
            qk = _mm(q, k, 'nt') * decay
            Gt = jnp.full((C, 1), cnt_t * lg, F32)
            o = _state_update(S_scr, h, v, None, q * jnp.exp(Gc), k * jnp.exp(Gt - Gc), qk, Gt, C, seg)
            mu = jnp.mean(o, axis=-1, keepdims=True)
            oc = o - mu
            var = jnp.mean(oc * oc, axis=-1, keepdims=True)
            on = oc * lax.rsqrt(var + EPS) * nrm_ref[h:h + 1, :]
            o_ref[r0:r0 + C, cs] = on * _silu(g_ref[r0:r0 + C, cs])

    @pl.when(t == pl.num_programs(1) - 1)
    def _():
        sfin_ref[...] = S_scr[...]


def _ret_call(z, cosf, sins, nrm, s0, n_outer, TL, seg, lo, hi):
    M = z.shape[0]
    nt = M // (n_outer * TL)
    nseg = CHUNK // seg
    W = RET_HEADS * RET_DK
    rowmap = lambda cb: (lambda b, t: (b * nt + t, cb))
    kern = functools.partial(_ret_kernel, C=CHUNK, seg=seg, lo=lo, hi=hi)
    return pl.pallas_call(
        kern,
        grid=(n_outer, nt),
        in_specs=[pl.BlockSpec((TL, W), rowmap(Z_RQ // W)),
                  pl.BlockSpec((TL, W), rowmap(Z_RK // W)),
                  pl.BlockSpec((TL, W), rowmap(Z_RV // W)),
                  pl.BlockSpec((TL, W), rowmap(Z_RG // W)),
                  pl.BlockSpec((TL, RET_DK), lambda b, t: (t, 0)),
                  pl.BlockSpec((TL, RET_DK), lambda b, t: (t, 0)),
                  pl.BlockSpec((RET_HEADS, RET_DV), lambda b, t: (0, 0)),
                  pl.BlockSpec((nseg, RET_HEADS, RET_DK, RET_DV), lambda b, t: (b, 0, 0, 0))],
        out_specs=[pl.BlockSpec((TL, W), lambda b, t: (b * nt + t, 0)),
                   pl.BlockSpec((nseg, RET_HEADS, RET_DK, RET_DV), lambda b, t: (b, 0, 0, 0))],
        out_shape=[jax.ShapeDtypeStruct((M, W), F32),
                   jax.ShapeDtypeStruct(s0.shape, F32)],
        scratch_shapes=[pltpu.VMEM((nseg, RET_HEADS, RET_DK, RET_DV), F32)],
        compiler_params=_cparams(("arbitrary", "arbitrary")),
        name="ret",
    )(z, z, z, z, cosf, sins, nrm, s0)


def _f2key(x):
    b = lax.bitcast_convert_type(x + 0.0, jnp.int32)
    return jnp.where(b >= 0, b, b ^ jnp.int32(0x7FFFFFFF))


def _t5_bucket(d):
    exact = N_BUCKETS // 2
    df = d.astype(F32)
    large = exact + (jnp.log(jnp.maximum(df, 1.0) / exact) / math.log(MAX_DISTANCE / exact)
                     * (N_BUCKETS - exact)).astype(jnp.int32)
    large = jnp.minimum(large, N_BUCKETS - 1)
    return jnp.where(d < exact, d, large)


def _bias_from_dist(d, rb_ref, h):
    bk = _t5_bucket(d)
    r = jnp.zeros(d.shape, F32)
    for jb in range(N_BUCKETS):
        r = jnp.where(bk == jb, rb_ref[jb, h], r)
    return r


def _threshold_search(count_ge, shape, total, kf, nbits=32):
    zero = jnp.zeros(shape, jnp.int32)
    c0 = count_ge(zero)
    ok0 = c0 >= kf
    T = jnp.where(ok0, 0, -2 ** (nbits - 1)).astype(jnp.int32)
    cT = jnp.where(ok0, c0, total)

    def body(it, carry):
        T, cT = carry
        cand = T + lax.shift_left(jnp.int32(1), jnp.int32(nbits - 2) - it)
        c = count_ge(cand)
        ok = c >= kf
        return jnp.where(ok, cand, T), jnp.where(ok, c, cT)

    return lax.fori_loop(0, nbits - 1, body, (T, cT))


def _fold_lanes(x):
    f = x[:, 0:LANES]
    for u in range(1, x.shape[1] // LANES):
        f = f + x[:, u * LANES:(u + 1) * LANES]
    return f


def _fold_rows(x):
    return jnp.sum(x.reshape(x.shape[0] // SUBLANES, SUBLANES, x.shape[1]), axis=0)


def _dsa_prompt_kernel(rb_ref, q_ref, qi_ref, smq_ref, k_ref, vt_ref, smk_ref, o_ref,
                       keys_scr, hi_scr, lo_scr, lg_scr, *, TQ, topk):
    i = pl.program_id(1)
    KC = TQ
    nk = i + 1
    kf = float(topk)
    qi = qi_ref[...]
    wT = smq_ref[...].T
    kpos0 = lax.broadcasted_iota(jnp.int32, (KC, TQ), 0)
    qidx = lax.broadcasted_iota(jnp.int32, (KC, TQ), 1)
    lane = lax.broadcasted_iota(jnp.int32, (KC, LANES), 1)

    def q_operand(e):
        eh = e.astype(BF16)
        hf = eh.astype(F32)
        lf = (e - hf).astype(BF16).astype(F32)
        return jnp.concatenate([(hf + pltpu.roll(lf, IDX_DIM, axis=1)).astype(BF16), eh], axis=1)

    q_ops = []
    for h in range(IDX_HEADS):
        slab = qi[:, (h // 2) * LANES:(h // 2 + 1) * LANES]
        if h % 2 == 0:
            q_ops.append(q_operand(jnp.where(lane < IDX_DIM, slab, 0.0)))
        else:
            q_ops.append(q_operand(pltpu.roll(jnp.where(lane >= IDX_DIM, slab, 0.0), IDX_DIM, axis=1)))

    def p1(j, c):
        r0 = pl.multiple_of(j * KC, KC)
        k0 = jnp.where(lane < IDX_DIM, smk_ref[pl.ds(r0, KC), :], 0.0)
        kh2 = (k0 + pltpu.roll(k0, IDX_DIM, axis=1)).astype(BF16)
        kl = (k0 - k0.astype(BF16).astype(F32)).astype(BF16)
        k_op = jnp.concatenate([kh2, kl], axis=1)
        acc = jnp.zeros((KC, TQ), F32)
        for h in range(IDX_HEADS):
            s = lax.dot_general(k_op, q_ops[h], _DIMS['nt'], preferred_element_type=F32)
            acc = acc + jnp.maximum(s, 0.0) * wT[SM_IW + h:SM_IW + h + 1, :]
        key = _f2key(acc * (IDX_DIM ** -0.5 * IDX_HEADS ** -0.5))
        key = jnp.where(kpos0 + r0 <= qidx + i * TQ, key, IMIN)
        keys_scr[j] = key
        hi_scr[j] = lax.shift_right_arithmetic(key, 16).astype(jnp.int16)
        lo_scr[j] = ((key & 0xFFFF) - HALF).astype(jnp.int16)
        return c

    lax.fori_loop(0, nk, p1, 0)

    npair = (nk + 1) // 2

    @pl.when(nk % 2 == 1)
    def _():
        hi_scr[nk] = jnp.full((KC, TQ), -HALF, jnp.int16)
        lo_scr[nk] = jnp.full((KC, TQ), -HALF, jnp.int16)
        for h in range(ATT_HEADS):
            lg_scr[h, nk] = jnp.full((KC, TQ), NEG, F32)

    def count_ge(cand):
        def body(j, part):
            return part + _fold_rows(jnp.where(keys_scr[j] >= cand, 1.0, 0.0))
        part = lax.fori_loop(0, nk, body, jnp.zeros((SUBLANES, TQ), F32))
        return jnp.sum(part, axis=0, keepdims=True)

    def count_ge16(scr):
        def count(cand):
            c16 = cand.astype(jnp.int16)

            def body(jj, part):
                for u in range(2):
                    ind = jnp.where(scr[2 * jj + u] >= c16, jnp.int16(1), jnp.int16(0))
                    for r in range(KC // PACK16):
                        part = part + ind[r * PACK16:(r + 1) * PACK16, :]
                return part
            part = lax.fori_loop(0, npair, body, jnp.zeros((PACK16, TQ), jnp.int16))
            return jnp.sum(part.astype(F32), axis=0, keepdims=True)
        return count

    total = (nk * KC).astype(F32)
    T_hi, _ = _threshold_search(count_ge16(hi_scr), (1, TQ), total, kf, 16)
    c_gt = jnp.where(T_hi >= HALF - 1, 0.0, count_ge16(hi_scr)(jnp.minimum(T_hi + 1, HALF - 1)))
    t16 = T_hi.astype(jnp.int16)

    def keep_class(j, c):
        lo_scr[j] = jnp.where(hi_scr[j] == t16, lo_scr[j], jnp.int16(-HALF))
        return c

    lax.fori_loop(0, nk, keep_class, 0)
    T_lo, c_lo = _threshold_search(count_ge16(lo_scr), (1, TQ), total, kf - c_gt, 16)
    T = T_hi * (2 * HALF) + (T_lo + HALF)
    cT = c_gt + c_lo

    ties = jnp.max(jnp.where((cT > kf) & (T > IMIN), 1.0, 0.0)) > 0.0

    @pl.when(ties)
    def _():
        need = kf - count_ge(T + 1)
        tril = jnp.where(lax.broadcasted_iota(jnp.int32, (KC, KC), 0)
                         >= lax.broadcasted_iota(jnp.int32, (KC, KC), 1), 1.0, 0.0).astype(BF16)

        def body(j, seen):
            kj = keys_scr[j]
            eq = kj == T
            pre = jnp.dot(tril, jnp.where(eq, 1.0, 0.0).astype(BF16), preferred_element_type=F32)
            keys_scr[j] = jnp.where(eq & (seen + pre > need), IMIN, kj)
            return seen + pre[KC - 1:KC, :]

        lax.fori_loop(0, nk, body, jnp.zeros((1, TQ), F32))

    Tp = jnp.maximum(T, IMIN + 1)

    cidx = lax.broadcasted_iota(jnp.int32, (1, 2 * KC), 1)
    e = jnp.where(cidx < KC, cidx, cidx - 2 * KC)
    scale = ATT_DH ** -0.5
    jprev = jnp.maximum(i - 1, 0)

    def toeplitz(r):
        y = pltpu.roll(jnp.broadcast_to(r, (KC, 2 * KC)), 0, 1, stride=1, stride_axis=0)
        return y[:, 0:TQ]

    def to_mask(j, c):
        keys_scr[j] = lax.bitcast_convert_type(jnp.where(keys_scr[j] >= Tp, 0.0, NEG), jnp.int32)
        return c

    lax.fori_loop(0, nk, to_mask, 0)

    heads = range(ATT_HEADS)
    hcols = [slice(h * ATT_DH, (h + 1) * ATT_DH) for h in heads]
    qhs = [q_ref[:, cs].astype(BF16) for cs in hcols]

    def pass_a(j, biases, ms):
        r0 = pl.multiple_of(j * KC, KC)
        madd = lax.bitcast_convert_type(keys_scr[j], F32)
        out = []
        for h in heads:
            lg = lax.dot_general(k_ref[pl.ds(r0, KC), hcols[h]], qhs[h], _DIMS['nt'], preferred_element_type=F32)
            lg = lg * scale + biases[h] + madd
            lg_scr[h, j] = lg
            out.append(jnp.maximum(ms[h], jnp.max(lg, axis=0, keepdims=True)))
        return tuple(out)

    far_bias = [rb_ref[N_BUCKETS - 1, h] for h in heads]
    ms = lax.fori_loop(0, jprev, lambda j, ms: pass_a(j, far_bias, ms),
                       tuple(jnp.full((1, TQ), NEG, F32) for _ in heads))
    ms = pass_a(jprev, [toeplitz(_bias_from_dist(jnp.maximum(KC + e, 0), rb_ref, h)) for h in heads], ms)
    ms = pass_a(i, [toeplitz(_bias_from_dist(jnp.maximum(e, 0), rb_ref, h)) for h in heads], ms)

    def pass_b(jj, carry):
        out = list(carry)
        for u in range(2):
            j = 2 * jj + u
            jv = jnp.minimum(j, vt_ref.shape[0] - 1)
            for h in heads:
                l, acc = out[h]
                p = jnp.exp(lg_scr[h, j] - ms[h])
                out[h] = (l + jnp.sum(p, axis=0, keepdims=True),
                          acc + jnp.dot(vt_ref[jv, hcols[h], :], p.astype(BF16), preferred_element_type=F32))
        return tuple(out)

    res = lax.fori_loop(0, npair, pass_b,
                        tuple((jnp.zeros((1, TQ), F32), jnp.zeros((ATT_DH, TQ), F32)) for _ in heads))
    for h in heads:
        l, acc = res[h]
        o_ref[:, hcols[h]] = (acc / l).T


def _dsa_prompt_call(rel_bias, z, kb, vt, B, L, TQ, topk):
    nq = L // TQ
    W = ATT_HEADS * ATT_DH
    kern = functools.partial(_dsa_prompt_kernel, TQ=TQ, topk=topk)
    return pl.pallas_call(
        kern,
        grid=(B, nq),
        in_specs=[pl.BlockSpec(memory_space=pltpu.SMEM),
                  pl.BlockSpec((TQ, W), lambda b, i: (b * nq + i, Z_AQ // W)),
                  pl.BlockSpec((TQ, IDX_HEADS * IDX_DIM), lambda b, i: (b * nq + i, Z_IQ // (IDX_HEADS * IDX_DIM))),
                  pl.BlockSpec((TQ, LANES), lambda b, i: (b * nq + i, Z_SM // LANES)),
                  pl.BlockSpec((L, W), lambda b, i: (b, 0)),
                  pl.BlockSpec((None, nq, W, TQ), lambda b, i: (b, 0, 0, 0)),
                  pl.BlockSpec((L, LANES), lambda b, i: (b, Z_SM // LANES))],
        out_specs=pl.BlockSpec((TQ, W), lambda b, i: (b * nq + i, 0)),
        out_shape=jax.ShapeDtypeStruct((B * L, W), F32),
        scratch_shapes=[pltpu.VMEM((nq, TQ, TQ), jnp.int32), pltpu.VMEM((nq + nq % 2, TQ, TQ), jnp.int16),
                        pltpu.VMEM((nq + nq % 2, TQ, TQ), jnp.int16),
                        pltpu.VMEM((ATT_HEADS, nq + nq % 2, TQ, TQ), F32)],
        compiler_params=_cparams(("arbitrary", "arbitrary")),
        name="dsa_prompt",
    )(rel_bias, z, z, z, kb, vt, z)


def _stack_heads(x, nh, w):
    return jnp.concatenate([x[:, h * w:(h + 1) * w] for h in range(nh)], axis=0)


def _page_map(layer, NS, G, g):
    return lambda b, p, pt: (layer, pt[(b * NS + p) * G + g], 0, 0)


def _dsa_s_index_kernel(pt_ref, qi_ref, sm_ref, *rest, NS, G, topk):
    kp_refs = rest[:G]
    keysp_ref, keysn_ref, tp_ref = rest[G:]
    p = pl.program_id(1)
    kf = float(topk)
    R = SROWS
    qs = _stack_heads(qi_ref[...], IDX_HEADS, IDX_DIM)
    wcol = _stack_heads(sm_ref[:, SM_IW:SM_IW + IDX_HEADS], IDX_HEADS, 1)

    def score_keys(kmat, dims):
        s = _mm(qs, kmat, dims, 'x3')
        t = jnp.maximum(s, 0.0) * wcol
        acc = t[0:R]
        for h in range(1, IDX_HEADS):
            acc = acc + t[h * R:(h + 1) * R]
        return _f2key(acc * (IDX_DIM ** -0.5 * IDX_HEADS ** -0.5))

    keysp_ref[p] = score_keys(jnp.concatenate([r[...] for r in kp_refs], axis=1), 'nn')

    @pl.when(p == NS - 1)
    def _():
        rowi = lax.broadcasted_iota(jnp.int32, (R, LANES), 0)
        coli = lax.broadcasted_iota(jnp.int32, (R, LANES), 1)
        knew = jnp.concatenate([sm_ref[:, SM_IK:SM_IK + IDX_DIM],
                                jnp.zeros((PAGE_SIZE - R, IDX_DIM), F32)], axis=0)
        ok = (coli >= S_LO) & (coli < S_HI) & (coli <= rowi)
        keysn_ref[...] = jnp.where(ok, score_keys(knew, 'nt'), IMIN)

        def count_ge(cand):
            a = jnp.sum(jnp.where(keysp_ref[...] >= cand[None], 1.0, 0.0), axis=0)
            b = jnp.where(keysn_ref[...] >= cand, 1.0, 0.0)
            return jnp.sum(_fold_lanes(a) + b, axis=1, keepdims=True)

        total = jnp.full((R, 1), float((NS * G + 1) * PAGE_SIZE), F32)
        T, cT = _threshold_search(count_ge, (R, 1), total, kf)
        rid = lax.broadcasted_iota(jnp.int32, (R, 1), 0)
        token_row = (rid >= S_LO) & (rid < S_HI)
        ties = jnp.max(jnp.where((cT > kf) & (T > IMIN) & token_row, 1.0, 0.0)) > 0.0

        @pl.when(ties)
        def _():
            need = kf - count_ge(T + 1)
            triu = jnp.where(lax.broadcasted_iota(jnp.int32, (LANES, LANES), 0)
                             <= lax.broadcasted_iota(jnp.int32, (LANES, LANES), 1), 1.0, 0.0).astype(BF16)

            def demote(blk, seen):
                eq = blk == T
                pre = jnp.dot(jnp.where(eq, 1.0, 0.0).astype(BF16), triu, preferred_element_type=F32)
                return jnp.where(eq & (seen + pre > need), IMIN, blk), seen + pre[:, LANES - 1:LANES]

            def body(j, seen):
                kj = keysp_ref[j]
                cols = []
                for g in range(G):
                    blk, seen = demote(kj[:, g * LANES:(g + 1) * LANES], seen)
                    cols.append(blk)
                keysp_ref[j] = jnp.concatenate(cols, axis=1)
                return seen

            seen = lax.fori_loop(0, NS, body, jnp.zeros((R, 1), F32))
            blk, _ = demote(keysn_ref[...], seen)
            keysn_ref[...] = blk

        tp_ref[...] = jnp.broadcast_to(jnp.maximum(T, IMIN + 1), (R, LANES))


def _dsa_s_index_call(page_table, z, cache_kidx, layer, DB, NP, G, topk):
    NS = NP // G
    GW = G * PAGE_SIZE
    kern = functools.partial(_dsa_s_index_kernel, NS=NS, G=G, topk=topk)
    QW = IDX_HEADS * IDX_DIM
    grid_spec = pltpu.PrefetchScalarGridSpec(
        num_scalar_prefetch=1,
        grid=(DB, NS),
        in_specs=[pl.BlockSpec((SROWS, QW), lambda b, p, pt: (b, Z_IQ // QW)),
                  pl.BlockSpec((SROWS, LANES), lambda b, p, pt: (b, Z_SM // LANES))]
                 + [pl.BlockSpec((None, None, IDX_DIM, PAGE_SIZE), _page_map(layer, NS, G, g)) for g in range(G)],
        out_specs=[pl.BlockSpec((None, NS, SROWS, GW), lambda b, p, pt: (b, 0, 0, 0)),
                   pl.BlockSpec((None, SROWS, LANES), lambda b, p, pt: (b, 0, 0)),
                   pl.BlockSpec((None, SROWS, LANES), lambda b, p, pt: (b, 0, 0))],
    )
    return pl.pallas_call(
        kern,
        grid_spec=grid_spec,
        out_shape=[jax.ShapeDtypeStruct((DB, NS, SROWS, GW), jnp.int32),
                   jax.ShapeDtypeStruct((DB, SROWS, LANES), jnp.int32),
                   jax.ShapeDtypeStruct((DB, SROWS, LANES), jnp.int32)],
        compiler_params=_cparams(("arbitrary", "arbitrary")),
        name="dsa_s_index",
    )(page_table.reshape(-1), z, z, *([cache_kidx] * G))


def _dsa_s_attend_kernel(pt_ref, rbr_ref, q_ref, keysp_ref, keysn_ref, tp_ref, *rest, NS, G, past):
    kp_refs, vp_refs = rest[:G], rest[G:2 * G]
    kn_ref, vn_ref, o_ref, m_scr, l_scr, acc_scr = rest[2 * G:]
    p = pl.program_id(1)
    R, H = SROWS, ATT_HEADS
    HR, W = H * R, H * ATT_DH
    rowi = lax.broadcasted_iota(jnp.int32, (HR, PAGE_SIZE), 0)
    coli = lax.broadcasted_iota(jnp.int32, (HR, PAGE_SIZE), 1)
    qpos = past + _imod(rowi, R) - S_LO
    rowhead = _idiv(lax.broadcasted_iota(jnp.int32, (HR, ATT_DH), 0), R)
    qa = _stack_heads(q_ref[...], H, ATT_DH)
    q_bd = jnp.concatenate([jnp.where(rowhead == h, qa, 0.0) for h in range(H)], axis=1).astype(BF16)
    Tp = tp_ref[...]
    scale = ATT_DH ** -0.5

    @pl.when(p == 0)
    def _():
        m_scr[...] = jnp.full(m_scr.shape, NEG, F32)
        l_scr[...] = jnp.zeros_like(l_scr)
        acc_scr[...] = jnp.zeros_like(acc_scr)

    def page_matrix(ref):
        return jnp.concatenate([ref[pl.ds(h, PAGE_SIZE, stride=H), :] for h in range(H)], axis=1).astype(BF16)

    def new_matrix(ref):
        return jnp.concatenate([ref[...], jnp.zeros((PAGE_SIZE - R, W), F32)], axis=0).astype(BF16)

    def process(pages, near):
        lgs = []
        for ktile, kbase, kmat, _ in pages:
            s = lax.dot_general(q_bd, kmat(), _DIMS['nt'], preferred_element_type=F32)
            madd = jnp.concatenate([jnp.where(ktile >= Tp, 0.0, NEG)] * H, axis=0)
            if near:
                bk = _t5_bucket(jnp.maximum(qpos - (kbase + coli), 0))
                bias = jnp.zeros((HR, PAGE_SIZE), F32)
                for jb in range(N_BUCKETS):
                    bias = jnp.where(bk == jb, rbr_ref[:, jb:jb + 1], bias)
            else:
                bias = rbr_ref[:, N_BUCKETS - 1:N_BUCKETS]
            lgs.append(s * scale + bias + madd)
        mx = lgs[0]
        for lg in lgs[1:]:
            mx = jnp.maximum(mx, lg)
        m_old = m_scr[...]
        m_new = jnp.maximum(m_old, jnp.max(mx, axis=1, keepdims=True))
        corr = jnp.exp(m_old - m_new)
        tot, pv = None, None
        for lg, (_, _, _, vmat) in zip(lgs, pages):
            pr = jnp.exp(lg - m_new)
            d = jnp.dot(pr.astype(BF16), vmat(), preferred_element_type=F32)
            tot = pr if tot is None else tot + pr
            pv = d if pv is None else pv + d
        l_scr[...] = l_scr[...] * corr + jnp.sum(tot, axis=1, keepdims=True)
        acc_scr[...] = acc_scr[...] * corr + pv
        m_scr[...] = m_new

    def cache_pages():
        kt = keysp_ref[...]
        return [(kt[:, g * PAGE_SIZE:(g + 1) * PAGE_SIZE], (p * G + g) * PAGE_SIZE,
                 functools.partial(page_matrix, kp_refs[g]), functools.partial(page_matrix, vp_refs[g]))
                for g in range(G)]

    @pl.when(p < NS - 1)
    def _():
        process(cache_pages(), False)

    @pl.when(p == NS - 1)
    def _():
        process(cache_pages(), True)
        process([(keysn_ref[...], past - S_LO, functools.partial(new_matrix, kn_ref),
                  functools.partial(new_matrix, vn_ref))], True)
        inv = 1.0 / l_scr[...]
        for h in range(H):
            hs = slice(h * ATT_DH, (h + 1) * ATT_DH)
            o_ref[:, hs] = acc_scr[h * R:(h + 1) * R, hs] * inv[h * R:(h + 1) * R, :]


def _dsa_s_attend_call(page_table, rbrows, z, keysp, keysn, tp, cache_k, cache_v, layer, DB, NP, G, past):
    NS = NP // G
    GW = G * PAGE_SIZE
    W = ATT_HEADS * ATT_DH
    PW = PAGE_SIZE * ATT_HEADS
    assert G * PAGE_SIZE >= MAX_DISTANCE
    kern = functools.partial(_dsa_s_attend_kernel, NS=NS, G=G, past=past)
    page_specs = [pl.BlockSpec((None, None, PW, ATT_DH), _page_map(layer, NS, G, g)) for g in range(G)]
    grid_spec = pltpu.PrefetchScalarGridSpec(
        num_scalar_prefetch=1,
        grid=(DB, NS),
        in_specs=[pl.BlockSpec((ATT_HEADS * SROWS, LANES), lambda b, p, pt: (0, 0)),
                  pl.BlockSpec((SROWS, W), lambda b, p, pt: (b, Z_AQ // W)),
                  pl.BlockSpec((None, None, SROWS, GW),
                               lambda b, p, pt: (b, p // (keysp.shape[3] // GW), 0, p % (keysp.shape[3] // GW))),
                  pl.BlockSpec((None, SROWS, LANES), lambda b, p, pt: (b, 0, 0)),
                  pl.BlockSpec((None, SROWS, LANES), lambda b, p, pt: (b, 0, 0))]
                 + page_specs + page_specs
                 + [pl.BlockSpec((SROWS, W), lambda b, p, pt: (b, Z_AK // W)),
                    pl.BlockSpec((SROWS, W), lambda b, p, pt: (b, Z_AV // W))],
        out_specs=pl.BlockSpec((SROWS, W), lambda b, p, pt: (b, 0)),
        scratch_shapes=[pltpu.VMEM((ATT_HEADS * SROWS, 1), F32),
                        pltpu.VMEM((ATT_HEADS * SROWS, 1), F32),
                        pltpu.VMEM((ATT_HEADS * SROWS, W), F32)],
    )
    return pl.pallas_call(
        kern,
        grid_spec=grid_spec,
        out_shape=jax.ShapeDtypeStruct((DB * SROWS, W), F32),
        compiler_params=_cparams(("arbitrary", "arbitrary")),
        name="dsa_s_attend",
    )(page_table.reshape(-1), rbrows, z, keysp, keysn, tp, *([cache_k] * G), *([cache_v] * G), z, z)


def _merge_kernel(oa_ref, ob_ref, oc_ref, g0_ref, g1_ref, g2_ref, x_ref, wb_ref, wo_ref, h_ref):
    acc = None
    for i, (o_ref, g_ref) in enumerate(((oa_ref, g0_ref), (ob_ref, g1_ref), (oc_ref, g2_ref))):
        br = jnp.dot(o_ref[...].astype(BF16), wb_ref[i], preferred_element_type=F32)
        term = _sigmoid(g_ref[...]) * br
        acc = term if acc is None else acc + term
    h_ref[...] = x_ref[...] + jnp.dot(acc.astype(BF16), wo_ref[...], preferred_element_type=F32)


def _merge_call(oa, ob, oc, z, x, wb, wo, layer, tm):
    M = x.shape[0]
    W = BRANCH_W
    g0 = Z_GATE // D_MODEL
    row = lambda c: (lambda i: (i, c))
    return pl.pallas_call(
        _merge_kernel,
        grid=(M // tm,),
        in_specs=[pl.BlockSpec((tm, W), row(0)), pl.BlockSpec((tm, W), row(0)), pl.BlockSpec((tm, W), row(0)),
                  pl.BlockSpec((tm, D_MODEL), row(g0)), pl.BlockSpec((tm, D_MODEL), row(g0 + 1)),
                  pl.BlockSpec((tm, D_MODEL), row(g0 + 2)),
                  pl.BlockSpec((tm, D_MODEL), row(0)),
                  pl.BlockSpec((None, N_BRANCH, W, D_MODEL), lambda i: (layer, 0, 0, 0)),
                  pl.BlockSpec((None, D_MODEL, D_MODEL), lambda i: (layer, 0, 0))],
        out_specs=pl.BlockSpec((tm, D_MODEL), row(0)),
        out_shape=jax.ShapeDtypeStruct((M, D_MODEL), F32),
        compiler_params=_cparams(("parallel",)),
        name="merge",
    )(oa, ob, oc, z, z, z, x, wb, wo)


def _ffn_down_kernel(a_ref, h_ref, cw_ref, wd_ref, gf_ref, y_ref, prev_scr, act_scr, *, final_norm):
    t = pl.program_id(1)
    tm = a_ref.shape[0]
    FH = FFN_CHUNK

    @pl.when(t == 0)
    def _():
        prev_scr[...] = jnp.zeros_like(prev_scr)

    acc = h_ref[...]
    for c in range(D_FF // FH):
        gs = slice(c * FH, (c + 1) * FH)
        vs = slice(D_FF + c * FH, D_FF + (c + 1) * FH)
        yg, yg0 = _conv_tile(a_ref[:, gs], prev_scr[:, gs], cw_ref[:, gs], FFN_CONV)
        yv, yv0 = _conv_tile(a_ref[:, vs], prev_scr[:, vs], cw_ref[:, vs], FFN_CONV)
        act_scr[...] = (_silu(yg) * yv).astype(BF16)
        act_scr[0:2 * SUBLANES, :] = jnp.concatenate(
            [_silu(yg0) * yv0, _silu(yg[SUBLANES:2 * SUBLANES]) * yv[SUBLANES:2 * SUBLANES]], axis=0).astype(BF16)
        acc = acc + jnp.dot(act_scr[...], wd_ref[gs, :], preferred_element_type=F32)
    prev_scr[...] = a_ref[tm - SUBLANES:tm, :]
    if final_norm:
        acc = acc * lax.rsqrt(jnp.mean(acc * acc, axis=-1, keepdims=True) + EPS) * gf_ref[...]
    y_ref[...] = acc


def _ffn_down_call(a, h, conv_w, wd, gf, layer, n_outer, tm, final_norm):
    M = h.shape[0]
    nt = M // (n_outer * tm)
    kern = functools.partial(_ffn_down_kernel, final_norm=final_norm)
    return pl.pallas_call(
        kern,
        grid=(n_outer, nt),
        in_specs=[pl.BlockSpec((tm, 2 * D_FF), lambda b, t: (b * nt + t, 0)),
                  pl.BlockSpec((tm, D_MODEL), lambda b, t: (b * nt + t, 0)),
                  pl.BlockSpec((None, FFN_CONV, 2 * D_FF), lambda b, t: (layer, 0, 0)),
                  pl.BlockSpec((None, D_FF, D_MODEL), lambda b, t: (layer, 0, 0), pipeline_mode=pl.Buffered(1)),
                  pl.BlockSpec((1, D_MODEL), lambda b, t: (0, 0))],
        out_specs=pl.BlockSpec((tm, D_MODEL), lambda b, t: (b * nt + t, 0)),
        out_shape=jax.ShapeDtypeStruct((M, D_MODEL), F32),
        scratch_shapes=[pltpu.VMEM((SUBLANES, 2 * D_FF), F32),
                        pltpu.VMEM((tm, FFN_CHUNK), BF16)],
        compiler_params=_cparams(("arbitrary", "arbitrary")),
        name="ffn_down",
    )(a, h, conv_w, wd, gf.reshape(1, D_MODEL))


def _kv_cast_kernel(k_ref, v_ref, kb_ref, vt_ref):
    kb_ref[...] = k_ref[...].astype(BF16)
    vt_ref[...] = v_ref[...].T.astype(BF16)


def _kv_cast_call(z, TQ):
    M = z.shape[0]
    W = ATT_HEADS * ATT_DH
    return pl.pallas_call(
        _kv_cast_kernel,
        grid=(M // TQ,),
        in_specs=[pl.BlockSpec((TQ, W), lambda i: (i, Z_AK // W)),
                  pl.BlockSpec((TQ, W), lambda i: (i, Z_AV // W))],
        out_specs=[pl.BlockSpec((TQ, W), lambda i: (i, 0)),
                   pl.BlockSpec((None, W, TQ), lambda i: (i, 0, 0))],
        out_shape=[jax.ShapeDtypeStruct((M, W), BF16), jax.ShapeDtypeStruct((M // TQ, W, TQ), BF16)],
        compiler_params=_cparams(("parallel",)),
        name="kv_cast",
    )(z, z)


def _kv_rows_kernel(*refs, depth):
    ins, (ko_ref, vo_ref) = refs[:2 * depth], refs[2 * depth:]
    l = pl.program_id(0)
    tm = ko_ref.shape[0] // ATT_HEADS
    for d in range(depth):
        @pl.when(l == d)
        def _():
            for src, dst in ((ins[2 * d], ko_ref), (ins[2 * d + 1], vo_ref)):
                x = src[...]
                for h in range(ATT_HEADS):
                    dst[pl.ds(h, tm, stride=ATT_HEADS), :] = x[:, h * ATT_DH:(h + 1) * ATT_DH]


def _kv_rows_call(zs_per_layer, tm):
    depth = len(zs_per_layer)
    M = zs_per_layer[0].shape[0]
    W = ATT_HEADS * ATT_DH
    in_specs, args = [], []
    for d, z in enumerate(zs_per_layer):
        rows = lambda l, i, d=d: jnp.where(l == d, i, 0)
        in_specs += [pl.BlockSpec((tm, W), lambda l, i, rows=rows: (rows(l, i), Z_AK // W)),
                     pl.BlockSpec((tm, W), lambda l, i, rows=rows: (rows(l, i), Z_AV // W))]
        args += [z, z]
    out = jax.ShapeDtypeStruct((depth, M * ATT_HEADS, ATT_DH), F32)
    return pl.pallas_call(
        functools.partial(_kv_rows_kernel, depth=depth),
        grid=(depth, M // tm),
        in_specs=in_specs,
        out_specs=[pl.BlockSpec((None, tm * ATT_HEADS, ATT_DH), lambda l, i: (l, i, 0))] * 2,
        out_shape=[out, out],
        compiler_params=_cparams(("parallel", "parallel")),
        name="kv_rows",
    )(*args)


def _rope_tables(pos):
    half = RET_DK // 2
    inv = 1.0 / (ROPE_BASE ** jnp.linspace(0.0, 1.0, half, dtype=F32))
    ang = pos.astype(F32)[:, None] * inv
    cos, sin = jnp.cos(ang), jnp.sin(ang)
    return jnp.concatenate([cos, cos], axis=-1), jnp.concatenate([-sin, sin], axis=-1)


def _dn_params(a_log, dt_bias):
    hp = jnp.zeros((SUBLANES, LANES), F32)
    hp = hp.at[0, SM_DNA:SM_DNA + DN_HEADS].set(a_log.astype(F32))
    hp = hp.at[1, SM_DNA:SM_DNA + DN_HEADS].set(dt_bias.astype(F32))
    return hp


def _mix_and_ffn(x, z, oa, ob, oc, sw, l, final, n_outer, tm_merge, tm_up, tn_up, tm_down, ffn_state=None):
    h = _merge_call(oa, ob, oc, z, x, sw['wb'], sw['wo'], l, tm_merge)
    a = _rms_matmul(h, sw['norm_ffn'], sw['w_up'], l, tm_up, tn_up)
    a_raw = a
    if ffn_state is not None:
        DB = ffn_state.shape[0]
        a = a.reshape(DB, SROWS, 2 * D_FF).at[:, S_LO - (FFN_CONV - 1):S_LO].set(ffn_state)
        a = a.reshape(DB * SROWS, 2 * D_FF)
    y = _ffn_down_call(a, h, sw['ffn_conv_w'], sw['wd'], sw['norm_final'], l, n_outer, tm_down, final)
    return y, a_raw


def kernel(x_prompt, x_sample, cache_k, cache_v, cache_kidx, state_dn_conv, state_dn, state_ret,
           state_ffn_conv, page_table, norm_mix, w_in, dn_conv_w, dn_a_log, dn_dt_bias, dn_norm,
           ret_norm, rel_bias, w_branch, w_o, norm_ffn, w_up, ffn_conv_w, w_down, norm_final):
    B, S, D = x_prompt.shape
    DB, DS, _ = x_sample.shape
    depth = w_in.shape[0]
    NP = page_table.shape[1]
    past = NP * PAGE_SIZE
    n_phys = cache_k.shape[1]
    W = ATT_HEADS * ATT_DH
    assert DS == S_HI - S_LO and S % CHUNK == 0 and (DB * SROWS) % CHUNK == 0

    TL = 256 if S % 256 == 0 else CHUNK
    TQ = 256 if S % 256 == 0 else CHUNK
    tm_p = 512 if (B * S) % 512 == 0 else CHUNK
    tm_mm = 1024 if (B * S) % 1024 == 0 else tm_p
    tm_d = 512 if S % 512 == 0 else (256 if S % 256 == 0 else CHUNK)
    MS = DB * SROWS
    NG = MS // CHUNK
    seg_per = CHUNK // SROWS

    xp = x_prompt.reshape(B * S, D)
    xs = jnp.zeros((DB, SROWS, D), F32).at[:, S_LO:S_HI].set(x_sample).reshape(MS, D)
    cos_p, sin_p = _rope_tables(jnp.arange(S))
    pos_s = past + (jnp.arange(CHUNK) % SROWS) - S_LO
    cos_s, sin_s = _rope_tables(pos_s)
    ck = cache_k.reshape(depth, n_phys, PAGE_SIZE * ATT_HEADS, ATT_DH)
    cv = cache_v.reshape(depth, n_phys, PAGE_SIZE * ATT_HEADS, ATT_DH)
    ckidx_t = jnp.swapaxes(cache_kidx, 2, 3)
    rb = rel_bias.astype(F32)
    rbrows = jnp.pad(jnp.repeat(rb.T, SROWS, axis=0), ((0, 0), (0, LANES - N_BUCKETS)))
    G = next(g for g in (8, 4, 2, 1) if NP % g == 0)
    G_idx = 2 * G if NP % (2 * G) == 0 else G
    zeros_p = jnp.zeros((B, DN_HEADS, DN_DK, DN_DV), F32)
    topk_p = min(TOPK_MAX, S // 4)
    topk_s = min(TOPK_MAX, (past + DS) // 4)

    sw = dict(w_in=_prep_w_in(w_in), wb=w_branch.astype(BF16), wo=w_o.astype(BF16), w_up=w_up.astype(BF16),
              wd=w_down.astype(BF16), norm_ffn=norm_ffn, ffn_conv_w=ffn_conv_w, norm_final=norm_final)

    p_states, s_states, z_prompt = [], [], []
    for l in range(depth):
        lw = dict(dn_conv_w=dn_conv_w[l], hp=_dn_params(dn_a_log[l], dn_dt_bias[l]),
                  dn_norm=dn_norm[l].reshape(1, DN_DV), ret_norm=ret_norm[l])
        final = l == depth - 1

        z = _rms_matmul(xp, norm_mix, sw['w_in'], l, tm_mm, 1536)
        oa, dn_s = _dn_call(z, lw['dn_conv_w'], lw['hp'], lw['dn_norm'], zeros_p, B, TL, CHUNK, 0, CHUNK)
        ob, ret_s = _ret_call(z, cos_p, sin_p, lw['ret_norm'], zeros_p, B, TL, CHUNK, 0, CHUNK)
        kb, vt = _kv_cast_call(z, TQ)
        oc = _dsa_prompt_call(rb, z, kb, vt.reshape(B, S // TQ, W, TQ), B, S, TQ, topk_p)
        xp, a_up = _mix_and_ffn(xp, z, oa, ob, oc, sw, l, final, B, tm_p, tm_mm, 1408, tm_d)
        z3 = z.reshape(B, S, Z_COLS)
        z_prompt.append(z)
        p_states.append((z3[:, S - (DN_CONV - 1):, Z_DNQKV:Z_DNQKV + DN_QKV], dn_s, ret_s, None, None,
                         z3[:, :, Z_SM + SM_IK:Z_SM + SM_IK + IDX_DIM],
                         a_up.reshape(B, S, 2 * D_FF)[:, S - (FFN_CONV - 1):]))

        zs = _rms_matmul(xs, norm_mix, sw['w_in'], l, MS, 1024)
        zs3 = zs.reshape(DB, SROWS, Z_COLS)
        zs_conv = zs3.at[:, S_LO - (DN_CONV - 1):S_LO, Z_DNQKV:Z_DNQKV + DN_QKV].set(state_dn_conv[l])
        zs_conv = zs_conv.reshape(MS, Z_COLS)
        oa, dn_s = _dn_call(zs_conv, lw['dn_conv_w'], lw['hp'], lw['dn_norm'], state_dn[l], NG, CHUNK,
                            SROWS, S_LO, S_HI)
        ob, ret_s = _ret_call(zs, cos_s, sin_s, lw['ret_norm'], state_ret[l], NG, CHUNK, SROWS, S_LO, S_HI)
        keysp, keysn, tp = _dsa_s_index_call(page_table, zs, ckidx_t, l, DB, NP, G_idx, topk_s)
        oc = _dsa_s_attend_call(page_table, rbrows, zs, keysp, keysn, tp, ck, cv, l, DB, NP, G, past)
        xs, a_up = _mix_and_ffn(xs, zs, oa, ob, oc, sw, l, final, 1, MS, MS, 1408, MS,
                                ffn_state=state_ffn_conv[l])
        tok = zs3[:, S_LO:S_HI]
        s_states.append((tok[:, DS - (DN_CONV - 1):, Z_DNQKV:Z_DNQKV + DN_QKV], dn_s, ret_s,
                         tok[:, :, Z_AK:Z_AK + W].reshape(DB, DS, ATT_HEADS, ATT_DH),
                         tok[:, :, Z_AV:Z_AV + W].reshape(DB, DS, ATT_HEADS, ATT_DH),
                         tok[:, :, Z_SM + SM_IK:Z_SM + SM_IK + IDX_DIM],
                         a_up.reshape(DB, SROWS, 2 * D_FF)[:, S_HI - (FFN_CONV - 1):S_HI]))

    y_prompt = xp.reshape(B, S, D)
    y_sample = xs.reshape(DB, SROWS, D)[:, S_LO:S_HI]
    p_k, p_v = (t.reshape(depth, B, S, ATT_HEADS, ATT_DH) for t in _kv_rows_call(z_prompt, tm_p))
    stk = lambda states, i: jnp.stack([st[i] for st in states])
    p_out = [stk(p_states, i) for i in (0, 1, 2)] + [p_k, p_v] + [stk(p_states, i) for i in (5, 6)]
    return (y_prompt, y_sample) + tuple(p_out) + tuple(stk(s_states, i) for i in range(7))
```

```python
import functools
import math

import numpy as np
import jax
import jax.numpy as jnp
from jax import lax
from jax.experimental import pallas as pl
from jax.experimental.pallas import tpu as pltpu

D_MODEL = 1024
DEPTH = 2
PAST_LEN = 8192
PAGE_SIZE = 128
DN_HEADS = 4
DN_DK = 128
DN_DV = 128
DN_CONV = 4
DN_QKV = 2 * DN_HEADS * DN_DK + DN_HEADS * DN_DV
RET_HEADS = 4
RET_DK = 128
RET_DV = 128
ROPE_BASE = 10000.0
ATT_HEADS = 4
ATT_DH = 128
IDX_HEADS = 4
IDX_DIM = 64
TOPK_MAX = 256
N_BUCKETS = 32
MAX_DISTANCE = 128
N_BRANCH = 3
BRANCH_W = DN_HEADS * DN_DV
D_FF = 2816
FFN_CONV = 3
EPS = 1e-6
F32 = jnp.float32
BF16 = jnp.bfloat16
IN_SIZES = (DN_QKV, DN_HEADS * DN_DV, DN_HEADS, DN_HEADS,
            RET_HEADS * RET_DK, RET_HEADS * RET_DK, RET_HEADS * RET_DV, RET_HEADS * RET_DV,
            ATT_HEADS * ATT_DH, ATT_HEADS * ATT_DH, ATT_HEADS * ATT_DH,
            IDX_HEADS * IDX_DIM, IDX_DIM, IDX_HEADS, N_BRANCH * D_MODEL)

Z_DNQKV = 0
Z_DNZ = 1536
Z_GATE = 2048
Z_RQ, Z_RK, Z_RV, Z_RG = 5120, 5632, 6144, 6656
Z_AQ, Z_AK, Z_AV = 7168, 7680, 8192
Z_IQ = 8704
Z_SM = 8960
SM_IK, SM_IW, SM_DNB, SM_DNA = 0, 64, 68, 72
Z_COLS = 9216

LANES = 128
SUBLANES = 8
CHUNK = 128
SROWS = 8
S_LO, S_HI = 3, 7
NEG = -1e30
IMIN = -2 ** 31
HALF = 2 ** 15
PACK16 = 2 * SUBLANES
FFN_CHUNK = 2 * LANES
VMEM_LIMIT = 56 * 1024 * 1024


def _cparams(sem):
    return pltpu.CompilerParams(dimension_semantics=sem, vmem_limit_bytes=VMEM_LIMIT)


def _sigmoid(x):
    return 1.0 / (1.0 + jnp.exp(-x))


def _silu(x):
    return x * _sigmoid(x)


def _softplus(x):
    return jnp.maximum(x, 0.0) + jnp.log(1.0 + jnp.exp(-jnp.abs(x)))


_DIMS = {'nn': (((1,), (0,)), ((), ())), 'nt': (((1,), (1,)), ((), ())), 'tn': (((0,), (0,)), ((), ()))}


def _split_bf16(a, n):
    parts = []
    r = a
    for i in range(n):
        p = r.astype(BF16)
        parts.append(p)
        if i + 1 < n:
            r = r - p.astype(F32)
    return parts


def _mm(a, b, dims='nn', mode='bf16'):
    dn = _DIMS[dims]
    dg = lambda x, y: lax.dot_general(x, y, dn, preferred_element_type=F32)
    if mode == 'bf16':
        return dg(a.astype(BF16), b.astype(BF16))
    if mode == 'x3':
        ah, al = _split_bf16(a, 2)
        bh, bl = _split_bf16(b, 2)
        return dg(ah, bh) + dg(ah, bl) + dg(al, bh)
    if mode == 'l01':
        ab = a.astype(BF16)
        b1, b2, b3 = _split_bf16(b, 3)
        return dg(ab, b1) + dg(ab, b2) + dg(ab, b3)
    raise ValueError(mode)


def _rms_mm_kernel(x_ref, g_ref, w_ref, o_ref, u_ref):
    @pl.when(pl.program_id(1) == 0)
    def _():
        x = x_ref[...]
        r = lax.rsqrt(jnp.mean(x * x, axis=-1, keepdims=True) + EPS)
        u_ref[...] = (x * r * g_ref[...]).astype(u_ref.dtype)

    o_ref[...] = jnp.dot(u_ref[...], w_ref[...], preferred_element_type=F32)


def _rms_matmul(x, g, w, layer, tm, tn):
    M, K = x.shape
    N = w.shape[2]
    return pl.pallas_call(
        _rms_mm_kernel,
        grid=(M // tm, N // tn),
        in_specs=[pl.BlockSpec((tm, K), lambda i, j: (i, 0)),
                  pl.BlockSpec((None, 1, K), lambda i, j: (layer, 0, 0)),
                  pl.BlockSpec((None, K, tn), lambda i, j: (layer, 0, j))],
        out_specs=pl.BlockSpec((tm, tn), lambda i, j: (i, j)),
        out_shape=jax.ShapeDtypeStruct((M, N), F32),
        scratch_shapes=[pltpu.VMEM((tm, K), BF16)],
        compiler_params=_cparams(("parallel", "arbitrary")),
        name="rms_matmul",
    )(x, g.reshape(g.shape[0], 1, K), w)


_IN_OFFS = [0] + np.cumsum(np.array(IN_SIZES)).tolist()
_SRC_DNB, _SRC_RQ, _SRC_IK, _SRC_IW, _SRC_GATE, IN_COLS = (_IN_OFFS[2], _IN_OFFS[4], _IN_OFFS[12], _IN_OFFS[13],
                                                           _IN_OFFS[14], _IN_OFFS[15])
IN_COLS_PAD = -(-IN_COLS // LANES) * LANES


def _prep_w_in_kernel(w_ref, o_ref):
    def shifted(src, width):
        a = src // LANES * LANES
        win = -(-(src - a + width) // LANES) * LANES
        return pltpu.roll(w_ref[:, a:a + win], win - (src - a), axis=1)[:, 0:width]

    n_head = Z_GATE
    o_ref[:, 0:n_head] = w_ref[:, 0:n_head].astype(BF16)
    o_ref[:, Z_GATE:Z_GATE + N_BRANCH * D_MODEL] = shifted(_SRC_GATE, N_BRANCH * D_MODEL).astype(BF16)
    o_ref[:, Z_RQ:Z_SM] = shifted(_SRC_RQ, Z_SM - Z_RQ).astype(BF16)
    lane = lax.broadcasted_iota(jnp.int32, (w_ref.shape[0], LANES), 1)
    n_idx = IDX_DIM + IDX_HEADS
    a_ik = _SRC_IK // LANES * LANES
    idx_part = pltpu.roll(w_ref[:, a_ik:a_ik + LANES], LANES - (_SRC_IK - a_ik), axis=1)
    dn_part = pltpu.roll(w_ref[:, _SRC_DNB:_SRC_DNB + LANES], SM_DNB, axis=1)
    small = jnp.where(lane < n_idx, idx_part, jnp.where(lane < n_idx + 2 * DN_HEADS, dn_part, 0.0))
    o_ref[:, Z_SM:Z_SM + LANES] = small.astype(BF16)
    o_ref[:, Z_SM + LANES:Z_COLS] = jnp.zeros((w_ref.shape[0], Z_COLS - Z_SM - LANES), BF16)


def _prep_w_in(w_in):
    depth, K, _ = w_in.shape
    assert _SRC_DNB % LANES == 0 and _IN_OFFS[1] == Z_DNZ and _SRC_DNB == Z_GATE
    assert _SRC_IW - _SRC_IK == IDX_DIM and _SRC_GATE - _SRC_IW == IDX_HEADS
    assert (_SRC_IK % LANES) + IDX_DIM + IDX_HEADS <= LANES and SM_DNB == IDX_DIM + IDX_HEADS
    assert Z_SM - Z_RQ == _SRC_IK - _SRC_RQ and SM_DNA == SM_DNB + DN_HEADS
    tr = 256
    wp = jnp.pad(w_in, ((0, 0), (0, 0), (0, IN_COLS_PAD - IN_COLS)))
    return pl.pallas_call(
        _prep_w_in_kernel,
        grid=(depth, K // tr),
        in_specs=[pl.BlockSpec((None, tr, IN_COLS_PAD), lambda l, i: (l, i, 0))],
        out_specs=pl.BlockSpec((None, tr, Z_COLS), lambda l, i: (l, i, 0)),
        out_shape=jax.ShapeDtypeStruct((depth, K, Z_COLS), BF16),
        compiler_params=_cparams(("parallel", "parallel")),
        name="prep_w_in",
    )(wp)


def _conv_tile(x, prev8, w, width):
    y = x * w[width - 1:width, :]
    for s in range(1, width):
        y = y + pltpu.roll(x, s, axis=0) * w[width - 1 - s:width - s, :]
    x0 = x[0:SUBLANES, :]
    rid = lax.broadcasted_iota(jnp.int32, x0.shape, 0)
    y0 = x0 * w[width - 1:width, :]
    for s in range(1, width):
        xs = jnp.where(rid < s, pltpu.roll(prev8, s, axis=0), pltpu.roll(x0, s, axis=0))
        y0 = y0 + xs * w[width - 1 - s:width - s, :]
    return y, y0


def _idiv(x, n):
    assert n & (n - 1) == 0
    return lax.shift_right_arithmetic(x, jnp.int32(n.bit_length() - 1))


def _imod(x, n):
    assert n & (n - 1) == 0
    return x & jnp.int32(n - 1)


def _chunk_masks(C, seg):
    ri = lax.broadcasted_iota(jnp.int32, (C, C), 0)
    ci = lax.broadcasted_iota(jnp.int32, (C, C), 1)
    if seg == C:
        return ri >= ci, ri > ci, None
    same = _idiv(ri, seg) == _idiv(ci, seg)
    return (ri >= ci) & same, (ri > ci) & same, same


def _valid_col(C, seg, lo, hi):
    r = _imod(lax.broadcasted_iota(jnp.int32, (C, 1), 0), seg)
    return jnp.where((r >= lo) & (r < hi), 1.0, 0.0)


def _tri_inv(ms, span):
    C = ms[0].shape[0]
    eye = jnp.where(lax.broadcasted_iota(jnp.int32, (C, C), 0) == lax.broadcasted_iota(jnp.int32, (C, C), 1),
                    1.0, 0.0)
    invs = [eye - m for m in ms]
    ps = list(ms)
    n = 2
    while n < span:
        ps = [_mm(p, p, 'nn', 'x3') for p in ps]
        invs = [inv + _mm(inv, p, 'nn', 'x3') for inv, p in zip(invs, ps)]
        n *= 2
    return invs


def _state_update(S_scr, h, u, kcum, qd, kd, qk, gtot, C, seg):
    nseg = C // seg
    ws, o1s = [], []
    for sg in range(nseg):
        rs = slice(sg * seg, (sg + 1) * seg)
        S = S_scr[sg, h]
        if kcum is None:
            ws.append(u[rs])
        else:
            ws.append(u[rs] - _mm(kcum[rs], S))
        o1s.append(_mm(qd[rs], S))
    w = ws[0] if nseg == 1 else jnp.concatenate(ws, axis=0)
    o1 = o1s[0] if nseg == 1 else jnp.concatenate(o1s, axis=0)
    o = o1 + _mm(qk, w)
    rowid = lax.broadcasted_iota(jnp.int32, (C, 1), 0)
    for sg in range(nseg):
        kdm = kd if nseg == 1 else jnp.where(_idiv(rowid, seg) == sg, kd, 0.0)
        gt = jnp.exp(gtot[sg * seg:sg * seg + 1, :])
        S_scr[sg, h] = S_scr[sg, h] * gt + _mm(kdm, w, 'tn')
    return o


def _dn_kernel(qkv_ref, dz_ref, sm_ref, cw_ref, hp_ref, nrm_ref, s0_ref, o_ref, sfin_ref,
               S_scr, prev_scr, c_scr, *, C, seg, lo, hi):
    t = pl.program_id(1)
    TL = qkv_ref.shape[0]
    H, DK = DN_HEADS, DN_DK
    masked = seg != C

    @pl.when(t == 0)
    def _():
        S_scr[...] = s0_ref[...]
        prev_scr[...] = jnp.zeros_like(prev_scr)

    x = qkv_ref[...]
    y, y0 = _conv_tile(x, prev_scr[...], cw_ref[...], DN_CONV)
    c_scr[...] = _silu(y)
    c_scr[0:SUBLANES, :] = _silu(y0)
    prev_scr[...] = x[TL - SUBLANES:TL, :]

    lowm, strictm, same = _chunk_masks(C, seg)
    ltri = jnp.where(lowm, 1.0, 0.0)
    valid = _valid_col(C, seg, lo, hi) if masked else None
    span = (hi - lo) if masked else C
    a_coef = -jnp.exp(hp_ref[0:1, :])
    dtb = hp_ref[1:2, :]

    units = []
    for cidx in range(TL // C):
        r0 = cidx * C
        cc = c_scr[r0:r0 + C, :]
        sm = sm_ref[r0:r0 + C, :]
        g128 = a_coef * _softplus(sm + dtb)
        b128 = _sigmoid(sm)
        if masked:
            g128 = g128 * valid
            b128 = b128 * valid
        Gc128 = _mm(ltri, g128, 'nn', 'l01')
        if masked:
            Gt128 = _mm(jnp.where(same, 1.0, 0.0), g128, 'nn', 'l01')
        else:
            Gt128 = jnp.broadcast_to(Gc128[C - 1:C, :], Gc128.shape)
        GT = Gc128.T
        for h in range(H):
            q = cc[:, h * DK:(h + 1) * DK]
            k = cc[:, (H + h) * DK:(H + h + 1) * DK]
            v = cc[:, (2 * H + h) * DK:(2 * H + h + 1) * DK]
            q = q * lax.rsqrt(jnp.sum(q * q, axis=-1, keepdims=True) + EPS) * DK ** -0.5
            k = k * lax.rsqrt(jnp.sum(k * k, axis=-1, keepdims=True) + EPS)
            if masked:
                k = k * valid
            Gc = Gc128[:, SM_DNA + h:SM_DNA + h + 1]
            Gr = GT[SM_DNA + h:SM_DNA + h + 1, :]
            Gt = Gt128[:, SM_DNA + h:SM_DNA + h + 1]
            bc = b128[:, SM_DNB + h:SM_DNB + h + 1]
            decay = jnp.where(lowm, jnp.exp(jnp.where(lowm, Gc - Gr, 0.0)), 0.0)
            eG = jnp.exp(Gc)
            units.append(dict(r0=r0, h=h, Gt=Gt, m=jnp.where(strictm, _mm(k, k, 'nt') * decay * bc, 0.0),
                              qk=_mm(q, k, 'nt') * decay, rhs_u=v * bc, rhs_k=k * (bc * eG),
                              qd=q * eG, kd=k * jnp.exp(Gt - Gc)))
    ainvs = _tri_inv([un['m'] for un in units], span)
    for un, ainv in zip(units, ainvs):
        un['u'] = _mm(ainv, un['rhs_u'], 'nn', 'x3')
        un['kcum'] = _mm(ainv, un['rhs_k'], 'nn', 'x3')

    for un in units:
        r0, h = un['r0'], un['h']
        o = _state_update(S_scr, h, un['u'], un['kcum'], un['qd'], un['kd'], un['qk'], un['Gt'], C, seg)
        on = o * lax.rsqrt(jnp.mean(o * o, axis=-1, keepdims=True) + EPS) * nrm_ref[...]
        zg = dz_ref[r0:r0 + C, h * DN_DV:(h + 1) * DN_DV]
        o_ref[r0:r0 + C, h * DN_DV:(h + 1) * DN_DV] = on * _silu(zg)

    @pl.when(t == pl.num_programs(1) - 1)
    def _():
        sfin_ref[...] = S_scr[...]


def _dn_call(z, conv_w, hp, nrm, s0, n_outer, TL, seg, lo, hi):
    M = z.shape[0]
    nt = M // (n_outer * TL)
    nseg = CHUNK // seg
    rowmap = lambda cb: (lambda b, t: (b * nt + t, cb))
    kern = functools.partial(_dn_kernel, C=CHUNK, seg=seg, lo=lo, hi=hi)
    return pl.pallas_call(
        kern,
        grid=(n_outer, nt),
        in_specs=[pl.BlockSpec((TL, DN_QKV), rowmap(Z_DNQKV // DN_QKV)),
                  pl.BlockSpec((TL, BRANCH_W), rowmap(Z_DNZ // BRANCH_W)),
                  pl.BlockSpec((TL, LANES), rowmap(Z_SM // LANES)),
                  pl.BlockSpec((DN_CONV, DN_QKV), lambda b, t: (0, 0)),
                  pl.BlockSpec((SUBLANES, LANES), lambda b, t: (0, 0)),
                  pl.BlockSpec((1, DN_DV), lambda b, t: (0, 0)),
                  pl.BlockSpec((nseg, DN_HEADS, DN_DK, DN_DV), lambda b, t: (b, 0, 0, 0))],
        out_specs=[pl.BlockSpec((TL, BRANCH_W), lambda b, t: (b * nt + t, 0)),
                   pl.BlockSpec((nseg, DN_HEADS, DN_DK, DN_DV), lambda b, t: (b, 0, 0, 0))],
        out_shape=[jax.ShapeDtypeStruct((M, BRANCH_W), F32),
                   jax.ShapeDtypeStruct(s0.shape, F32)],
        scratch_shapes=[pltpu.VMEM((nseg, DN_HEADS, DN_DK, DN_DV), F32),
                        pltpu.VMEM((SUBLANES, DN_QKV), F32),
                        pltpu.VMEM((TL, DN_QKV), F32)],
        compiler_params=_cparams(("arbitrary", "arbitrary")),
        name="dn",
    )(z, z, z, conv_w, hp, nrm, s0)


_LOG_GAMMA = [float(np.log1p(-np.exp2(-5.0 - h))) for h in range(RET_HEADS)]


def _ret_kernel(q_ref, k_ref, v_ref, g_ref, cos_ref, sin_ref, nrm_ref, s0_ref, o_ref, sfin_ref,
                S_scr, *, C, seg, lo, hi):
    t = pl.program_id(1)
    TL = q_ref.shape[0]
    H, DK = RET_HEADS, RET_DK
    masked = seg != C

    @pl.when(t == 0)
    def _():
        S_scr[...] = s0_ref[...]

    lowm, _, _ = _chunk_masks(C, seg)
    ri = lax.broadcasted_iota(jnp.int32, (C, 1), 0)
    ci = lax.broadcasted_iota(jnp.int32, (1, C), 1)
    if masked:
        valid = _valid_col(C, seg, lo, hi)
        cnt_c = jnp.clip(_imod(ri, seg) - lo + 1, 0, hi - lo).astype(F32)
        cnt_r = jnp.clip(_imod(ci, seg) - lo + 1, 0, hi - lo).astype(F32)
        cnt_t = float(hi - lo)
    else:
        valid = None
        cnt_c = (ri + 1).astype(F32)
        cnt_r = (ci + 1).astype(F32)
        cnt_t = float(C)

    for cidx in range(TL // C):
        r0 = cidx * C
        cosf = cos_ref[r0:r0 + C, :]
        sins = sin_ref[r0:r0 + C, :]
        for h in range(H):
            cs = slice(h * DK, (h + 1) * DK)
            q = q_ref[r0:r0 + C, cs]
            k = k_ref[r0:r0 + C, cs]
            v = v_ref[r0:r0 + C, cs]
            q = (q * cosf + pltpu.roll(q, DK // 2, axis=1) * sins) * DK ** -0.5
            k = k * cosf + pltpu.roll(k, DK // 2, axis=1) * sins
            if masked:
                v = v * valid
            lg = _LOG_GAMMA[h]
            Gc = cnt_c * lg
            decay = jnp.where(lowm, jnp.exp(jnp.where(lowm, (cnt_c - cnt_r) * lg, 0.0)), 0.0)
            qk = _mm(q, k, 'nt') * decay
            Gt = jnp.full((C, 1), cnt_t * lg, F32)
            o = _state_update(S_scr, h, v, None, q * jnp.exp(Gc), k * jnp.exp(Gt - Gc), qk, Gt, C, seg)
            mu = jnp.mean(o, axis=-1, keepdims=True)
            oc = o - mu
            var = jnp.mean(oc * oc, axis=-1, keepdims=True)
            on = oc * lax.rsqrt(var + EPS) * nrm_ref[h:h + 1, :]
            o_ref[r0:r0 + C, cs] = on * _silu(g_ref[r0:r0 + C, cs])

    @pl.when(t == pl.num_programs(1) - 1)
    def _():
        sfin_ref[...] = S_scr[...]


def _ret_call(z, cosf, sins, nrm, s0, n_outer, TL, seg, lo, hi):
    M = z.shape[0]
    nt = M // (n_outer * TL)
    nseg = CHUNK // seg
    W = RET_HEADS * RET_DK
    rowmap = lambda cb: (lambda b, t: (b * nt + t, cb))
    kern = functools.partial(_ret_kernel, C=CHUNK, seg=seg, lo=lo, hi=hi)
    return pl.pallas_call(
        kern,
        grid=(n_outer, nt),
        in_specs=[pl.BlockSpec((TL, W), rowmap(Z_RQ // W)),
                  pl.BlockSpec((TL, W), rowmap(Z_RK // W)),
                  pl.BlockSpec((TL, W), rowmap(Z_RV // W)),
                  pl.BlockSpec((TL, W), rowmap(Z_RG // W)),
                  pl.BlockSpec((TL, RET_DK), lambda b, t: (t, 0)),
                  pl.BlockSpec((TL, RET_DK), lambda b, t: (t, 0)),
                  pl.BlockSpec((RET_HEADS, RET_DV), lambda b, t: (0, 0)),
                  pl.BlockSpec((nseg, RET_HEADS, RET_DK, RET_DV), lambda b, t: (b, 0, 0, 0))],
        out_specs=[pl.BlockSpec((TL, W), lambda b, t: (b * nt + t, 0)),
                   pl.BlockSpec((nseg, RET_HEADS, RET_DK, RET_DV), lambda b, t: (b, 0, 0, 0))],
        out_shape=[jax.ShapeDtypeStruct((M, W), F32),
                   jax.ShapeDtypeStruct(s0.shape, F32)],
        scratch_shapes=[pltpu.VMEM((nseg, RET_HEADS, RET_DK, RET_DV), F32)],
        compiler_params=_cparams(("arbitrary", "arbitrary")),
        name="ret",
    )(z, z, z, z, cosf, sins, nrm, s0)


def _f2key(x):
    b = lax.bitcast_convert_type(x + 0.0, jnp.int32)
    return jnp.where(b >= 0, b, b ^ jnp.int32(0x7FFFFFFF))


def _t5_bucket(d):
    exact = N_BUCKETS // 2
    df = d.astype(F32)
    large = exact + (jnp.log(jnp.maximum(df, 1.0) / exact) / math.log(MAX_DISTANCE / exact)
                     * (N_BUCKETS - exact)).astype(jnp.int32)
    large = jnp.minimum(large, N_BUCKETS - 1)
    return jnp.where(d < exact, d, large)


def _bias_from_dist(d, rb_ref, h):
    bk = _t5_bucket(d)
    r = jnp.zeros(d.shape, F32)
    for jb in range(N_BUCKETS):
        r = jnp.where(bk == jb, rb_ref[jb, h], r)
    return r


def _threshold_search(count_ge, shape, total, kf, nbits=32):
    zero = jnp.zeros(shape, jnp.int32)
    c0 = count_ge(zero)
    ok0 = c0 >= kf
    T = jnp.where(ok0, 0, -2 ** (nbits - 1)).astype(jnp.int32)
    cT = jnp.where(ok0, c0, total)

    def body(it, carry):
        T, cT = carry
        cand = T + lax.shift_left(jnp.int32(1), jnp.int32(nbits - 2) - it)
        c = count_ge(cand)
        ok = c >= kf
        return jnp.where(ok, cand, T), jnp.where(ok, c, cT)

    return lax.fori_loop(0, nbits - 1, body, (T, cT))


def _fold_lanes(x):
    f = x[:, 0:LANES]
    for u in range(1, x.shape[1] // LANES):
        f = f + x[:, u * LANES:(u + 1) * LANES]
    return f


def _fold_rows(x):
    return jnp.sum(x.reshape(x.shape[0] // SUBLANES, SUBLANES, x.shape[1]), axis=0)


def _dsa_prompt_kernel(rb_ref, q_ref, qi_ref, smq_ref, k_ref, vt_ref, smk_ref, o_ref,
                       keys_scr, hi_scr, lo_scr, lg_scr, *, TQ, topk):
    i = pl.program_id(1)
    KC = TQ
    nk = i + 1
    kf = float(topk)
    qi = qi_ref[...]
    wT = smq_ref[...].T
    kpos0 = lax.broadcasted_iota(jnp.int32, (KC, TQ), 0)
    qidx = lax.broadcasted_iota(jnp.int32, (KC, TQ), 1)
    lane = lax.broadcasted_iota(jnp.int32, (KC, LANES), 1)

    def q_operand(e):
        eh = e.astype(BF16)
        hf = eh.astype(F32)
        lf = (e - hf).astype(BF16).astype(F32)
        return jnp.concatenate([(hf + pltpu.roll(lf, IDX_DIM, axis=1)).astype(BF16), eh], axis=1)

    q_ops = []
    for h in range(IDX_HEADS):
        slab = qi[:, (h // 2) * LANES:(h // 2 + 1) * LANES]
        if h % 2 == 0:
            q_ops.append(q_operand(jnp.where(lane < IDX_DIM, slab, 0.0)))
        else:
            q_ops.append(q_operand(pltpu.roll(jnp.where(lane >= IDX_DIM, slab, 0.0), IDX_DIM, axis=1)))

    def p1(j, diagonal):
        r0 = pl.multiple_of(j * KC, KC)
        k0 = jnp.where(lane < IDX_DIM, smk_ref[pl.ds(r0, KC), :], 0.0)
        kh2 = (k0 + pltpu.roll(k0, IDX_DIM, axis=1)).astype(BF16)
        kl = (k0 - k0.astype(BF16).astype(F32)).astype(BF16)
        k_op = jnp.concatenate([kh2, kl], axis=1)
        acc = jnp.zeros((KC, TQ), F32)
        for h in range(IDX_HEADS):
            s = lax.dot_general(k_op, q_ops[h], _DIMS['nt'], preferred_element_type=F32)
            acc = acc + jnp.maximum(s, 0.0) * wT[SM_IW + h:SM_IW + h + 1, :]
        key = _f2key(acc * (IDX_DIM ** -0.5 * IDX_HEADS ** -0.5))
        if diagonal:
            key = jnp.where(kpos0 <= qidx, key, IMIN)
        keys_scr[j] = key
        hi_scr[j] = lax.shift_right_arithmetic(key, 16).astype(jnp.int16)
        lo_scr[j] = ((key & 0xFFFF) - HALF).astype(jnp.int16)

    def p1_earlier(j, c):
        p1(j, False)
        return c

    lax.fori_loop(0, i, p1_earlier, 0)
    p1(i, True)

    npair = (nk + 1) // 2

    @pl.when(nk % 2 == 1)
    def _():
        hi_scr[nk] = jnp.full((KC, TQ), -HALF, jnp.int16)
        lo_scr[nk] = jnp.full((KC, TQ), -HALF, jnp.int16)
        for h in range(ATT_HEADS):
            lg_scr[h, nk] = jnp.full((KC, TQ), NEG, F32)

    def count_ge(cand):
        def body(j, part):
            return part + _fold_rows(jnp.where(keys_scr[j] >= cand, 1.0, 0.0))
        part = lax.fori_loop(0, nk, body, jnp.zeros((SUBLANES, TQ), F32))
        return jnp.sum(part, axis=0, keepdims=True)

    def count_ge16(scr):
        def count(cand):
            c16 = cand.astype(jnp.int16)

            def body(jj, part):
                for u in range(2):
                    ind = jnp.where(scr[2 * jj + u] >= c16, jnp.int16(1), jnp.int16(0))
                    for r in range(KC // PACK16):
                        part = part + ind[r * PACK16:(r + 1) * PACK16, :]
                return part
            part = lax.fori_loop(0, npair, body, jnp.zeros((PACK16, TQ), jnp.int16))
            return jnp.sum(part.astype(F32), axis=0, keepdims=True)
        return count

    total = (nk * KC).astype(F32)
    T_hi, _ = _threshold_search(count_ge16(hi_scr), (1, TQ), total, kf, 16)
    c_gt = jnp.where(T_hi >= HALF - 1, 0.0, count_ge16(hi_scr)(jnp.minimum(T_hi + 1, HALF - 1)))
    t16 = T_hi.astype(jnp.int16)

    def keep_class(j, c):
        lo_scr[j] = jnp.where(hi_scr[j] == t16, lo_scr[j], jnp.int16(-HALF))
        return c

    lax.fori_loop(0, nk, keep_class, 0)
    T_lo, c_lo = _threshold_search(count_ge16(lo_scr), (1, TQ), total, kf - c_gt, 16)
    T = T_hi * (2 * HALF) + (T_lo + HALF)
    cT = c_gt + c_lo

    ties = jnp.max(jnp.where((cT > kf) & (T > IMIN), 1.0, 0.0)) > 0.0

    @pl.when(ties)
    def _():
        need = kf - count_ge(T + 1)
        tril = jnp.where(lax.broadcasted_iota(jnp.int32, (KC, KC), 0)
                         >= lax.broadcasted_iota(jnp.int32, (KC, KC), 1), 1.0, 0.0).astype(BF16)

        def body(j, seen):
            kj = keys_scr[j]
            eq = kj == T
            pre = jnp.dot(tril, jnp.where(eq, 1.0, 0.0).astype(BF16), preferred_element_type=F32)
            keys_scr[j] = jnp.where(eq & (seen + pre > need), IMIN, kj)
            return seen + pre[KC - 1:KC, :]

        lax.fori_loop(0, nk, body, jnp.zeros((1, TQ), F32))

    Tp = jnp.maximum(T, IMIN + 1)

    cidx = lax.broadcasted_iota(jnp.int32, (1, 2 * KC), 1)
    e = jnp.where(cidx < KC, cidx, cidx - 2 * KC)
    scale = ATT_DH ** -0.5
    jprev = jnp.maximum(i - 1, 0)

    def toeplitz(r):
        y = pltpu.roll(jnp.broadcast_to(r, (KC, 2 * KC)), 0, 1, stride=1, stride_axis=0)
        return y[:, 0:TQ]

    def to_mask(j, c):
        keys_scr[j] = lax.bitcast_convert_type(jnp.where(keys_scr[j] >= Tp, 0.0, NEG), jnp.int32)
        return c

    lax.fori_loop(0, nk, to_mask, 0)

    heads = range(ATT_HEADS)
    hcols = [slice(h * ATT_DH, (h + 1) * ATT_DH) for h in heads]
    qhs = [q_ref[:, cs].astype(BF16) for cs in hcols]

    def pass_a(j, biases, ms):
        r0 = pl.multiple_of(j * KC, KC)
        madd = lax.bitcast_convert_type(keys_scr[j], F32)
        out = []
        for h in heads:
            lg = lax.dot_general(k_ref[pl.ds(r0, KC), hcols[h]], qhs[h], _DIMS['nt'], preferred_element_type=F32)
            lg = lg * scale + biases[h] + madd
            lg_scr[h, j] = lg
            out.append(jnp.maximum(ms[h], jnp.max(lg, axis=0, keepdims=True)))
        return tuple(out)

    far_bias = [rb_ref[N_BUCKETS - 1, h] for h in heads]
    ms = lax.fori_loop(0, jprev, lambda j, ms: pass_a(j, far_bias, ms),
                       tuple(jnp.full((1, TQ), NEG, F32) for _ in heads))
    ms = pass_a(jprev, [toeplitz(_bias_from_dist(jnp.maximum(KC + e, 0), rb_ref, h)) for h in heads], ms)
    ms = pass_a(i, [toeplitz(_bias_from_dist(jnp.maximum(e, 0), rb_ref, h)) for h in heads], ms)

    def pass_b(jj, carry):
        out = list(carry)
        for u in range(2):
            j = 2 * jj + u
            jv = jnp.minimum(j, vt_ref.shape[0] - 1)
            for h in heads:
                l, acc = out[h]
                p = jnp.exp(lg_scr[h, j] - ms[h])
                out[h] = (l + jnp.sum(p, axis=0, keepdims=True),
                          acc + jnp.dot(vt_ref[jv, hcols[h], :], p.astype(BF16), preferred_element_type=F32))
        return tuple(out)

    res = lax.fori_loop(0, npair, pass_b,
                        tuple((jnp.zeros((1, TQ), F32), jnp.zeros((ATT_DH, TQ), F32)) for _ in heads))
    for h in heads:
        l, acc = res[h]
        o_ref[:, hcols[h]] = (acc / l).T


def _dsa_prompt_call(rel_bias, z, kb, vt, B, L, TQ, topk):
    nq = L // TQ
    W = ATT_HEADS * ATT_DH
    kern = functools.partial(_dsa_prompt_kernel, TQ=TQ, topk=topk)
    return pl.pallas_call(
        kern,
        grid=(B, nq),
        in_specs=[pl.BlockSpec(memory_space=pltpu.SMEM),
                  pl.BlockSpec((TQ, W), lambda b, i: (b * nq + i, Z_AQ // W)),
                  pl.BlockSpec((TQ, IDX_HEADS * IDX_DIM), lambda b, i: (b * nq + i, Z_IQ // (IDX_HEADS * IDX_DIM))),
                  pl.BlockSpec((TQ, LANES), lambda b, i: (b * nq + i, Z_SM // LANES)),
                  pl.BlockSpec((L, W), lambda b, i: (b, 0)),
                  pl.BlockSpec((None, nq, W, TQ), lambda b, i: (b, 0, 0, 0)),
                  pl.BlockSpec((L, LANES), lambda b, i: (b, Z_SM // LANES))],
        out_specs=pl.BlockSpec((TQ, W), lambda b, i: (b * nq + i, 0)),
        out_shape=jax.ShapeDtypeStruct((B * L, W), F32),
        scratch_shapes=[pltpu.VMEM((nq, TQ, TQ), jnp.int32), pltpu.VMEM((nq + nq % 2, TQ, TQ), jnp.int16),
                        pltpu.VMEM((nq + nq % 2, TQ, TQ), jnp.int16),
                        pltpu.VMEM((ATT_HEADS, nq + nq % 2, TQ, TQ), F32)],
        compiler_params=_cparams(("arbitrary", "arbitrary")),
        name="dsa_prompt",
    )(rel_bias, z, z, z, kb, vt, z)


def _stack_heads(x, nh, w):
    return jnp.concatenate([x[:, h * w:(h + 1) * w] for h in range(nh)], axis=0)


def _page_map(layer, NS, G, g):
    return lambda b, p, pt: (layer, pt[(b * NS + p) * G + g], 0, 0)


def _dsa_s_index_kernel(pt_ref, qi_ref, sm_ref, *rest, NS, G, topk):
    kp_refs = rest[:G]
    keysp_ref, keysn_ref, tp_ref = rest[G:]
    p = pl.program_id(1)
    kf = float(topk)
    R = SROWS
    qs = _stack_heads(qi_ref[...], IDX_HEADS, IDX_DIM)
    wcol = _stack_heads(sm_ref[:, SM_IW:SM_IW + IDX_HEADS], IDX_HEADS, 1)

    def score_keys(kmat, dims):
        s = _mm(qs, kmat, dims, 'x3')
        t = jnp.maximum(s, 0.0) * wcol
        acc = t[0:R]
        for h in range(1, IDX_HEADS):
            acc = acc + t[h * R:(h + 1) * R]
        return _f2key(acc * (IDX_DIM ** -0.5 * IDX_HEADS ** -0.5))

    keysp_ref[p] = score_keys(jnp.concatenate([r[...] for r in kp_refs], axis=1), 'nn')

    @pl.when(p == NS - 1)
    def _():
        rowi = lax.broadcasted_iota(jnp.int32, (R, LANES), 0)
        coli = lax.broadcasted_iota(jnp.int32, (R, LANES), 1)
        knew = jnp.concatenate([sm_ref[:, SM_IK:SM_IK + IDX_DIM],
                                jnp.zeros((PAGE_SIZE - R, IDX_DIM), F32)], axis=0)
        ok = (coli >= S_LO) & (coli < S_HI) & (coli <= rowi)
        keysn_ref[...] = jnp.where(ok, score_keys(knew, 'nt'), IMIN)

        def count_ge(cand):
            a = jnp.sum(jnp.where(keysp_ref[...] >= cand[None], 1.0, 0.0), axis=0)
            b = jnp.where(keysn_ref[...] >= cand, 1.0, 0.0)
            return jnp.sum(_fold_lanes(a) + b, axis=1, keepdims=True)

        total = jnp.full((R, 1), float((NS * G + 1) * PAGE_SIZE), F32)
        T, cT = _threshold_search(count_ge, (R, 1), total, kf)
        rid = lax.broadcasted_iota(jnp.int32, (R, 1), 0)
        token_row = (rid >= S_LO) & (rid < S_HI)
        ties = jnp.max(jnp.where((cT > kf) & (T > IMIN) & token_row, 1.0, 0.0)) > 0.0

        @pl.when(ties)
        def _():
            need = kf - count_ge(T + 1)
            triu = jnp.where(lax.broadcasted_iota(jnp.int32, (LANES, LANES), 0)
                             <= lax.broadcasted_iota(jnp.int32, (LANES, LANES), 1), 1.0, 0.0).astype(BF16)

            def demote(blk, seen):
                eq = blk == T
                pre = jnp.dot(jnp.where(eq, 1.0, 0.0).astype(BF16), triu, preferred_element_type=F32)
                return jnp.where(eq & (seen + pre > need), IMIN, blk), seen + pre[:, LANES - 1:LANES]

            def body(j, seen):
                kj = keysp_ref[j]
                cols = []
                for g in range(G):
                    blk, seen = demote(kj[:, g * LANES:(g + 1) * LANES], seen)
                    cols.append(blk)
                keysp_ref[j] = jnp.concatenate(cols, axis=1)
                return seen

            seen = lax.fori_loop(0, NS, body, jnp.zeros((R, 1), F32))
            blk, _ = demote(keysn_ref[...], seen)
            keysn_ref[...] = blk

        tp_ref[...] = jnp.broadcast_to(jnp.maximum(T, IMIN + 1), (R, LANES))


def _dsa_s_index_call(page_table, z, cache_kidx, layer, DB, NP, G, topk):
    NS = NP // G
    GW = G * PAGE_SIZE
    kern = functools.partial(_dsa_s_index_kernel, NS=NS, G=G, topk=topk)
    QW = IDX_HEADS * IDX_DIM
    grid_spec = pltpu.PrefetchScalarGridSpec(
        num_scalar_prefetch=1,
        grid=(DB, NS),
        in_specs=[pl.BlockSpec((SROWS, QW), lambda b, p, pt: (b, Z_IQ // QW)),
                  pl.BlockSpec((SROWS, LANES), lambda b, p, pt: (b, Z_SM // LANES))]
                 + [pl.BlockSpec((None, None, IDX_DIM, PAGE_SIZE), _page_map(layer, NS, G, g)) for g in range(G)],
        out_specs=[pl.BlockSpec((None, NS, SROWS, GW), lambda b, p, pt: (b, 0, 0, 0)),
                   pl.BlockSpec((None, SROWS, LANES), lambda b, p, pt: (b, 0, 0)),
                   pl.BlockSpec((None, SROWS, LANES), lambda b, p, pt: (b, 0, 0))],
    )
    return pl.pallas_call(
        kern,
        grid_spec=grid_spec,
        out_shape=[jax.ShapeDtypeStruct((DB, NS, SROWS, GW), jnp.int32),
                   jax.ShapeDtypeStruct((DB, SROWS, LANES), jnp.int32),
                   jax.ShapeDtypeStruct((DB, SROWS, LANES), jnp.int32)],
        compiler_params=_cparams(("arbitrary", "arbitrary")),
        name="dsa_s_index",
    )(page_table.reshape(-1), z, z, *([cache_kidx] * G))


def _dsa_s_attend_kernel(pt_ref, rbr_ref, q_ref, keysp_ref, keysn_ref, tp_ref, *rest, NS, G, past):
    kp_refs, vp_refs = rest[:G], rest[G:2 * G]
    kn_ref, vn_ref, o_ref, m_scr, l_scr, acc_scr = rest[2 * G:]
    p = pl.program_id(1)
    R, H = SROWS, ATT_HEADS
    HR, W = H * R, H * ATT_DH
    rowi = lax.broadcasted_iota(jnp.int32, (HR, PAGE_SIZE), 0)
    coli = lax.broadcasted_iota(jnp.int32, (HR, PAGE_SIZE), 1)
    qpos = past + _imod(rowi, R) - S_LO
    rowhead = _idiv(lax.broadcasted_iota(jnp.int32, (HR, ATT_DH), 0), R)
    qa = _stack_heads(q_ref[...], H, ATT_DH)
    q_bd = jnp.concatenate([jnp.where(rowhead == h, qa, 0.0) for h in range(H)], axis=1).astype(BF16)
    Tp = tp_ref[...]
    scale = ATT_DH ** -0.5

    @pl.when(p == 0)
    def _():
        m_scr[...] = jnp.full(m_scr.shape, NEG, F32)
        l_scr[...] = jnp.zeros_like(l_scr)
        acc_scr[...] = jnp.zeros_like(acc_scr)

    def page_matrix(ref):
        return jnp.concatenate([ref[pl.ds(h, PAGE_SIZE, stride=H), :] for h in range(H)], axis=1).astype(BF16)

    def new_matrix(ref):
        return jnp.concatenate([ref[...], jnp.zeros((PAGE_SIZE - R, W), F32)], axis=0).astype(BF16)

    def process(pages, near):
        lgs = []
        for ktile, kbase, kmat, _ in pages:
            s = lax.dot_general(q_bd, kmat(), _DIMS['nt'], preferred_element_type=F32)
            madd = jnp.concatenate([jnp.where(ktile >= Tp, 0.0, NEG)] * H, axis=0)
            if near:
                bk = _t5_bucket(jnp.maximum(qpos - (kbase + coli), 0))
                bias = jnp.zeros((HR, PAGE_SIZE), F32)
                for jb in range(N_BUCKETS):
                    bias = jnp.where(bk == jb, rbr_ref[:, jb:jb + 1], bias)
            else:
                bias = rbr_ref[:, N_BUCKETS - 1:N_BUCKETS]
            lgs.append(s * scale + bias + madd)
        mx = lgs[0]
        for lg in lgs[1:]:
            mx = jnp.maximum(mx, lg)
        m_old = m_scr[...]
        m_new = jnp.maximum(m_old, jnp.max(mx, axis=1, keepdims=True))
        corr = jnp.exp(m_old - m_new)
        tot, pv = None, None
        for lg, (_, _, _, vmat) in zip(lgs, pages):
            pr = jnp.exp(lg - m_new)
            d = jnp.dot(pr.astype(BF16), vmat(), preferred_element_type=F32)
            tot = pr if tot is None else tot + pr
            pv = d if pv is None else pv + d
        l_scr[...] = l_scr[...] * corr + jnp.sum(tot, axis=1, keepdims=True)
        acc_scr[...] = acc_scr[...] * corr + pv
        m_scr[...] = m_new

    def cache_pages():
        kt = keysp_ref[...]
        return [(kt[:, g * PAGE_SIZE:(g + 1) * PAGE_SIZE], (p * G + g) * PAGE_SIZE,
                 functools.partial(page_matrix, kp_refs[g]), functools.partial(page_matrix, vp_refs[g]))
                for g in range(G)]

    @pl.when(p < NS - 1)
    def _():
        process(cache_pages(), False)

    @pl.when(p == NS - 1)
    def _():
        process(cache_pages(), True)
        process([(keysn_ref[...], past - S_LO, functools.partial(new_matrix, kn_ref),
                  functools.partial(new_matrix, vn_ref))], True)
        inv = 1.0 / l_scr[...]
        for h in range(H):
            hs = slice(h * ATT_DH, (h + 1) * ATT_DH)
            o_ref[:, hs] = acc_scr[h * R:(h + 1) * R, hs] * inv[h * R:(h + 1) * R, :]


def _dsa_s_attend_call(page_table, rbrows, z, keysp, keysn, tp, cache_k, cache_v, layer, DB, NP, G, past):
    NS = NP // G
    GW = G * PAGE_SIZE
    W = ATT_HEADS * ATT_DH
    PW = PAGE_SIZE * ATT_HEADS
    assert G * PAGE_SIZE >= MAX_DISTANCE
    kern = functools.partial(_dsa_s_attend_kernel, NS=NS, G=G, past=past)
    page_specs = [pl.BlockSpec((None, None, PW, ATT_DH), _page_map(layer, NS, G, g)) for g in range(G)]
    grid_spec = pltpu.PrefetchScalarGridSpec(
        num_scalar_prefetch=1,
        grid=(DB, NS),
        in_specs=[pl.BlockSpec((ATT_HEADS * SROWS, LANES), lambda b, p, pt: (0, 0)),
                  pl.BlockSpec((SROWS, W), lambda b, p, pt: (b, Z_AQ // W)),
                  pl.BlockSpec((None, None, SROWS, GW),
                               lambda b, p, pt: (b, p // (keysp.shape[3] // GW), 0, p % (keysp.shape[3] // GW))),
                  pl.BlockSpec((None, SROWS, LANES), lambda b, p, pt: (b, 0, 0)),
                  pl.BlockSpec((None, SROWS, LANES), lambda b, p, pt: (b, 0, 0))]
                 + page_specs + page_specs
                 + [pl.BlockSpec((SROWS, W), lambda b, p, pt: (b, Z_AK // W)),
                    pl.BlockSpec((SROWS, W), lambda b, p, pt: (b, Z_AV // W))],
        out_specs=pl.BlockSpec((SROWS, W), lambda b, p, pt: (b, 0)),
        scratch_shapes=[pltpu.VMEM((ATT_HEADS * SROWS, 1), F32),
                        pltpu.VMEM((ATT_HEADS * SROWS, 1), F32),
                        pltpu.VMEM((ATT_HEADS * SROWS, W), F32)],
    )
    return pl.pallas_call(
        kern,
        grid_spec=grid_spec,
        out_shape=jax.ShapeDtypeStruct((DB * SROWS, W), F32),
        compiler_params=_cparams(("arbitrary", "arbitrary")),
        name="dsa_s_attend",
    )(page_table.reshape(-1), rbrows, z, keysp, keysn, tp, *([cache_k] * G), *([cache_v] * G), z, z)


def _merge_kernel(oa_ref, ob_ref, oc_ref, g0_ref, g1_ref, g2_ref, x_ref, wb_ref, wo_ref, h_ref):
    acc = None
    for i, (o_ref, g_ref) in enumerate(((oa_ref, g0_ref), (ob_ref, g1_ref), (oc_ref, g2_ref))):
        br = jnp.dot(o_ref[...].astype(BF16), wb_ref[i], preferred_element_type=F32)
        term = _sigmoid(g_ref[...]) * br
        acc = term if acc is None else acc + term
    h_ref[...] = x_ref[...] + jnp.dot(acc.astype(BF16), wo_ref[...], preferred_element_type=F32)


def _merge_call(oa, ob, oc, z, x, wb, wo, layer, tm):
    M = x.shape[0]
    W = BRANCH_W
    g0 = Z_GATE // D_MODEL
    row = lambda c: (lambda i: (i, c))
    return pl.pallas_call(
        _merge_kernel,
        grid=(M // tm,),
        in_specs=[pl.BlockSpec((tm, W), row(0)), pl.BlockSpec((tm, W), row(0)), pl.BlockSpec((tm, W), row(0)),
                  pl.BlockSpec((tm, D_MODEL), row(g0)), pl.BlockSpec((tm, D_MODEL), row(g0 + 1)),
                  pl.BlockSpec((tm, D_MODEL), row(g0 + 2)),
                  pl.BlockSpec((tm, D_MODEL), row(0)),
                  pl.BlockSpec((None, N_BRANCH, W, D_MODEL), lambda i: (layer, 0, 0, 0)),
                  pl.BlockSpec((None, D_MODEL, D_MODEL), lambda i: (layer, 0, 0))],
        out_specs=pl.BlockSpec((tm, D_MODEL), row(0)),
        out_shape=jax.ShapeDtypeStruct((M, D_MODEL), F32),
        compiler_params=_cparams(("parallel",)),
        name="merge",
    )(oa, ob, oc, z, z, z, x, wb, wo)


def _ffn_down_kernel(a_ref, h_ref, cw_ref, wd_ref, gf_ref, y_ref, prev_scr, act_scr, *, final_norm):
    t = pl.program_id(1)
    tm = a_ref.shape[0]
    FH = FFN_CHUNK

    @pl.when(t == 0)
    def _():
        prev_scr[...] = jnp.zeros_like(prev_scr)

    acc = h_ref[...]
    for c in range(D_FF // FH):
        gs = slice(c * FH, (c + 1) * FH)
        vs = slice(D_FF + c * FH, D_FF + (c + 1) * FH)
        yg, yg0 = _conv_tile(a_ref[:, gs], prev_scr[:, gs], cw_ref[:, gs], FFN_CONV)
        yv, yv0 = _conv_tile(a_ref[:, vs], prev_scr[:, vs], cw_ref[:, vs], FFN_CONV)
        act_scr[...] = (_silu(yg) * yv).astype(BF16)
        act_scr[0:2 * SUBLANES, :] = jnp.concatenate(
            [_silu(yg0) * yv0, _silu(yg[SUBLANES:2 * SUBLANES]) * yv[SUBLANES:2 * SUBLANES]], axis=0).astype(BF16)
        acc = acc + jnp.dot(act_scr[...], wd_ref[gs, :], preferred_element_type=F32)
    prev_scr[...] = a_ref[tm - SUBLANES:tm, :]
    if final_norm:
        acc = acc * lax.rsqrt(jnp.mean(acc * acc, axis=-1, keepdims=True) + EPS) * gf_ref[...]
    y_ref[...] = acc


def _ffn_down_call(a, h, conv_w, wd, gf, layer, n_outer, tm, final_norm):
    M = h.shape[0]
    nt = M // (n_outer * tm)
    kern = functools.partial(_ffn_down_kernel, final_norm=final_norm)
    return pl.pallas_call(
        kern,
        grid=(n_outer, nt),
        in_specs=[pl.BlockSpec((tm, 2 * D_FF), lambda b, t: (b * nt + t, 0)),
                  pl.BlockSpec((tm, D_MODEL), lambda b, t: (b * nt + t, 0)),
                  pl.BlockSpec((None, FFN_CONV, 2 * D_FF), lambda b, t: (layer, 0, 0)),
                  pl.BlockSpec((None, D_FF, D_MODEL), lambda b, t: (layer, 0, 0), pipeline_mode=pl.Buffered(1)),
                  pl.BlockSpec((1, D_MODEL), lambda b, t: (0, 0))],
        out_specs=pl.BlockSpec((tm, D_MODEL), lambda b, t: (b * nt + t, 0)),
        out_shape=jax.ShapeDtypeStruct((M, D_MODEL), F32),
        scratch_shapes=[pltpu.VMEM((SUBLANES, 2 * D_FF), F32),
                        pltpu.VMEM((tm, FFN_CHUNK), BF16)],
        compiler_params=_cparams(("arbitrary", "arbitrary")),
        name="ffn_down",
    )(a, h, conv_w, wd, gf.reshape(1, D_MODEL))


def _kv_cast_kernel(k_ref, v_ref, kb_ref, vt_ref):
    kb_ref[...] = k_ref[...].astype(BF16)
    vt_ref[...] = v_ref[...].T.astype(BF16)


def _kv_cast_call(z, TQ):
    M = z.shape[0]
    W = ATT_HEADS * ATT_DH
    return pl.pallas_call(
        _kv_cast_kernel,
        grid=(M // TQ,),
        in_specs=[pl.BlockSpec((TQ, W), lambda i: (i, Z_AK // W)),
                  pl.BlockSpec((TQ, W), lambda i: (i, Z_AV // W))],
        out_specs=[pl.BlockSpec((TQ, W), lambda i: (i, 0)),
                   pl.BlockSpec((None, W, TQ), lambda i: (i, 0, 0))],
        out_shape=[jax.ShapeDtypeStruct((M, W), BF16), jax.ShapeDtypeStruct((M // TQ, W, TQ), BF16)],
        compiler_params=_cparams(("parallel",)),
        name="kv_cast",
    )(z, z)


def _kv_rows_kernel(*refs, depth):
    ins, (ko_ref, vo_ref) = refs[:2 * depth], refs[2 * depth:]
    l = pl.program_id(0)
    tm = ko_ref.shape[0] // ATT_HEADS
    for d in range(depth):
        @pl.when(l == d)
        def _():
            for src, dst in ((ins[2 * d], ko_ref), (ins[2 * d + 1], vo_ref)):
                x = src[...]
                for h in range(ATT_HEADS):
                    dst[pl.ds(h, tm, stride=ATT_HEADS), :] = x[:, h * ATT_DH:(h + 1) * ATT_DH]


def _kv_rows_call(zs_per_layer, tm):
    depth = len(zs_per_layer)
    M = zs_per_layer[0].shape[0]
    W = ATT_HEADS * ATT_DH
    in_specs, args = [], []
    for d, z in enumerate(zs_per_layer):
        rows = lambda l, i, d=d: jnp.where(l == d, i, 0)
        in_specs += [pl.BlockSpec((tm, W), lambda l, i, rows=rows: (rows(l, i), Z_AK // W)),
                     pl.BlockSpec((tm, W), lambda l, i, rows=rows: (rows(l, i), Z_AV // W))]
        args += [z, z]
    out = jax.ShapeDtypeStruct((depth, M * ATT_HEADS, ATT_DH), F32)
    return pl.pallas_call(
        functools.partial(_kv_rows_kernel, depth=depth),
        grid=(depth, M // tm),
        in_specs=in_specs,
        out_specs=[pl.BlockSpec((None, tm * ATT_HEADS, ATT_DH), lambda l, i: (l, i, 0))] * 2,
        out_shape=[out, out],
        compiler_params=_cparams(("parallel", "parallel")),
        name="kv_rows",
    )(*args)


def _rope_tables(pos):
    half = RET_DK // 2
    inv = 1.0 / (ROPE_BASE ** jnp.linspace(0.0, 1.0, half, dtype=F32))
    ang = pos.astype(F32)[:, None] * inv
    cos, sin = jnp.cos(ang), jnp.sin(ang)
    return jnp.concatenate([cos, cos], axis=-1), jnp.concatenate([-sin, sin], axis=-1)


def _dn_params(a_log, dt_bias):
    hp = jnp.zeros((SUBLANES, LANES), F32)
    hp = hp.at[0, SM_DNA:SM_DNA + DN_HEADS].set(a_log.astype(F32))
    hp = hp.at[1, SM_DNA:SM_DNA + DN_HEADS].set(dt_bias.astype(F32))
    return hp


def _mix_and_ffn(x, z, oa, ob, oc, sw, l, final, n_outer, tm_merge, tm_up, tn_up, tm_down, ffn_state=None):
    h = _merge_call(oa, ob, oc, z, x, sw['wb'], sw['wo'], l, tm_merge)
    a = _rms_matmul(h, sw['norm_ffn'], sw['w_up'], l, tm_up, tn_up)
    a_raw = a
    if ffn_state is not None:
        DB = ffn_state.shape[0]
        a = a.reshape(DB, SROWS, 2 * D_FF).at[:, S_LO - (FFN_CONV - 1):S_LO].set(ffn_state)
        a = a.reshape(DB * SROWS, 2 * D_FF)
    y = _ffn_down_call(a, h, sw['ffn_conv_w'], sw['wd'], sw['norm_final'], l, n_outer, tm_down, final)
    return y, a_raw


def kernel(x_prompt, x_sample, cache_k, cache_v, cache_kidx, state_dn_conv, state_dn, state_ret,
           state_ffn_conv, page_table, norm_mix, w_in, dn_conv_w, dn_a_log, dn_dt_bias, dn_norm,
           ret_norm, rel_bias, w_branch, w_o, norm_ffn, w_up, ffn_conv_w, w_down, norm_final):
    B, S, D = x_prompt.shape
    DB, DS, _ = x_sample.shape
    depth = w_in.shape[0]
    NP = page_table.shape[1]
    past = NP * PAGE_SIZE
    n_phys = cache_k.shape[1]
    W = ATT_HEADS * ATT_DH
    assert DS == S_HI - S_LO and S % CHUNK == 0 and (DB * SROWS) % CHUNK == 0

    TL = 256 if S % 256 == 0 else CHUNK
    TQ = 256 if S % 256 == 0 else CHUNK
    tm_p = 512 if (B * S) % 512 == 0 else CHUNK
    tm_mm = 2048 if (B * S) % 2048 == 0 else tm_p
    tn_in, tn_up = (1024, 512) if tm_mm == 2048 else (1536, 1408)
    tm_d = 512 if S % 512 == 0 else (256 if S % 256 == 0 else CHUNK)
    MS = DB * SROWS
    NG = MS // CHUNK
    seg_per = CHUNK // SROWS

    xp = x_prompt.reshape(B * S, D)
    xs = jnp.zeros((DB, SROWS, D), F32).at[:, S_LO:S_HI].set(x_sample).reshape(MS, D)
    cos_p, sin_p = _rope_tables(jnp.arange(S))
    pos_s = past + (jnp.arange(CHUNK) % SROWS) - S_LO
    cos_s, sin_s = _rope_tables(pos_s)
    ck = cache_k.reshape(depth, n_phys, PAGE_SIZE * ATT_HEADS, ATT_DH)
    cv = cache_v.reshape(depth, n_phys, PAGE_SIZE * ATT_HEADS, ATT_DH)
    ckidx_t = jnp.swapaxes(cache_kidx, 2, 3)
    rb = rel_bias.astype(F32)
    rbrows = jnp.pad(jnp.repeat(rb.T, SROWS, axis=0), ((0, 0), (0, LANES - N_BUCKETS)))
    G = next(g for g in (8, 4, 2, 1) if NP % g == 0)
    G_idx = 2 * G if NP % (2 * G) == 0 else G
    zeros_p = jnp.zeros((B, DN_HEADS, DN_DK, DN_DV), F32)
    topk_p = min(TOPK_MAX, S // 4)
    topk_s = min(TOPK_MAX, (past + DS) // 4)

    sw = dict(w_in=_prep_w_in(w_in), wb=w_branch.astype(BF16), wo=w_o.astype(BF16), w_up=w_up.astype(BF16),
              wd=w_down.astype(BF16), norm_ffn=norm_ffn, ffn_conv_w=ffn_conv_w, norm_final=norm_final)

    p_states, s_states, z_prompt = [], [], []
    for l in range(depth):
        lw = dict(dn_conv_w=dn_conv_w[l], hp=_dn_params(dn_a_log[l], dn_dt_bias[l]),
                  dn_norm=dn_norm[l].reshape(1, DN_DV), ret_norm=ret_norm[l])
        final = l == depth - 1

        z = _rms_matmul(xp, norm_mix, sw['w_in'], l, tm_mm, tn_in)
        oa, dn_s = _dn_call(z, lw['dn_conv_w'], lw['hp'], lw['dn_norm'], zeros_p, B, TL, CHUNK, 0, CHUNK)
        ob, ret_s = _ret_call(z, cos_p, sin_p, lw['ret_norm'], zeros_p, B, TL, CHUNK, 0, CHUNK)
        kb, vt = _kv_cast_call(z, TQ)
        oc = _dsa_prompt_call(rb, z, kb, vt.reshape(B, S // TQ, W, TQ), B, S, TQ, topk_p)
        xp, a_up = _mix_and_ffn(xp, z, oa, ob, oc, sw, l, final, B, tm_p, tm_mm, tn_up, tm_d)
        z3 = z.reshape(B, S, Z_COLS)
        z_prompt.append(z)
        p_states.append((z3[:, S - (DN_CONV - 1):, Z_DNQKV:Z_DNQKV + DN_QKV], dn_s, ret_s, None, None,
                         z3[:, :, Z_SM + SM_IK:Z_SM + SM_IK + IDX_DIM],
                         a_up.reshape(B, S, 2 * D_FF)[:, S - (FFN_CONV - 1):]))

        zs = _rms_matmul(xs, norm_mix, sw['w_in'], l, MS, 1024)
        zs3 = zs.reshape(DB, SROWS, Z_COLS)
        zs_conv = zs3.at[:, S_LO - (DN_CONV - 1):S_LO, Z_DNQKV:Z_DNQKV + DN_QKV].set(state_dn_conv[l])
        zs_conv = zs_conv.reshape(MS, Z_COLS)
        oa, dn_s = _dn_call(zs_conv, lw['dn_conv_w'], lw['hp'], lw['dn_norm'], state_dn[l], NG, CHUNK,
                            SROWS, S_LO, S_HI)
        ob, ret_s = _ret_call(zs, cos_s, sin_s, lw['ret_norm'], state_ret[l], NG, CHUNK, SROWS, S_LO, S_HI)
        keysp, keysn, tp = _dsa_s_index_call(page_table, zs, ckidx_t, l, DB, NP, G_idx, topk_s)
        oc = _dsa_s_attend_call(page_table, rbrows, zs, keysp, keysn, tp, ck, cv, l, DB, NP, G, past)
        xs, a_up = _mix_and_ffn(xs, zs, oa, ob, oc, sw, l, final, 1, MS, MS, 1408, MS,
                                ffn_state=state_ffn_conv[l])
        tok = zs3[:, S_LO:S_HI]
        s_states.append((tok[:, DS - (DN_CONV - 1):, Z_DNQKV:Z_DNQKV + DN_QKV], dn_s, ret_s,
                         tok[:, :, Z_AK:Z_AK + W].reshape(DB, DS, ATT_HEADS, ATT_DH),
                         tok[:, :, Z_AV:Z_AV + W].reshape(DB, DS, ATT_HEADS, ATT_DH),
                         tok[:, :, Z_SM + SM_IK:Z_SM + SM_IK + IDX_DIM],
                         a_up.reshape(DB, SROWS, 2 * D_FF)[:, S_HI - (FFN_CONV - 1):S_HI]))

    y_prompt = xp.reshape(B, S, D)
    y_sample = xs.reshape(DB, SROWS, D)[:, S_LO:S_HI]
    p_k, p_v = (t.reshape(depth, B, S, ATT_HEADS, ATT_DH) for t in _kv_rows_call(z_prompt, tm_p))
    stk = lambda states, i: jnp.stack([st[i] for st in states])
    p_out = [stk(p_states, i) for i in (0, 1, 2)] + [p_k, p_v] + [stk(p_states, i) for i in (5, 6)]
    return (y_prompt, y_sample) + tuple(p_out) + tuple(stk(s_states, i) for i in range(7))
```

```python
import functools
import math

import numpy as np
import jax
import jax.numpy as jnp
from jax import lax
from jax.experimental import pallas as pl
from jax.experimental.pallas import tpu as pltpu

D_MODEL = 1024
DEPTH = 2
PAST_LEN = 8192
PAGE_SIZE = 128
DN_HEADS = 4
DN_DK = 128
DN_DV = 128
DN_CONV = 4
DN_QKV = 2 * DN_HEADS * DN_DK + DN_HEADS * DN_DV
RET_HEADS = 4
RET_DK = 128
RET_DV = 128
ROPE_BASE = 10000.0
ATT_HEADS = 4
ATT_DH = 128
IDX_HEADS = 4
IDX_DIM = 64
TOPK_MAX = 256
N_BUCKETS = 32
MAX_DISTANCE = 128
N_BRANCH = 3
BRANCH_W = DN_HEADS * DN_DV
D_FF = 2816
FFN_CONV = 3
EPS = 1e-6
F32 = jnp.float32
BF16 = jnp.bfloat16
IN_SIZES = (DN_QKV, DN_HEADS * DN_DV, DN_HEADS, DN_HEADS,
            RET_HEADS * RET_DK, RET_HEADS * RET_DK, RET_HEADS * RET_DV, RET_HEADS * RET_DV,
            ATT_HEADS * ATT_DH, ATT_HEADS * ATT_DH, ATT_HEADS * ATT_DH,
            IDX_HEADS * IDX_DIM, IDX_DIM, IDX_HEADS, N_BRANCH * D_MODEL)

Z_DNQKV = 0
Z_DNZ = 1536
Z_GATE = 2048
Z_RQ, Z_RK, Z_RV, Z_RG = 5120, 5632, 6144, 6656
Z_AQ, Z_AK, Z_AV = 7168, 7680, 8192
Z_IQ = 8704
Z_SM = 8960
SM_IK, SM_IW, SM_DNB, SM_DNA = 0, 64, 68, 72
Z_COLS = 9216

LANES = 128
SUBLANES = 8
CHUNK = 128
SROWS = 8
S_LO, S_HI = 3, 7
NEG = -1e30
IMIN = -2 ** 31
HALF = 2 ** 15
PACK16 = 2 * SUBLANES
FFN_CHUNK = 2 * LANES
VMEM_LIMIT = 56 * 1024 * 1024


def _cparams(sem):
    return pltpu.CompilerParams(dimension_semantics=sem, vmem_limit_bytes=VMEM_LIMIT)


def _sigmoid(x):
    return 1.0 / (1.0 + jnp.exp(-x))


def _silu(x):
    return x * _sigmoid(x)


def _softplus(x):
    return jnp.maximum(x, 0.0) + jnp.log(1.0 + jnp.exp(-jnp.abs(x)))


_DIMS = {'nn': (((1,), (0,)), ((), ())), 'nt': (((1,), (1,)), ((), ())), 'tn': (((0,), (0,)), ((), ()))}


def _split_bf16(a, n):
    parts = []
    r = a
    for i in range(n):
        p = r.astype(BF16)
        parts.append(p)
        if i + 1 < n:
            r = r - p.astype(F32)
    return parts


def _mm(a, b, dims='nn', mode='bf16'):
    dn = _DIMS[dims]
    dg = lambda x, y: lax.dot_general(x, y, dn, preferred_element_type=F32)
    if mode == 'bf16':
        return dg(a.astype(BF16), b.astype(BF16))
    if mode == 'x3':
        ah, al = _split_bf16(a, 2)
        bh, bl = _split_bf16(b, 2)
        return dg(ah, bh) + dg(ah, bl) + dg(al, bh)
    if mode == 'l01':
        ab = a.astype(BF16)
        b1, b2, b3 = _split_bf16(b, 3)
        return dg(ab, b1) + dg(ab, b2) + dg(ab, b3)
    raise ValueError(mode)


def _rms_mm_kernel(x_ref, g_ref, w_ref, o_ref, u_ref):
    @pl.when(pl.program_id(1) == 0)
    def _():
        x = x_ref[...]
        r = lax.rsqrt(jnp.mean(x * x, axis=-1, keepdims=True) + EPS)
        u_ref[...] = (x * r * g_ref[...]).astype(u_ref.dtype)

    o_ref[...] = jnp.dot(u_ref[...], w_ref[...], preferred_element_type=F32)


def _rms_matmul(x, g, w, layer, tm, tn):
    M, K = x.shape
    N = w.shape[2]
    return pl.pallas_call(
        _rms_mm_kernel,
        grid=(M // tm, N // tn),
        in_specs=[pl.BlockSpec((tm, K), lambda i, j: (i, 0)),
                  pl.BlockSpec((None, 1, K), lambda i, j: (layer, 0, 0)),
                  pl.BlockSpec((None, K, tn), lambda i, j: (layer, 0, j))],
        out_specs=pl.BlockSpec((tm, tn), lambda i, j: (i, j)),
        out_shape=jax.ShapeDtypeStruct((M, N), F32),
        scratch_shapes=[pltpu.VMEM((tm, K), BF16)],
        compiler_params=_cparams(("parallel", "arbitrary")),
        name="rms_matmul",
    )(x, g.reshape(g.shape[0], 1, K), w)


_IN_OFFS = [0] + np.cumsum(np.array(IN_SIZES)).tolist()
_SRC_DNB, _SRC_RQ, _SRC_IK, _SRC_IW, _SRC_GATE, IN_COLS = (_IN_OFFS[2], _IN_OFFS[4], _IN_OFFS[12], _IN_OFFS[13],
                                                           _IN_OFFS[14], _IN_OFFS[15])
IN_COLS_PAD = -(-IN_COLS // LANES) * LANES


def _prep_w_in_kernel(w_ref, o_ref):
    def shifted(src, width):
        a = src // LANES * LANES
        win = -(-(src - a + width) // LANES) * LANES
        return pltpu.roll(w_ref[:, a:a + win], win - (src - a), axis=1)[:, 0:width]

    n_head = Z_GATE
    o_ref[:, 0:n_head] = w_ref[:, 0:n_head].astype(BF16)
    o_ref[:, Z_GATE:Z_GATE + N_BRANCH * D_MODEL] = shifted(_SRC_GATE, N_BRANCH * D_MODEL).astype(BF16)
    o_ref[:, Z_RQ:Z_SM] = shifted(_SRC_RQ, Z_SM - Z_RQ).astype(BF16)
    lane = lax.broadcasted_iota(jnp.int32, (w_ref.shape[0], LANES), 1)
    n_idx = IDX_DIM + IDX_HEADS
    a_ik = _SRC_IK // LANES * LANES
    idx_part = pltpu.roll(w_ref[:, a_ik:a_ik + LANES], LANES - (_SRC_IK - a_ik), axis=1)
    dn_part = pltpu.roll(w_ref[:, _SRC_DNB:_SRC_DNB + LANES], SM_DNB, axis=1)
    small = jnp.where(lane < n_idx, idx_part, jnp.where(lane < n_idx + 2 * DN_HEADS, dn_part, 0.0))
    o_ref[:, Z_SM:Z_SM + LANES] = small.astype(BF16)
    o_ref[:, Z_SM + LANES:Z_COLS] = jnp.zeros((w_ref.shape[0], Z_COLS - Z_SM - LANES), BF16)


def _prep_w_in(w_in):
    depth, K, _ = w_in.shape
    assert _SRC_DNB % LANES == 0 and _IN_OFFS[1] == Z_DNZ and _SRC_DNB == Z_GATE
    assert _SRC_IW - _SRC_IK == IDX_DIM and _SRC_GATE - _SRC_IW == IDX_HEADS
    assert (_SRC_IK % LANES) + IDX_DIM + IDX_HEADS <= LANES and SM_DNB == IDX_DIM + IDX_HEADS
    assert Z_SM - Z_RQ == _SRC_IK - _SRC_RQ and SM_DNA == SM_DNB + DN_HEADS
    tr = 256
    wp = jnp.pad(w_in, ((0, 0), (0, 0), (0, IN_COLS_PAD - IN_COLS)))
    return pl.pallas_call(
        _prep_w_in_kernel,
        grid=(depth, K // tr),
        in_specs=[pl.BlockSpec((None, tr, IN_COLS_PAD), lambda l, i: (l, i, 0))],
        out_specs=pl.BlockSpec((None, tr, Z_COLS), lambda l, i: (l, i, 0)),
        out_shape=jax.ShapeDtypeStruct((depth, K, Z_COLS), BF16),
        compiler_params=_cparams(("parallel", "parallel")),
        name="prep_w_in",
    )(wp)


def _conv_tile(x, prev8, w, width):
    y = x * w[width - 1:width, :]
    for s in range(1, width):
        y = y + pltpu.roll(x, s, axis=0) * w[width - 1 - s:width - s, :]
    x0 = x[0:SUBLANES, :]
    rid = lax.broadcasted_iota(jnp.int32, x0.shape, 0)
    y0 = x0 * w[width - 1:width, :]
    for s in range(1, width):
        xs = jnp.where(rid < s, pltpu.roll(prev8, s, axis=0), pltpu.roll(x0, s, axis=0))
        y0 = y0 + xs * w[width - 1 - s:width - s, :]
    return y, y0


def _idiv(x, n):
    assert n & (n - 1) == 0
    return lax.shift_right_arithmetic(x, jnp.int32(n.bit_length() - 1))


def _imod(x, n):
    assert n & (n - 1) == 0
    return x & jnp.int32(n - 1)


def _chunk_masks(C, seg):
    ri = lax.broadcasted_iota(jnp.int32, (C, C), 0)
    ci = lax.broadcasted_iota(jnp.int32, (C, C), 1)
    if seg == C:
        return ri >= ci, ri > ci, None
    same = _idiv(ri, seg) == _idiv(ci, seg)
    return (ri >= ci) & same, (ri > ci) & same, same


def _valid_col(C, seg, lo, hi):
    r = _imod(lax.broadcasted_iota(jnp.int32, (C, 1), 0), seg)
    return jnp.where((r >= lo) & (r < hi), 1.0, 0.0)


def _tri_inv(ms, span):
    C = ms[0].shape[0]
    eye = jnp.where(lax.broadcasted_iota(jnp.int32, (C, C), 0) == lax.broadcasted_iota(jnp.int32, (C, C), 1),
                    1.0, 0.0)
    invs = [eye - m for m in ms]
    ps = list(ms)
    n = 2
    while n < span:
        ps = [_mm(p, p, 'nn', 'x3') for p in ps]
        invs = [inv + _mm(inv, p, 'nn', 'x3') for inv, p in zip(invs, ps)]
        n *= 2
    return invs


def _state_update(S_scr, h, u, kcum, qd, kd, qk, gtot, C, seg):
    nseg = C // seg
    ws, o1s = [], []
    for sg in range(nseg):
        rs = slice(sg * seg, (sg + 1) * seg)
        S = S_scr[sg, h]
        if kcum is None:
            ws.append(u[rs])
        else:
            ws.append(u[rs] - _mm(kcum[rs], S))
        o1s.append(_mm(qd[rs], S))
    w = ws[0] if nseg == 1 else jnp.concatenate(ws, axis=0)
    o1 = o1s[0] if nseg == 1 else jnp.concatenate(o1s, axis=0)
    o = o1 + _mm(qk, w)
    rowid = lax.broadcasted_iota(jnp.int32, (C, 1), 0)
    for sg in range(nseg):
        kdm = kd if nseg == 1 else jnp.where(_idiv(rowid, seg) == sg, kd, 0.0)
        gt = jnp.exp(gtot[sg * seg:sg * seg + 1, :])
        S_scr[sg, h] = S_scr[sg, h] * gt + _mm(kdm, w, 'tn')
    return o


def _dn_kernel(qkv_ref, dz_ref, sm_ref, cw_ref, hp_ref, nrm_ref, s0_ref, o_ref, sfin_ref,
               S_scr, prev_scr, c_scr, *, C, seg, lo, hi):
    t = pl.program_id(1)
    TL = qkv_ref.shape[0]
    H, DK = DN_HEADS, DN_DK
    masked = seg != C

    @pl.when(t == 0)
    def _():
        S_scr[...] = s0_ref[...]
        prev_scr[...] = jnp.zeros_like(prev_scr)

    x = qkv_ref[...]
    y, y0 = _conv_tile(x, prev_scr[...], cw_ref[...], DN_CONV)
    c_scr[...] = _silu(y)
    c_scr[0:SUBLANES, :] = _silu(y0)
    prev_scr[...] = x[TL - SUBLANES:TL, :]

    lowm, strictm, same = _chunk_masks(C, seg)
    ltri = jnp.where(lowm, 1.0, 0.0)
    valid = _valid_col(C, seg, lo, hi) if masked else None
    span = (hi - lo) if masked else C
    a_coef = -jnp.exp(hp_ref[0:1, :])
    dtb = hp_ref[1:2, :]

    units = []
    for cidx in range(TL // C):
        r0 = cidx * C
        cc = c_scr[r0:r0 + C, :]
        sm = sm_ref[r0:r0 + C, :]
        g128 = a_coef * _softplus(sm + dtb)
        b128 = _sigmoid(sm)
        if masked:
            g128 = g128 * valid
            b128 = b128 * valid
        Gc128 = _mm(ltri, g128, 'nn', 'l01')
        if masked:
            Gt128 = _mm(jnp.where(same, 1.0, 0.0), g128, 'nn', 'l01')
        else:
            Gt128 = jnp.broadcast_to(Gc128[C - 1:C, :], Gc128.shape)
        GT = Gc128.T
        for h in range(H):
            q = cc[:, h * DK:(h + 1) * DK]
            k = cc[:, (H + h) * DK:(H + h + 1) * DK]
            v = cc[:, (2 * H + h) * DK:(2 * H + h + 1) * DK]
            q = q * lax.rsqrt(jnp.sum(q * q, axis=-1, keepdims=True) + EPS) * DK ** -0.5
            k = k * lax.rsqrt(jnp.sum(k * k, axis=-1, keepdims=True) + EPS)
            if masked:
                k = k * valid
            Gc = Gc128[:, SM_DNA + h:SM_DNA + h + 1]
            Gr = GT[SM_DNA + h:SM_DNA + h + 1, :]
            Gt = Gt128[:, SM_DNA + h:SM_DNA + h + 1]
            bc = b128[:, SM_DNB + h:SM_DNB + h + 1]
            decay = jnp.where(lowm, jnp.exp(jnp.where(lowm, Gc - Gr, 0.0)), 0.0)
            eG = jnp.exp(Gc)
            units.append(dict(r0=r0, h=h, Gt=Gt, m=jnp.where(strictm, _mm(k, k, 'nt') * decay * bc, 0.0),
                              qk=_mm(q, k, 'nt') * decay, rhs_u=v * bc, rhs_k=k * (bc * eG),
                              qd=q * eG, kd=k * jnp.exp(Gt - Gc)))
    ainvs = _tri_inv([un['m'] for un in units], span)
    for un, ainv in zip(units, ainvs):
        un['u'] = _mm(ainv, un['rhs_u'], 'nn', 'x3')
        un['kcum'] = _mm(ainv, un['rhs_k'], 'nn', 'x3')

    for un in units:
        r0, h = un['r0'], un['h']
        o = _state_update(S_scr, h, un['u'], un['kcum'], un['qd'], un['kd'], un['qk'], un['Gt'], C, seg)
        on = o * lax.rsqrt(jnp.mean(o * o, axis=-1, keepdims=True) + EPS) * nrm_ref[...]
        zg = dz_ref[r0:r0 + C, h * DN_DV:(h + 1) * DN_DV]
        o_ref[r0:r0 + C, h * DN_DV:(h + 1) * DN_DV] = on * _silu(zg)

    @pl.when(t == pl.num_programs(1) - 1)
    def _():
        sfin_ref[...] = S_scr[...]


def _dn_call(z, conv_w, hp, nrm, s0, n_outer, TL, seg, lo, hi):
    M = z.shape[0]
    nt = M // (n_outer * TL)
    nseg = CHUNK // seg
    rowmap = lambda cb: (lambda b, t: (b * nt + t, cb))
    kern = functools.partial(_dn_kernel, C=CHUNK, seg=seg, lo=lo, hi=hi)
    return pl.pallas_call(
        kern,
        grid=(n_outer, nt),
        in_specs=[pl.BlockSpec((TL, DN_QKV), rowmap(Z_DNQKV // DN_QKV)),
                  pl.BlockSpec((TL, BRANCH_W), rowmap(Z_DNZ // BRANCH_W)),
                  pl.BlockSpec((TL, LANES), rowmap(Z_SM // LANES)),
                  pl.BlockSpec((DN_CONV, DN_QKV), lambda b, t: (0, 0)),
                  pl.BlockSpec((SUBLANES, LANES), lambda b, t: (0, 0)),
                  pl.BlockSpec((1, DN_DV), lambda b, t: (0, 0)),
                  pl.BlockSpec((nseg, DN_HEADS, DN_DK, DN_DV), lambda b, t: (b, 0, 0, 0))],
        out_specs=[pl.BlockSpec((TL, BRANCH_W), lambda b, t: (b * nt + t, 0)),
                   pl.BlockSpec((nseg, DN_HEADS, DN_DK, DN_DV), lambda b, t: (b, 0, 0, 0))],
        out_shape=[jax.ShapeDtypeStruct((M, BRANCH_W), F32),
                   jax.ShapeDtypeStruct(s0.shape, F32)],
        scratch_shapes=[pltpu.VMEM((nseg, DN_HEADS, DN_DK, DN_DV), F32),
                        pltpu.VMEM((SUBLANES, DN_QKV), F32),
                        pltpu.VMEM((TL, DN_QKV), F32)],
        compiler_params=_cparams(("arbitrary", "arbitrary")),
        name="dn",
    )(z, z, z, conv_w, hp, nrm, s0)


_LOG_GAMMA = [float(np.log1p(-np.exp2(-5.0 - h))) for h in range(RET_HEADS)]


def _ret_kernel(q_ref, k_ref, v_ref, g_ref, cos_ref, sin_ref, nrm_ref, s0_ref, o_ref, sfin_ref,
                S_scr, *, C, seg, lo, hi):
    t = pl.program_id(1)
    TL = q_ref.shape[0]
    H, DK = RET_HEADS, RET_DK
    masked = seg != C

    @pl.when(t == 0)
    def _():
        S_scr[...] = s0_ref[...]

    lowm, _, _ = _chunk_masks(C, seg)
    ri = lax.broadcasted_iota(jnp.int32, (C, 1), 0)
    ci = lax.broadcasted_iota(jnp.int32, (1, C), 1)
    if masked:
        valid = _valid_col(C, seg, lo, hi)
        cnt_c = jnp.clip(_imod(ri, seg) - lo + 1, 0, hi - lo).astype(F32)
        cnt_r = jnp.clip(_imod(ci, seg) - lo + 1, 0, hi - lo).astype(F32)
        cnt_t = float(hi - lo)
    else:
        valid = None
        cnt_c = (ri + 1).astype(F32)
        cnt_r = (ci + 1).astype(F32)
        cnt_t = float(C)

    for cidx in range(TL // C):
        r0 = cidx * C
        cosf = cos_ref[r0:r0 + C, :]
        sins = sin_ref[r0:r0 + C, :]
        for h in range(H):
            cs = slice(h * DK, (h + 1) * DK)
            q = q_ref[r0:r0 + C, cs]
            k = k_ref[r0:r0 + C, cs]
            v = v_ref[r0:r0 + C, cs]
            q = (q * cosf + pltpu.roll(q, DK // 2, axis=1) * sins) * DK ** -0.5
            k = k * cosf + pltpu.roll(k, DK // 2, axis=1) * sins
            if masked:
                v = v * valid
            lg = _LOG_GAMMA[h]
            Gc = cnt_c * lg
            decay = jnp.where(lowm, jnp.exp(jnp.where(lowm, (cnt_c - cnt_r) * lg, 0.0)), 0.0)
            qk = _mm(q, k, 'nt') * decay
            Gt = jnp.full((C, 1), cnt_t * lg, F32)
            o = _state_update(S_scr, h, v, None, q * jnp.exp(Gc), k * jnp.exp(Gt - Gc), qk, Gt, C, seg)
            mu = jnp.mean(o, axis=-1, keepdims=True)
            oc = o - mu
            var = jnp.mean(oc * oc, axis=-1, keepdims=True)
            on = oc * lax.rsqrt(var + EPS) * nrm_ref[h:h + 1, :]
            o_ref[r0:r0 + C, cs] = on * _silu(g_ref[r0:r0 + C, cs])

    @pl.when(t == pl.num_programs(1) - 1)
    def _():
        sfin_ref[...] = S_scr[...]


def _ret_call(z, cosf, sins, nrm, s0, n_outer, TL, seg, lo, hi):
    M = z.shape[0]
    nt = M // (n_outer * TL)
    nseg = CHUNK // seg
    W = RET_HEADS * RET_DK
    rowmap = lambda cb: (lambda b, t: (b * nt + t, cb))
    kern = functools.partial(_ret_kernel, C=CHUNK, seg=seg, lo=lo, hi=hi)
    return pl.pallas_call(
        kern,
        grid=(n_outer, nt),
        in_specs=[pl.BlockSpec((TL, W), rowmap(Z_RQ // W)),
                  pl.BlockSpec((TL, W), rowmap(Z_RK // W)),
                  pl.BlockSpec((TL, W), rowmap(Z_RV // W)),
                  pl.BlockSpec((TL, W), rowmap(Z_RG // W)),
                  pl.BlockSpec((TL, RET_DK), lambda b, t: (t, 0)),
                  pl.BlockSpec((TL, RET_DK), lambda b, t: (t, 0)),
                  pl.BlockSpec((RET_HEADS, RET_DV), lambda b, t: (0, 0)),
                  pl.BlockSpec((nseg, RET_HEADS, RET_DK, RET_DV), lambda b, t: (b, 0, 0, 0))],
        out_specs=[pl.BlockSpec((TL, W), lambda b, t: (b * nt + t, 0)),
                   pl.BlockSpec((nseg, RET_HEADS, RET_DK, RET_DV), lambda b, t: (b, 0, 0, 0))],
        out_shape=[jax.ShapeDtypeStruct((M, W), F32),
                   jax.ShapeDtypeStruct(s0.shape, F32)],
        scratch_shapes=[pltpu.VMEM((nseg, RET_HEADS, RET_DK, RET_DV), F32)],
        compiler_params=_cparams(("arbitrary", "arbitrary")),
        name="ret",
    )(z, z, z, z, cosf, sins, nrm, s0)


def _f2key(x):
    b = lax.bitcast_convert_type(x + 0.0, jnp.int32)
    return jnp.where(b >= 0, b, b ^ jnp.int32(0x7FFFFFFF))


def _t5_bucket(d):
    exact = N_BUCKETS // 2
    df = d.astype(F32)
    large = exact + (jnp.log(jnp.maximum(df, 1.0) / exact) / math.log(MAX_DISTANCE / exact)
                     * (N_BUCKETS - exact)).astype(jnp.int32)
    large = jnp.minimum(large, N_BUCKETS - 1)
    return jnp.where(d < exact, d, large)


def _bias_from_dist(d, rb_ref, h):
    bk = _t5_bucket(d)
    r = jnp.zeros(d.shape, F32)
    for jb in range(N_BUCKETS):
        r = jnp.where(bk == jb, rb_ref[jb, h], r)
    return r


def _threshold_search(count_ge, shape, total, kf, nbits=32):
    zero = jnp.zeros(shape, jnp.int32)
    c0 = count_ge(zero)
    ok0 = c0 >= kf
    T = jnp.where(ok0, 0, -2 ** (nbits - 1)).astype(jnp.int32)
    cT = jnp.where(ok0, c0, total)

    def body(it, carry):
        T, cT = carry
        cand = T + lax.shift_left(jnp.int32(1), jnp.int32(nbits - 2) - it)
        c = count_ge(cand)
        ok = c >= kf
        return jnp.where(ok, cand, T), jnp.where(ok, c, cT)

    return lax.fori_loop(0, nbits - 1, body, (T, cT))


def _fold_lanes(x):
    f = x[:, 0:LANES]
    for u in range(1, x.shape[1] // LANES):
        f = f + x[:, u * LANES:(u + 1) * LANES]
    return f


def _fold_rows(x):
    return jnp.sum(x.reshape(x.shape[0] // SUBLANES, SUBLANES, x.shape[1]), axis=0)


def _dsa_prompt_kernel(rb_ref, q_ref, qi_ref, smq_ref, k_ref, vt_ref, smk_ref, o_ref,
                       keys_scr, hi_scr, lo_scr, lg_scr, *, TQ, topk):
    i = pl.program_id(1)
    KC = TQ
    nk = i + 1
    kf = float(topk)
    qi = qi_ref[...]
    wT = smq_ref[...].T
    kpos0 = lax.broadcasted_iota(jnp.int32, (KC, TQ), 0)
    qidx = lax.broadcasted_iota(jnp.int32, (KC, TQ), 1)
    lane = lax.broadcasted_iota(jnp.int32, (KC, LANES), 1)

    def q_operand(e):
        eh = e.astype(BF16)
        hf = eh.astype(F32)
        lf = (e - hf).astype(BF16).astype(F32)
        return jnp.concatenate([(hf + pltpu.roll(lf, IDX_DIM, axis=1)).astype(BF16), eh], axis=1)

    q_ops = []
    for h in range(IDX_HEADS):
        slab = qi[:, (h // 2) * LANES:(h // 2 + 1) * LANES]
        if h % 2 == 0:
            q_ops.append(q_operand(jnp.where(lane < IDX_DIM, slab, 0.0)))
        else:
            q_ops.append(q_operand(pltpu.roll(jnp.where(lane >= IDX_DIM, slab, 0.0), IDX_DIM, axis=1)))

    def p1(j, diagonal):
        r0 = pl.multiple_of(j * KC, KC)
        k0 = jnp.where(lane < IDX_DIM, smk_ref[pl.ds(r0, KC), :], 0.0)
        kh2 = (k0 + pltpu.roll(k0, IDX_DIM, axis=1)).astype(BF16)
        kl = (k0 - k0.astype(BF16).astype(F32)).astype(BF16)
        k_op = jnp.concatenate([kh2, kl], axis=1)
        acc = jnp.zeros((KC, TQ), F32)
        for h in range(IDX_HEADS):
            s = lax.dot_general(k_op, q_ops[h], _DIMS['nt'], preferred_element_type=F32)
            acc = acc + jnp.maximum(s, 0.0) * wT[SM_IW + h:SM_IW + h + 1, :]
        key = _f2key(acc * (IDX_DIM ** -0.5 * IDX_HEADS ** -0.5))
        if diagonal:
            key = jnp.where(kpos0 <= qidx, key, IMIN)
        keys_scr[j] = key
        hi_scr[j] = lax.shift_right_arithmetic(key, 16).astype(jnp.int16)
        lo_scr[j] = ((key & 0xFFFF) - HALF).astype(jnp.int16)

    def p1_earlier(j, c):
        p1(j, False)
        return c

    lax.fori_loop(0, i, p1_earlier, 0)
    p1(i, True)

    npair = (nk + 1) // 2

    @pl.when(nk % 2 == 1)
    def _():
        hi_scr[nk] = jnp.full((KC, TQ), -HALF, jnp.int16)
        lo_scr[nk] = jnp.full((KC, TQ), -HALF, jnp.int16)
        for h in range(ATT_HEADS):
            lg_scr[h, nk] = jnp.full((KC, TQ), NEG, F32)

    def count_ge(cand):
        def body(j, part):
            return part + _fold_rows(jnp.where(keys_scr[j] >= cand, 1.0, 0.0))
        part = lax.fori_loop(0, nk, body, jnp.zeros((SUBLANES, TQ), F32))
        return jnp.sum(part, axis=0, keepdims=True)

    def count_ge16(scr):
        def count(cand):
            c16 = cand.astype(jnp.int16)

            def body(jj, part):
                for u in range(2):
                    ind = jnp.where(scr[2 * jj + u] >= c16, jnp.int16(1), jnp.int16(0))
                    for r in range(KC // PACK16):
                        part = part + ind[r * PACK16:(r + 1) * PACK16, :]
                return part
            part = lax.fori_loop(0, npair, body, jnp.zeros((PACK16, TQ), jnp.int16))
            return jnp.sum(part.astype(F32), axis=0, keepdims=True)
        return count

    total = (nk * KC).astype(F32)
    T_hi, _ = _threshold_search(count_ge16(hi_scr), (1, TQ), total, kf, 16)
    c_gt = jnp.where(T_hi >= HALF - 1, 0.0, count_ge16(hi_scr)(jnp.minimum(T_hi + 1, HALF - 1)))
    t16 = T_hi.astype(jnp.int16)

    def keep_class(j, c):
        lo_scr[j] = jnp.where(hi_scr[j] == t16, lo_scr[j], jnp.int16(-HALF))
        return c

    lax.fori_loop(0, nk, keep_class, 0)
    T_lo, c_lo = _threshold_search(count_ge16(lo_scr), (1, TQ), total, kf - c_gt, 16)
    T = T_hi * (2 * HALF) + (T_lo + HALF)
    cT = c_gt + c_lo

    ties = jnp.max(jnp.where((cT > kf) & (T > IMIN), 1.0, 0.0)) > 0.0

    @pl.when(ties)
    def _():
        need = kf - count_ge(T + 1)
        tril = jnp.where(lax.broadcasted_iota(jnp.int32, (KC, KC), 0)
                         >= lax.broadcasted_iota(jnp.int32, (KC, KC), 1), 1.0, 0.0).astype(BF16)

        def body(j, seen):
            kj = keys_scr[j]
            eq = kj == T
            pre = jnp.dot(tril, jnp.where(eq, 1.0, 0.0).astype(BF16), preferred_element_type=F32)
            keys_scr[j] = jnp.where(eq & (seen + pre > need), IMIN, kj)
            return seen + pre[KC - 1:KC, :]

        lax.fori_loop(0, nk, body, jnp.zeros((1, TQ), F32))

    Tp = jnp.maximum(T, IMIN + 1)

    cidx = lax.broadcasted_iota(jnp.int32, (1, 2 * KC), 1)
    e = jnp.where(cidx < KC, cidx, cidx - 2 * KC)
    scale = ATT_DH ** -0.5
    jprev = jnp.maximum(i - 1, 0)

    def toeplitz(r):
        y = pltpu.roll(jnp.broadcast_to(r, (KC, 2 * KC)), 0, 1, stride=1, stride_axis=0)
        return y[:, 0:TQ]

    def to_mask(j, c):
        keys_scr[j] = lax.bitcast_convert_type(jnp.where(keys_scr[j] >= Tp, 0.0, NEG), jnp.int32)
        return c

    lax.fori_loop(0, nk, to_mask, 0)

    heads = range(ATT_HEADS)
    hcols = [slice(h * ATT_DH, (h + 1) * ATT_DH) for h in heads]
    qhs = [q_ref[:, cs].astype(BF16) for cs in hcols]

    def pass_a(j, biases, ms):
        r0 = pl.multiple_of(j * KC, KC)
        madd = lax.bitcast_convert_type(keys_scr[j], F32)
        out = []
        for h in heads:
            lg = lax.dot_general(k_ref[pl.ds(r0, KC), hcols[h]], qhs[h], _DIMS['nt'], preferred_element_type=F32)
            lg = lg * scale + biases[h] + madd
            lg_scr[h, j] = lg
            out.append(jnp.maximum(ms[h], jnp.max(lg, axis=0, keepdims=True)))
        return tuple(out)

    far_bias = [rb_ref[N_BUCKETS - 1, h] for h in heads]
    ms = lax.fori_loop(0, jprev, lambda j, ms: pass_a(j, far_bias, ms),
                       tuple(jnp.full((1, TQ), NEG, F32) for _ in heads))
    ms = pass_a(jprev, [toeplitz(_bias_from_dist(jnp.maximum(KC + e, 0), rb_ref, h)) for h in heads], ms)
    ms = pass_a(i, [toeplitz(_bias_from_dist(jnp.maximum(e, 0), rb_ref, h)) for h in heads], ms)

    def pass_b(jj, carry):
        out = list(carry)
        for u in range(2):
            j = 2 * jj + u
            jv = jnp.minimum(j, vt_ref.shape[0] - 1)
            for h in heads:
                l, acc = out[h]
                p = jnp.exp(lg_scr[h, j] - ms[h])
                out[h] = (l + jnp.sum(p, axis=0, keepdims=True),
                          acc + jnp.dot(vt_ref[jv, hcols[h], :], p.astype(BF16), preferred_element_type=F32))
        return tuple(out)

    res = lax.fori_loop(0, npair, pass_b,
                        tuple((jnp.zeros((1, TQ), F32), jnp.zeros((ATT_DH, TQ), F32)) for _ in heads))
    for h in heads:
        l, acc = res[h]
        o_ref[:, hcols[h]] = (acc / l).T


def _dsa_prompt_call(rel_bias, z, kb, vt, B, L, TQ, topk):
    nq = L // TQ
    W = ATT_HEADS * ATT_DH
    kern = functools.partial(_dsa_prompt_kernel, TQ=TQ, topk=topk)
    return pl.pallas_call(
        kern,
        grid=(B, nq),
        in_specs=[pl.BlockSpec(memory_space=pltpu.SMEM),
                  pl.BlockSpec((TQ, W), lambda b, i: (b * nq + i, Z_AQ // W)),
                  pl.BlockSpec((TQ, IDX_HEADS * IDX_DIM), lambda b, i: (b * nq + i, Z_IQ // (IDX_HEADS * IDX_DIM))),
                  pl.BlockSpec((TQ, LANES), lambda b, i: (b * nq + i, Z_SM // LANES)),
                  pl.BlockSpec((L, W), lambda b, i: (b, 0)),
                  pl.BlockSpec((None, nq, W, TQ), lambda b, i: (b, 0, 0, 0)),
                  pl.BlockSpec((L, LANES), lambda b, i: (b, Z_SM // LANES))],
        out_specs=pl.BlockSpec((TQ, W), lambda b, i: (b * nq + i, 0)),
        out_shape=jax.ShapeDtypeStruct((B * L, W), F32),
        scratch_shapes=[pltpu.VMEM((nq, TQ, TQ), jnp.int32), pltpu.VMEM((nq + nq % 2, TQ, TQ), jnp.int16),
                        pltpu.VMEM((nq + nq % 2, TQ, TQ), jnp.int16),
                        pltpu.VMEM((ATT_HEADS, nq + nq % 2, TQ, TQ), F32)],
        compiler_params=_cparams(("arbitrary", "arbitrary")),
        name="dsa_prompt",
    )(rel_bias, z, z, z, kb, vt, z)


def _stack_heads(x, nh, w):
    return jnp.concatenate([x[:, h * w:(h + 1) * w] for h in range(nh)], axis=0)


def _page_map(layer, NS, G, g):
    return lambda b, p, pt: (layer, pt[(b * NS + p) * G + g], 0, 0)


def _dsa_s_index_kernel(pt_ref, qi_ref, sm_ref, *rest, NS, G, topk):
    kp_refs = rest[:G]
    keysp_ref, keysn_ref, tp_ref = rest[G:]
    p = pl.program_id(1)
    kf = float(topk)
    R = SROWS
    qs = _stack_heads(qi_ref[...], IDX_HEADS, IDX_DIM)
    wcol = _stack_heads(sm_ref[:, SM_IW:SM_IW + IDX_HEADS], IDX_HEADS, 1)

    def score_keys(kmat, dims):
        s = _mm(qs, kmat, dims, 'x3')
        t = jnp.maximum(s, 0.0) * wcol
        acc = t[0:R]
        for h in range(1, IDX_HEADS):
            acc = acc + t[h * R:(h + 1) * R]
        return _f2key(acc * (IDX_DIM ** -0.5 * IDX_HEADS ** -0.5))

    keysp_ref[p] = score_keys(jnp.concatenate([r[...] for r in kp_refs], axis=1), 'nn')

    @pl.when(p == NS - 1)
    def _():
        rowi = lax.broadcasted_iota(jnp.int32, (R, LANES), 0)
        coli = lax.broadcasted_iota(jnp.int32, (R, LANES), 1)
        knew = jnp.concatenate([sm_ref[:, SM_IK:SM_IK + IDX_DIM],
                                jnp.zeros((PAGE_SIZE - R, IDX_DIM), F32)], axis=0)
        ok = (coli >= S_LO) & (coli < S_HI) & (coli <= rowi)
        keysn_ref[...] = jnp.where(ok, score_keys(knew, 'nt'), IMIN)

        def count_ge(cand):
            a = jnp.sum(jnp.where(keysp_ref[...] >= cand[None], 1.0, 0.0), axis=0)
            b = jnp.where(keysn_ref[...] >= cand, 1.0, 0.0)
            return jnp.sum(_fold_lanes(a) + b, axis=1, keepdims=True)

        total = jnp.full((R, 1), float((NS * G + 1) * PAGE_SIZE), F32)
        T, cT = _threshold_search(count_ge, (R, 1), total, kf)
        rid = lax.broadcasted_iota(jnp.int32, (R, 1), 0)
        token_row = (rid >= S_LO) & (rid < S_HI)
        ties = jnp.max(jnp.where((cT > kf) & (T > IMIN) & token_row, 1.0, 0.0)) > 0.0

        @pl.when(ties)
        def _():
            need = kf - count_ge(T + 1)
            triu = jnp.where(lax.broadcasted_iota(jnp.int32, (LANES, LANES), 0)
                             <= lax.broadcasted_iota(jnp.int32, (LANES, LANES), 1), 1.0, 0.0).astype(BF16)

            def demote(blk, seen):
                eq = blk == T
                pre = jnp.dot(jnp.where(eq, 1.0, 0.0).astype(BF16), triu, preferred_element_type=F32)
                return jnp.where(eq & (seen + pre > need), IMIN, blk), seen + pre[:, LANES - 1:LANES]

            def body(j, seen):
                kj = keysp_ref[j]
                cols = []
                for g in range(G):
                    blk, seen = demote(kj[:, g * LANES:(g + 1) * LANES], seen)
                    cols.append(blk)
                keysp_ref[j] = jnp.concatenate(cols, axis=1)
                return seen

            seen = lax.fori_loop(0, NS, body, jnp.zeros((R, 1), F32))
            blk, _ = demote(keysn_ref[...], seen)
            keysn_ref[...] = blk

        tp_ref[...] = jnp.broadcast_to(jnp.maximum(T, IMIN + 1), (R, LANES))


def _dsa_s_index_call(page_table, z, cache_kidx, layer, DB, NP, G, topk):
    NS = NP // G
    GW = G * PAGE_SIZE
    kern = functools.partial(_dsa_s_index_kernel, NS=NS, G=G, topk=topk)
    QW = IDX_HEADS * IDX_DIM
    grid_spec = pltpu.PrefetchScalarGridSpec(
        num_scalar_prefetch=1,
        grid=(DB, NS),
        in_specs=[pl.BlockSpec((SROWS, QW), lambda b, p, pt: (b, Z_IQ // QW)),
                  pl.BlockSpec((SROWS, LANES), lambda b, p, pt: (b, Z_SM // LANES))]
                 + [pl.BlockSpec((None, None, IDX_DIM, PAGE_SIZE), _page_map(layer, NS, G, g)) for g in range(G)],
        out_specs=[pl.BlockSpec((None, NS, SROWS, GW), lambda b, p, pt: (b, 0, 0, 0)),
                   pl.BlockSpec((None, SROWS, LANES), lambda b, p, pt: (b, 0, 0)),
                   pl.BlockSpec((None, SROWS, LANES), lambda b, p, pt: (b, 0, 0))],
    )
    return pl.pallas_call(
        kern,
        grid_spec=grid_spec,
        out_shape=[jax.ShapeDtypeStruct((DB, NS, SROWS, GW), jnp.int32),
                   jax.ShapeDtypeStruct((DB, SROWS, LANES), jnp.int32),
                   jax.ShapeDtypeStruct((DB, SROWS, LANES), jnp.int32)],
        compiler_params=_cparams(("arbitrary", "arbitrary")),
        name="dsa_s_index",
    )(page_table.reshape(-1), z, z, *([cache_kidx] * G))


def _dsa_s_attend_kernel(pt_ref, rbr_ref, q_ref, keysp_ref, keysn_ref, tp_ref, *rest, NS, G, past):
    kp_refs, vp_refs = rest[:G], rest[G:2 * G]
    kn_ref, vn_ref, o_ref, m_scr, l_scr, acc_scr = rest[2 * G:]
    p = pl.program_id(1)
    R, H = SROWS, ATT_HEADS
    HR, W = H * R, H * ATT_DH
    rowi = lax.broadcasted_iota(jnp.int32, (HR, PAGE_SIZE), 0)
    coli = lax.broadcasted_iota(jnp.int32, (HR, PAGE_SIZE), 1)
    qpos = past + _imod(rowi, R) - S_LO
    rowhead = _idiv(lax.broadcasted_iota(jnp.int32, (HR, ATT_DH), 0), R)
    qa = _stack_heads(q_ref[...], H, ATT_DH)
    q_bd = jnp.concatenate([jnp.where(rowhead == h, qa, 0.0) for h in range(H)], axis=1).astype(BF16)
    Tp = tp_ref[...]
    scale = ATT_DH ** -0.5

    @pl.when(p == 0)
    def _():
        m_scr[...] = jnp.full(m_scr.shape, NEG, F32)
        l_scr[...] = jnp.zeros_like(l_scr)
        acc_scr[...] = jnp.zeros_like(acc_scr)

    def page_matrix(ref):
        return jnp.concatenate([ref[pl.ds(h, PAGE_SIZE, stride=H), :] for h in range(H)], axis=1).astype(BF16)

    def new_matrix(ref):
        return jnp.concatenate([ref[...], jnp.zeros((PAGE_SIZE - R, W), F32)], axis=0).astype(BF16)

    def process(pages, near):
        lgs = []
        for ktile, kbase, kmat, _ in pages:
            s = lax.dot_general(q_bd, kmat(), _DIMS['nt'], preferred_element_type=F32)
            madd = jnp.concatenate([jnp.where(ktile >= Tp, 0.0, NEG)] * H, axis=0)
            if near:
                bk = _t5_bucket(jnp.maximum(qpos - (kbase + coli), 0))
                bias = jnp.zeros((HR, PAGE_SIZE), F32)
                for jb in range(N_BUCKETS):
                    bias = jnp.where(bk == jb, rbr_ref[:, jb:jb + 1], bias)
            else:
                bias = rbr_ref[:, N_BUCKETS - 1:N_BUCKETS]
            lgs.append(s * scale + bias + madd)
        mx = lgs[0]
        for lg in lgs[1:]:
            mx = jnp.maximum(mx, lg)
        m_old = m_scr[...]
        m_new = jnp.maximum(m_old, jnp.max(mx, axis=1, keepdims=True))
        corr = jnp.exp(m_old - m_new)
        tot, pv = None, None
        for lg, (_, _, _, vmat) in zip(lgs, pages):
            pr = jnp.exp(lg - m_new)
            d = jnp.dot(pr.astype(BF16), vmat(), preferred_element_type=F32)
            tot = pr if tot is None else tot + pr
            pv = d if pv is None else pv + d
        l_scr[...] = l_scr[...] * corr + jnp.sum(tot, axis=1, keepdims=True)
        acc_scr[...] = acc_scr[...] * corr + pv
        m_scr[...] = m_new

    def cache_pages():
        kt = keysp_ref[...]
        return [(kt[:, g * PAGE_SIZE:(g + 1) * PAGE_SIZE], (p * G + g) * PAGE_SIZE,
                 functools.partial(page_matrix, kp_refs[g]), functools.partial(page_matrix, vp_refs[g]))
                for g in range(G)]

    @pl.when(p < NS - 1)
    def _():
        process(cache_pages(), False)

    @pl.when(p == NS - 1)
    def _():
        process(cache_pages(), True)
        process([(keysn_ref[...], past - S_LO, functools.partial(new_matrix, kn_ref),
                  functools.partial(new_matrix, vn_ref))], True)
        inv = 1.0 / l_scr[...]
        for h in range(H):
            hs = slice(h * ATT_DH, (h + 1) * ATT_DH)
            o_ref[:, hs] = acc_scr[h * R:(h + 1) * R, hs] * inv[h * R:(h + 1) * R, :]


def _dsa_s_attend_call(page_table, rbrows, z, keysp, keysn, tp, cache_k, cache_v, layer, DB, NP, G, past):
    NS = NP // G
    GW = G * PAGE_SIZE
    W = ATT_HEADS * ATT_DH
    PW = PAGE_SIZE * ATT_HEADS
    assert G * PAGE_SIZE >= MAX_DISTANCE
    kern = functools.partial(_dsa_s_attend_kernel, NS=NS, G=G, past=past)
    page_specs = [pl.BlockSpec((None, None, PW, ATT_DH), _page_map(layer, NS, G, g)) for g in range(G)]
    grid_spec = pltpu.PrefetchScalarGridSpec(
        num_scalar_prefetch=1,
        grid=(DB, NS),
        in_specs=[pl.BlockSpec((ATT_HEADS * SROWS, LANES), lambda b, p, pt: (0, 0)),
                  pl.BlockSpec((SROWS, W), lambda b, p, pt: (b, Z_AQ // W)),
                  pl.BlockSpec((None, None, SROWS, GW),
                               lambda b, p, pt: (b, p // (keysp.shape[3] // GW), 0, p % (keysp.shape[3] // GW))),
                  pl.BlockSpec((None, SROWS, LANES), lambda b, p, pt: (b, 0, 0)),
                  pl.BlockSpec((None, SROWS, LANES), lambda b, p, pt: (b, 0, 0))]
                 + page_specs + page_specs
                 + [pl.BlockSpec((SROWS, W), lambda b, p, pt: (b, Z_AK // W)),
                    pl.BlockSpec((SROWS, W), lambda b, p, pt: (b, Z_AV // W))],
        out_specs=pl.BlockSpec((SROWS, W), lambda b, p, pt: (b, 0)),
        scratch_shapes=[pltpu.VMEM((ATT_HEADS * SROWS, 1), F32),
                        pltpu.VMEM((ATT_HEADS * SROWS, 1), F32),
                        pltpu.VMEM((ATT_HEADS * SROWS, W), F32)],
    )
    return pl.pallas_call(
        kern,
        grid_spec=grid_spec,
        out_shape=jax.ShapeDtypeStruct((DB * SROWS, W), F32),
        compiler_params=_cparams(("arbitrary", "arbitrary")),
        name="dsa_s_attend",
    )(page_table.reshape(-1), rbrows, z, keysp, keysn, tp, *([cache_k] * G), *([cache_v] * G), z, z)


def _merge_kernel(oa_ref, ob_ref, oc_ref, g0_ref, g1_ref, g2_ref, x_ref, wb_ref, wo_ref, h_ref):
    acc = None
    for i, (o_ref, g_ref) in enumerate(((oa_ref, g0_ref), (ob_ref, g1_ref), (oc_ref, g2_ref))):
        br = jnp.dot(o_ref[...].astype(BF16), wb_ref[i], preferred_element_type=F32)
        term = _sigmoid(g_ref[...]) * br
        acc = term if acc is None else acc + term
    h_ref[...] = x_ref[...] + jnp.dot(acc.astype(BF16), wo_ref[...], preferred_element_type=F32)


def _merge_call(oa, ob, oc, z, x, wb, wo, layer, tm):
    M = x.shape[0]
    W = BRANCH_W
    g0 = Z_GATE // D_MODEL
    row = lambda c: (lambda i: (i, c))
    return pl.pallas_call(
        _merge_kernel,
        grid=(M // tm,),
        in_specs=[pl.BlockSpec((tm, W), row(0)), pl.BlockSpec((tm, W), row(0)), pl.BlockSpec((tm, W), row(0)),
                  pl.BlockSpec((tm, D_MODEL), row(g0)), pl.BlockSpec((tm, D_MODEL), row(g0 + 1)),
                  pl.BlockSpec((tm, D_MODEL), row(g0 + 2)),
                  pl.BlockSpec((tm, D_MODEL), row(0)),
                  pl.BlockSpec((None, N_BRANCH, W, D_MODEL), lambda i: (layer, 0, 0, 0)),
                  pl.BlockSpec((None, D_MODEL, D_MODEL), lambda i: (layer, 0, 0))],
        out_specs=pl.BlockSpec((tm, D_MODEL), row(0)),
        out_shape=jax.ShapeDtypeStruct((M, D_MODEL), F32),
        compiler_params=_cparams(("parallel",)),
        name="merge",
    )(oa, ob, oc, z, z, z, x, wb, wo)


def _ffn_down_kernel(a_ref, h_ref, cw_ref, wd_ref, gf_ref, y_ref, prev_scr, act_scr, *, final_norm):
    t = pl.program_id(1)
    tm = a_ref.shape[0]
    FH = FFN_CHUNK

    @pl.when(t == 0)
    def _():
        prev_scr[...] = jnp.zeros_like(prev_scr)

    acc = h_ref[...]
    for c in range(D_FF // FH):
        gs = slice(c * FH, (c + 1) * FH)
        vs = slice(D_FF + c * FH, D_FF + (c + 1) * FH)
        yg, yg0 = _conv_tile(a_ref[:, gs], prev_scr[:, gs], cw_ref[:, gs], FFN_CONV)
        yv, yv0 = _conv_tile(a_ref[:, vs], prev_scr[:, vs], cw_ref[:, vs], FFN_CONV)
        act_scr[...] = (_silu(yg) * yv).astype(BF16)
        act_scr[0:2 * SUBLANES, :] = jnp.concatenate(
            [_silu(yg0) * yv0, _silu(yg[SUBLANES:2 * SUBLANES]) * yv[SUBLANES:2 * SUBLANES]], axis=0).astype(BF16)
        acc = acc + jnp.dot(act_scr[...], wd_ref[gs, :], preferred_element_type=F32)
    prev_scr[...] = a_ref[tm - SUBLANES:tm, :]
    if final_norm:
        acc = acc * lax.rsqrt(jnp.mean(acc * acc, axis=-1, keepdims=True) + EPS) * gf_ref[...]
    y_ref[...] = acc


def _ffn_down_call(a, h, conv_w, wd, gf, layer, n_outer, tm, final_norm):
    M = h.shape[0]
    nt = M // (n_outer * tm)
    kern = functools.partial(_ffn_down_kernel, final_norm=final_norm)
    return pl.pallas_call(
        kern,
        grid=(n_outer, nt),
        in_specs=[pl.BlockSpec((tm, 2 * D_FF), lambda b, t: (b * nt + t, 0)),
                  pl.BlockSpec((tm, D_MODEL), lambda b, t: (b * nt + t, 0)),
                  pl.BlockSpec((None, FFN_CONV, 2 * D_FF), lambda b, t: (layer, 0, 0)),
                  pl.BlockSpec((None, D_FF, D_MODEL), lambda b, t: (layer, 0, 0), pipeline_mode=pl.Buffered(1)),
                  pl.BlockSpec((1, D_MODEL), lambda b, t: (0, 0))],
        out_specs=pl.BlockSpec((tm, D_MODEL), lambda b, t: (b * nt + t, 0)),
        out_shape=jax.ShapeDtypeStruct((M, D_MODEL), F32),
        scratch_shapes=[pltpu.VMEM((SUBLANES, 2 * D_FF), F32),
                        pltpu.VMEM((tm, FFN_CHUNK), BF16)],
        compiler_params=_cparams(("arbitrary", "arbitrary")),
        name="ffn_down",
    )(a, h, conv_w, wd, gf.reshape(1, D_MODEL))


def _kv_cast_kernel(k_ref, v_ref, kb_ref, vt_ref):
    kb_ref[...] = k_ref[...].astype(BF16)
    vt_ref[...] = v_ref[...].T.astype(BF16)


def _kv_cast_call(z, TQ):
    M = z.shape[0]
    W = ATT_HEADS * ATT_DH
    return pl.pallas_call(
        _kv_cast_kernel,
        grid=(M // TQ,),
        in_specs=[pl.BlockSpec((TQ, W), lambda i: (i, Z_AK // W)),
                  pl.BlockSpec((TQ, W), lambda i: (i, Z_AV // W))],
        out_specs=[pl.BlockSpec((TQ, W), lambda i: (i, 0)),
                   pl.BlockSpec((None, W, TQ), lambda i: (i, 0, 0))],
        out_shape=[jax.ShapeDtypeStruct((M, W), BF16), jax.ShapeDtypeStruct((M // TQ, W, TQ), BF16)],
        compiler_params=_cparams(("parallel",)),
        name="kv_cast",
    )(z, z)


def _kv_rows_kernel(*refs, depth):
    ins, (ko_ref, vo_ref) = refs[:2 * depth], refs[2 * depth:]
    l = pl.program_id(0)
    tm = ko_ref.shape[0] // ATT_HEADS
    for d in range(depth):
        @pl.when(l == d)
        def _():
            for src, dst in ((ins[2 * d], ko_ref), (ins[2 * d + 1], vo_ref)):
                x = src[...]
                for h in range(ATT_HEADS):
                    dst[pl.ds(h, tm, stride=ATT_HEADS), :] = x[:, h * ATT_DH:(h + 1) * ATT_DH]


def _kv_rows_call(zs_per_layer, tm):
    depth = len(zs_per_layer)
    M = zs_per_layer[0].shape[0]
    W = ATT_HEADS * ATT_DH
    in_specs, args = [], []
    for d, z in enumerate(zs_per_layer):
        rows = lambda l, i, d=d: jnp.where(l == d, i, 0)
        in_specs += [pl.BlockSpec((tm, W), lambda l, i, rows=rows: (rows(l, i), Z_AK // W)),
                     pl.BlockSpec((tm, W), lambda l, i, rows=rows: (rows(l, i), Z_AV // W))]
        args += [z, z]
    out = jax.ShapeDtypeStruct((depth, M * ATT_HEADS, ATT_DH), F32)
    return pl.pallas_call(
        functools.partial(_kv_rows_kernel, depth=depth),
        grid=(depth, M // tm),
        in_specs=in_specs,
        out_specs=[pl.BlockSpec((None, tm * ATT_HEADS, ATT_DH), lambda l, i: (l, i, 0))] * 2,
        out_shape=[out, out],
        compiler_params=_cparams(("parallel", "parallel")),
        name="kv_rows",
    )(*args)


def _rope_tables(pos):
    half = RET_DK // 2
    inv = 1.0 / (ROPE_BASE ** jnp.linspace(0.0, 1.0, half, dtype=F32))
    ang = pos.astype(F32)[:, None] * inv
    cos, sin = jnp.cos(ang), jnp.sin(ang)
    return jnp.concatenate([cos, cos], axis=-1), jnp.concatenate([-sin, sin], axis=-1)


def _dn_params(a_log, dt_bias):
    hp = jnp.zeros((SUBLANES, LANES), F32)
    hp = hp.at[0, SM_DNA:SM_DNA + DN_HEADS].set(a_log.astype(F32))
    hp = hp.at[1, SM_DNA:SM_DNA + DN_HEADS].set(dt_bias.astype(F32))
    return hp


def _mix_and_ffn(x, z, oa, ob, oc, sw, l, final, n_outer, tm_merge, tm_up, tn_up, tm_down, ffn_state=None):
    h = _merge_call(oa, ob, oc, z, x, sw['wb'], sw['wo'], l, tm_merge)
    a = _rms_matmul(h, sw['norm_ffn'], sw['w_up'], l, tm_up, tn_up)
    a_raw = a
    if ffn_state is not None:
        DB = ffn_state.shape[0]
        a = a.reshape(DB, SROWS, 2 * D_FF).at[:, S_LO - (FFN_CONV - 1):S_LO].set(ffn_state)
        a = a.reshape(DB * SROWS, 2 * D_FF)
    y = _ffn_down_call(a, h, sw['ffn_conv_w'], sw['wd'], sw['norm_final'], l, n_outer, tm_down, final)
    return y, a_raw


def kernel(x_prompt, x_sample, cache_k, cache_v, cache_kidx, state_dn_conv, state_dn, state_ret,
           state_ffn_conv, page_table, norm_mix, w_in, dn_conv_w, dn_a_log, dn_dt_bias, dn_norm,
           ret_norm, rel_bias, w_branch, w_o, norm_ffn, w_up, ffn_conv_w, w_down, norm_final):
    B, S, D = x_prompt.shape
    DB, DS, _ = x_sample.shape
    depth = w_in.shape[0]
    NP = page_table.shape[1]
    past = NP * PAGE_SIZE
    n_phys = cache_k.shape[1]
    W = ATT_HEADS * ATT_DH
    assert DS == S_HI - S_LO and S % CHUNK == 0 and (DB * SROWS) % CHUNK == 0

    TL = 256 if S % 256 == 0 else CHUNK
    TQ = 256 if S % 256 == 0 else CHUNK
    tm_p = 512 if (B * S) % 512 == 0 else CHUNK
    tm_mm = 2048 if (B * S) % 2048 == 0 else tm_p
    tn_in, tn_up = (1024, 512) if tm_mm == 2048 else (1536, 1408)
    tm_d = 512 if S % 512 == 0 else (256 if S % 256 == 0 else CHUNK)
    MS = DB * SROWS
    NG = MS // CHUNK
    seg_per = CHUNK // SROWS

    xp = x_prompt.reshape(B * S, D)
    xs = jnp.zeros((DB, SROWS, D), F32).at[:, S_LO:S_HI].set(x_sample).reshape(MS, D)
    cos_p, sin_p = _rope_tables(jnp.arange(S))
    pos_s = past + (jnp.arange(CHUNK) % SROWS) - S_LO
    cos_s, sin_s = _rope_tables(pos_s)
    ck = cache_k.reshape(depth, n_phys, PAGE_SIZE * ATT_HEADS, ATT_DH)
    cv = cache_v.reshape(depth, n_phys, PAGE_SIZE * ATT_HEADS, ATT_DH)
    ckidx_t = jnp.swapaxes(cache_kidx, 2, 3)
    rb = rel_bias.astype(F32)
    rbrows = jnp.pad(jnp.repeat(rb.T, SROWS, axis=0), ((0, 0), (0, LANES - N_BUCKETS)))
    G = next(g for g in (8, 4, 2, 1) if NP % g == 0)
    G_idx = next(g for g in (4 * G, 2 * G, G) if NP % g == 0)
    zeros_p = jnp.zeros((B, DN_HEADS, DN_DK, DN_DV), F32)
    topk_p = min(TOPK_MAX, S // 4)
    topk_s = min(TOPK_MAX, (past + DS) // 4)

    sw = dict(w_in=_prep_w_in(w_in), wb=w_branch.astype(BF16), wo=w_o.astype(BF16), w_up=w_up.astype(BF16),
              wd=w_down.astype(BF16), norm_ffn=norm_ffn, ffn_conv_w=ffn_conv_w, norm_final=norm_final)

    p_states, s_states, z_prompt = [], [], []
    for l in range(depth):
        lw = dict(dn_conv_w=dn_conv_w[l], hp=_dn_params(dn_a_log[l], dn_dt_bias[l]),
                  dn_norm=dn_norm[l].reshape(1, DN_DV), ret_norm=ret_norm[l])
        final = l == depth - 1

        z = _rms_matmul(xp, norm_mix, sw['w_in'], l, tm_mm, tn_in)
        oa, dn_s = _dn_call(z, lw['dn_conv_w'], lw['hp'], lw['dn_norm'], zeros_p, B, TL, CHUNK, 0, CHUNK)
        ob, ret_s = _ret_call(z, cos_p, sin_p, lw['ret_norm'], zeros_p, B, TL, CHUNK, 0, CHUNK)
        kb, vt = _kv_cast_call(z, TQ)
        oc = _dsa_prompt_call(rb, z, kb, vt.reshape(B, S // TQ, W, TQ), B, S, TQ, topk_p)
        xp, a_up = _mix_and_ffn(xp, z, oa, ob, oc, sw, l, final, B, tm_p, tm_mm, tn_up, tm_d)
        z3 = z.reshape(B, S, Z_COLS)
        z_prompt.append(z)
        p_states.append((z3[:, S - (DN_CONV - 1):, Z_DNQKV:Z_DNQKV + DN_QKV], dn_s, ret_s, None, None,
                         z3[:, :, Z_SM + SM_IK:Z_SM + SM_IK + IDX_DIM],
                         a_up.reshape(B, S, 2 * D_FF)[:, S - (FFN_CONV - 1):]))

        zs = _rms_matmul(xs, norm_mix, sw['w_in'], l, MS, 1024)
        zs3 = zs.reshape(DB, SROWS, Z_COLS)
        zs_conv = zs3.at[:, S_LO - (DN_CONV - 1):S_LO, Z_DNQKV:Z_DNQKV + DN_QKV].set(state_dn_conv[l])
        zs_conv = zs_conv.reshape(MS, Z_COLS)
        oa, dn_s = _dn_call(zs_conv, lw['dn_conv_w'], lw['hp'], lw['dn_norm'], state_dn[l], NG, CHUNK,
                            SROWS, S_LO, S_HI)
        ob, ret_s = _ret_call(zs, cos_s, sin_s, lw['ret_norm'], state_ret[l], NG, CHUNK, SROWS, S_LO, S_HI)
        keysp, keysn, tp = _dsa_s_index_call(page_table, zs, ckidx_t, l, DB, NP, G_idx, topk_s)
        oc = _dsa_s_attend_call(page_table, rbrows, zs, keysp, keysn, tp, ck, cv, l, DB, NP, G, past)
        xs, a_up = _mix_and_ffn(xs, zs, oa, ob, oc, sw, l, final, 1, MS, MS, 1408, MS,
                                ffn_state=state_ffn_conv[l])
        tok = zs3[:, S_LO:S_HI]
        s_states.append((tok[:, DS - (DN_CONV - 1):, Z_DNQKV:Z_DNQKV + DN_QKV], dn_s, ret_s,
                         tok[:, :, Z_AK:Z_AK + W].reshape(DB, DS, ATT_HEADS, ATT_DH),
                         tok[:, :, Z_AV:Z_AV + W].reshape(DB, DS, ATT_HEADS, ATT_DH),
                         tok[:, :, Z_SM + SM_IK:Z_SM + SM_IK + IDX_DIM],
                         a_up.reshape(DB, SROWS, 2 * D_FF)[:, S_HI - (FFN_CONV - 1):S_HI]))

    y_prompt = xp.reshape(B, S, D)
    y_sample = xs.reshape(DB, SROWS, D)[:, S_LO:S_HI]
    p_k, p_v = (t.reshape(depth, B, S, ATT_HEADS, ATT_DH) for t in _kv_rows_call(z_prompt, tm_p))
    stk = lambda states, i: jnp.stack([st[i] for st in states])
    p_out = [stk(p_states, i) for i in (0, 1, 2)] + [p_k, p_v] + [stk(p_states, i) for i in (5, 6)]
    return (y_prompt, y_sample) + tuple(p_out) + tuple(stk(s_states, i) for i in range(7))
```
